```python
import math
import jax, jax.numpy as jnp
from jax import lax
import numpy as np

D_MODEL = 1024
BATCH = 2
SEQ = 8192
DEPTH = 4
DEC_BATCH = 128
DEC_SEQ = 1
PAST_LEN = 2048
PAGE_SIZE = 128

N_BRANCH = 4
BRANCH_W = D_MODEL // N_BRANCH
HEAD_DIM = 64
N_HEADS = BRANCH_W // HEAD_DIM
SSM_DSTATE = 128
SSM_GROUPS = 2
SSM_CONV = 4
SSM_CHUNK = 128
SSM_CONV_DIM = BRANCH_W + 2 * SSM_GROUPS * SSM_DSTATE
MOBA_BLOCK = 256
MOBA_TOPK = 3
DSA_TOPK = 256
IDX_HEADS = 8
IDX_DIM = 32
RWKV_W_LORA = 64
RWKV_A_LORA = 64
RWKV_G_LORA = 128
RWKV_IN = 3 * BRANCH_W + RWKV_W_LORA + RWKV_A_LORA + RWKV_G_LORA
RWKV_GN_EPS = 64e-5
D_FF = 11 * D_MODEL // 4
FFN_CONV = 3
NUM_BUCKETS = 32
MAX_DISTANCE = 1024
Q_BLOCK = 128
NORM_EPS = 1e-6
IN_SPLITS = (BRANCH_W, SSM_CONV_DIM, N_HEADS,
             BRANCH_W, BRANCH_W, BRANCH_W,
             BRANCH_W, BRANCH_W, BRANCH_W, IDX_HEADS * IDX_DIM, IDX_DIM, IDX_HEADS,
             RWKV_IN, N_BRANCH * D_MODEL)
N_IN = sum(IN_SPLITS)
F32 = jnp.float32

kernel_name = 'hybrid_gated_ssd_moba_dsa_rwkv7_step'


def rms_norm(x, g):
    xf = x.astype(F32)
    y = xf * lax.rsqrt(jnp.mean(xf * xf, axis=-1, keepdims=True) + NORM_EPS)
    return (y * g.astype(F32)).astype(x.dtype)


def split_cols(a, sizes):
    return jnp.split(a, [int(i) for i in np.cumsum(sizes)[:-1]], axis=-1)


def causal_dwconv(x, buf, w, b):
    width = w.shape[0]
    length = x.shape[1]
    xp = jnp.concatenate([buf.astype(x.dtype), x], axis=1)
    y = b + sum(xp[:, j:j + length] * w[j] for j in range(width))
    return y, xp[:, length:]


def rel_bucket(dist):
    n = jnp.maximum(dist, 0)
    max_exact = NUM_BUCKETS // 2
    nf = jnp.maximum(n, 1).astype(F32)
    large = max_exact + (jnp.log(nf / max_exact) / math.log(MAX_DISTANCE / max_exact)
                         * (NUM_BUCKETS - max_exact)).astype(jnp.int32)
    large = jnp.minimum(large, NUM_BUCKETS - 1)
    return jnp.where(n < max_exact, n, large)


def to_chunks(a, nq, qb):
    return a.reshape((a.shape[0], nq, qb) + a.shape[2:]).swapaxes(0, 1)


def gather_pages(pool, page_table):
    rows = pool[page_table]
    return rows.reshape((page_table.shape[0], -1) + pool.shape[2:])


def segsum(a):
    t = a.shape[-1]
    cs = jnp.cumsum(a, axis=-1)
    d = cs[..., :, None] - cs[..., None, :]
    return jnp.where(jnp.tril(jnp.ones((t, t), bool)), d, -jnp.inf)


def ssd_scan(x, dt, a_neg, bm, cm, h0):
    b, l, nh, p = x.shape
    rep = nh // bm.shape[2]
    q = math.gcd(l, SSM_CHUNK)
    c = l // q
    bh = jnp.repeat(bm.astype(F32), rep, axis=2).reshape(b, c, q, nh, -1)
    ch = jnp.repeat(cm.astype(F32), rep, axis=2).reshape(b, c, q, nh, -1)
    xdt = (x.astype(F32) * dt[..., None]).reshape(b, c, q, nh, p)
    da = (dt * a_neg).reshape(b, c, q, nh).transpose(0, 3, 1, 2)
    da_cs = jnp.cumsum(da, axis=-1)
    scores = jnp.einsum('bclhn,bcshn->bhcls', ch, bh) * jnp.exp(segsum(da))
    y_diag = jnp.einsum('bhcls,bcshp->bclhp', scores, xdt)
    decay_states = jnp.exp(da_cs[..., -1:] - da_cs)
    states = jnp.einsum('bclhn,bhcl,bclhp->bchpn', bh, decay_states, xdt)
    states = jnp.concatenate([h0[:, None].astype(F32), states], axis=1)
    chunk_decay = jnp.exp(segsum(jnp.pad(da_cs[..., -1], ((0, 0), (0, 0), (1, 0)))))
    states = jnp.einsum('bhzc,bchpn->bzhpn', chunk_decay, states)
    y_off = jnp.einsum('bclhn,bchpn,bhcl->bclhp', ch, states[:, :-1], jnp.exp(da_cs))
    return (y_diag + y_off).reshape(b, l, nh, p), states[:, -1]


def mamba_mixer(pz, pxbc, pdt, conv_buf, h0, conv_w, conv_b, dt_bias, a_log, d_skip, norm_g):
    b, l, _ = pz.shape
    xbc, new_buf = causal_dwconv(pxbc, conv_buf, conv_w, conv_b)
    xbc = jax.nn.silu(xbc)
    xs, bm, cm = jnp.split(xbc, [BRANCH_W, BRANCH_W + SSM_GROUPS * SSM_DSTATE], axis=-1)
    xs = xs.reshape(b, l, N_HEADS, HEAD_DIM)
    bm = bm.reshape(b, l, SSM_GROUPS, SSM_DSTATE)
    cm = cm.reshape(b, l, SSM_GROUPS, SSM_DSTATE)
    dt = jax.nn.softplus(pdt.astype(F32) + dt_bias.astype(F32))
    y, h_new = ssd_scan(xs, dt, -jnp.exp(a_log.astype(F32)), bm, cm, h0)
    y = y + d_skip.astype(F32)[:, None] * xs.astype(F32)
    y = rms_norm(y.reshape(b, l, BRANCH_W) * jax.nn.silu(pz.astype(F32)), norm_g)
    return y.astype(pz.dtype), new_buf, h_new.astype(h0.dtype)


def moba_attention(q, k, v, q_pos, tab):
    b, t, nh, hd = q.shape
    length = k.shape[1]
    nb = -(-length // MOBA_BLOCK)
    pad = nb * MOBA_BLOCK - length

    def blocks(a):
        a = jnp.pad(a.astype(F32), ((0, 0), (0, pad), (0, 0), (0, 0)))
        return a.reshape(b, nb, MOBA_BLOCK, nh, hd).transpose(0, 3, 1, 2, 4)

    kb, vb = blocks(k), blocks(v)
    kmean = kb.mean(axis=3)
    n_sel = min(MOBA_TOPK, nb)
    hidx = jnp.arange(nh)[None, :, None, None, None]
    tab_t = tab.T
    take = jax.vmap(jax.vmap(lambda a, i: a[i]))

    def chunk(args):
        qc, pc = args
        qc = qc.astype(F32) * hd ** -0.5
        own = pc // MOBA_BLOCK
        score = jnp.einsum('bqhd,bhnd->bhqn', qc, kmean)
        score = jnp.where(jnp.arange(nb)[None, :] < own[:, None], score, -jnp.inf)
        _, top = lax.top_k(score, n_sel)
        blk = jnp.concatenate([top, jnp.broadcast_to(own[None, None, :, None], top.shape[:3] + (1,))], axis=-1)
        blk_ok = jnp.concatenate([jnp.arange(n_sel)[None, :] < own[:, None],
                                  jnp.ones((pc.shape[0], 1), bool)], axis=-1)
        kg, vg = take(kb, blk), take(vb, blk)
        kpos = blk[..., None] * MOBA_BLOCK + jnp.arange(MOBA_BLOCK)
        qp = pc[None, None, :, None, None]
        logits = jnp.einsum('bqhd,bhqjsd->bhqjs', qc, kg) + tab_t[hidx, rel_bucket(qp - kpos)]
        logits = jnp.where(blk_ok[None, None, :, :, None] & (kpos <= qp), logits, -jnp.inf)
        sh = logits.shape
        prob = jax.nn.softmax(logits.reshape(sh[:3] + (-1,)), axis=-1).reshape(sh)
        return jnp.einsum('bhqjs,bhqjsd->bqhd', prob, vg)

    qb = Q_BLOCK if t % Q_BLOCK == 0 else t
    nq = t // qb
    out = lax.map(chunk, (to_chunks(q, nq, qb), q_pos.reshape(nq, qb)))
    return out.swapaxes(0, 1).reshape(b, t, nh * hd).astype(q.dtype)


def dsa_attention(q, k, v, qi, ki, wi, q_pos, tab):
    b, t, nh, hd = q.shape
    length = k.shape[1]
    n_keep = min(DSA_TOPK, length // 4)
    kf, vf, kif = k.astype(F32), v.astype(F32), ki.astype(F32)
    kpos_all = jnp.arange(length)
    take = jax.vmap(lambda a, i: a[i])

    def chunk(args):
        qc, qic, wic, pc = args
        s = jax.nn.relu(jnp.einsum('bqhe,bse->bqhs', qic.astype(F32) * IDX_DIM ** -0.5, kif))
        score = jnp.einsum('bqh,bqhs->bqs', wic.astype(F32) * IDX_HEADS ** -0.5, s)
        score = jnp.where(kpos_all[None, None, :] <= pc[None, :, None], score, -jnp.inf)
        _, idx = lax.top_k(score, n_keep)
        kg, vg = take(kf, idx), take(vf, idx)
        bias = tab[rel_bucket(pc[None, :, None] - idx)].transpose(0, 1, 3, 2)
        logits = jnp.einsum('bqhd,bqjhd->bqhj', qc.astype(F32) * hd ** -0.5, kg) + bias
        logits = jnp.where((idx <= pc[None, :, None])[:, :, None, :], logits, -jnp.inf)
        return jnp.einsum('bqhj,bqjhd->bqhd', jax.nn.softmax(logits, axis=-1), vg)

    qb = Q_BLOCK if t % Q_BLOCK == 0 else t
    nq = t // qb
    out = lax.map(chunk, (to_chunks(q, nq, qb), to_chunks(qi, nq, qb), to_chunks(wi, nq, qb),
                          q_pos.reshape(nq, qb)))
    return out.swapaxes(0, 1).reshape(b, t, nh * hd).astype(q.dtype)


def rwkv_mixer(p, shift_prev, s0, mu, w0, w2, a0, a2, g2, k_k, k_a, r_k, ln_w, ln_b):
    b, l, _ = p.shape
    p_prev = jnp.concatenate([shift_prev[:, None].astype(p.dtype), p[:, :-1]], axis=1)
    pm = (p + (p_prev - p) * mu).astype(F32)
    r, k, v, xw, xa, xg = split_cols(pm, (BRANCH_W, BRANCH_W, BRANCH_W, RWKV_W_LORA, RWKV_A_LORA, RWKV_G_LORA))
    logw = -jax.nn.softplus(-(w0 + jnp.tanh(xw) @ w2)) - 0.5
    decay = jnp.exp(-jnp.exp(logw))
    a = jax.nn.sigmoid(a0 + xa @ a2)
    g = jax.nn.sigmoid(xg) @ g2
    heads = lambda z: z.reshape(b, l, N_HEADS, HEAD_DIM)
    kk = heads(k * k_k)
    kk = kk * lax.rsqrt(jnp.maximum(jnp.sum(kk * kk, axis=-1, keepdims=True), 1e-24))
    k = k * (1.0 + (a - 1.0) * k_a)
    r_h, k_h, v_h, w_h, a_h = heads(r), heads(k), heads(v), heads(decay), heads(a)

    def step(s, inp):
        rt, wt, kt, vt, kkt, at = inp
        sa = jnp.einsum('bhvk,bhk->bhv', s, -kkt)
        s = s * wt[:, :, None, :] + sa[..., None] * (kkt * at)[:, :, None, :] + vt[..., None] * kt[:, :, None, :]
        return s, jnp.einsum('bhvk,bhk->bhv', s, rt)

    xs = tuple(z.transpose(1, 0, 2, 3) for z in (r_h, w_h, k_h, v_h, kk, a_h))
    s_new, o = lax.scan(step, s0.astype(F32), xs)
    o = o.transpose(1, 0, 2, 3)
    mean = jnp.mean(o, axis=-1, keepdims=True)
    var = jnp.mean(jnp.square(o - mean), axis=-1, keepdims=True)
    o = ((o - mean) * lax.rsqrt(var + RWKV_GN_EPS)).reshape(b, l, BRANCH_W) * ln_w + ln_b
    bonus = jnp.sum(r_h * k_h * r_k, axis=-1, keepdims=True) * v_h
    o = (o + bonus.reshape(b, l, BRANCH_W)) * g
    return o.astype(p.dtype), p[:, -1], s_new.astype(s0.dtype)


def conv_ffn(h, buf, w_up, conv_w, conv_b, w_down):
    a, bv = jnp.split(h @ w_up, 2, axis=-1)
    a, new_buf = causal_dwconv(a, buf, conv_w, conv_b)
    return (jax.nn.silu(a) * bv) @ w_down, new_buf


def layer(x, past, pos0, norm1_g, w_in, ssm_conv_w, ssm_conv_b, ssm_dt_bias, ssm_A_log, ssm_D, ssm_norm_g,
          rwkv_mu, rwkv_w0, rwkv_w2, rwkv_a0, rwkv_a2, rwkv_g2, rwkv_k_k, rwkv_k_a, rwkv_r_k, rwkv_ln_w, rwkv_ln_b,
          w_branch, w_out, norm2_g, ffn_w_up, ffn_conv_w, ffn_conv_b, ffn_w_down, rel_bias):
    (mk_past, mv_past, dk_past, dv_past, dki_past, ssm_h, ssm_buf, rw_s, rw_shift, ffn_buf) = past
    b, t, _ = x.shape
    q_pos = pos0 + jnp.arange(t, dtype=jnp.int32)
    heads = lambda z: z.reshape(b, t, N_HEADS, HEAD_DIM)
    cat = lambda old, new: jnp.concatenate([old.astype(new.dtype), new], axis=1)
    h = rms_norm(x, norm1_g)
    (s_z, s_xbc, s_dt, m_q, m_k, m_v, d_q, d_k, d_v, d_qi, d_ki, d_wi, p_rwkv, p_gate) = split_cols(h @ w_in, IN_SPLITS)
    o_a, ssm_buf_new, ssm_h_new = mamba_mixer(s_z, s_xbc, s_dt, ssm_buf, ssm_h, ssm_conv_w, ssm_conv_b,
                                              ssm_dt_bias, ssm_A_log, ssm_D, ssm_norm_g)
    m_k, m_v = heads(m_k), heads(m_v)
    o_b = moba_attention(heads(m_q), cat(mk_past, m_k), cat(mv_past, m_v), q_pos, rel_bias[:, :N_HEADS])
    d_k, d_v = heads(d_k), heads(d_v)
    o_c = dsa_attention(heads(d_q), cat(dk_past, d_k), cat(dv_past, d_v), d_qi.reshape(b, t, IDX_HEADS, IDX_DIM),
                        cat(dki_past, d_ki), d_wi, q_pos, rel_bias[:, N_HEADS:])
    o_d, rw_shift_new, rw_s_new = rwkv_mixer(p_rwkv, rw_shift, rw_s, rwkv_mu, rwkv_w0, rwkv_w2, rwkv_a0, rwkv_a2,
                                             rwkv_g2, rwkv_k_k, rwkv_k_a, rwkv_r_k, rwkv_ln_w, rwkv_ln_b)
    branches = jnp.stack([o_a, o_b, o_c, o_d], axis=2)
    proj = jnp.einsum('btnw,nwd->btnd', branches, w_branch)
    gates = jax.nn.sigmoid(p_gate.reshape(b, t, N_BRANCH, D_MODEL))
    x = x + jnp.sum(gates * proj, axis=2) @ w_out
    f, ffn_buf_new = conv_ffn(rms_norm(x, norm2_g), ffn_buf, ffn_w_up, ffn_conv_w, ffn_conv_b, ffn_w_down)
    x = x + f
    return x, (m_k, m_v, d_k, d_v, d_ki, ssm_h_new, ssm_buf_new, rw_s_new, rw_shift_new, ffn_buf_new)


def setup_inputs(seed: int = 0) -> dict:
    key = jax.random.key(seed)
    keys = jax.random.split(key, 48)
    cnt = iter(range(48))
    nk = lambda: keys[next(cnt)]

    def nrm(shape, scale):
        return scale * jax.random.normal(nk(), shape, jnp.float32)

    def unif(shape, lo, hi):
        return jax.random.uniform(nk(), shape, jnp.float32, lo, hi)

    n_pages = PAST_LEN // PAGE_SIZE
    n_used = DEC_BATCH * n_pages
    n_pool = n_used + max(1, n_used // 4)
    kv_shape = (DEPTH, n_pool, PAGE_SIZE, N_HEADS, HEAD_DIM)
    x_prompt = nrm((BATCH, SEQ, D_MODEL), 1.0)
    x_sample = nrm((DEC_BATCH, DEC_SEQ, D_MODEL), 1.0)
    cache_moba_k = nrm(kv_shape, 1.0)
    cache_moba_v = nrm(kv_shape, 1.0)
    cache_dsa_k = nrm(kv_shape, 1.0)
    cache_dsa_v = nrm(kv_shape, 1.0)
    cache_dsa_kidx = nrm((DEPTH, n_pool, PAGE_SIZE, IDX_DIM), 1.0)
    page_table = jax.random.permutation(nk(), n_pool)[:n_used].reshape(DEC_BATCH, n_pages).astype(jnp.int32)
    dt0 = jnp.exp(unif((DEPTH, N_HEADS), math.log(1e-3), math.log(1e-1)))
    return {
        'x_prompt': x_prompt,
        'x_sample': x_sample,
        'cache_moba_k': cache_moba_k,
        'cache_moba_v': cache_moba_v,
        'cache_dsa_k': cache_dsa_k,
        'cache_dsa_v': cache_dsa_v,
        'cache_dsa_kidx': cache_dsa_kidx,
        'page_table': page_table,
        'state_ssm': nrm((DEPTH, DEC_BATCH, N_HEADS, HEAD_DIM, SSM_DSTATE), 0.5),
        'state_ssm_conv': nrm((DEPTH, DEC_BATCH, SSM_CONV - 1, SSM_CONV_DIM), 1.0),
        'state_rwkv': nrm((DEPTH, DEC_BATCH, N_HEADS, HEAD_DIM, HEAD_DIM), 0.5),
        'state_rwkv_shift': nrm((DEPTH, DEC_BATCH, RWKV_IN), 1.0),
        'state_ffn_conv': nrm((DEPTH, DEC_BATCH, FFN_CONV - 1, D_FF), 1.0),
        'norm1_g': 1.0 + nrm((DEPTH, D_MODEL), 0.02),
        'w_in': nrm((DEPTH, D_MODEL, N_IN), D_MODEL ** -0.5),
        'ssm_conv_w': nrm((DEPTH, SSM_CONV, SSM_CONV_DIM), SSM_CONV ** -0.5),
        'ssm_conv_b': nrm((DEPTH, SSM_CONV_DIM), 0.01),
        'ssm_dt_bias': dt0 + jnp.log(-jnp.expm1(-dt0)),
        'ssm_A_log': jnp.log(unif((DEPTH, N_HEADS), 1.0, 16.0)),
        'ssm_D': 1.0 + nrm((DEPTH, N_HEADS), 0.1),
        'ssm_norm_g': 1.0 + nrm((DEPTH, BRANCH_W), 0.02),
        'rwkv_mu': unif((DEPTH, RWKV_IN), 0.0, 1.0),
        'rwkv_w0': unif((DEPTH, BRANCH_W), -6.0, 1.0),
        'rwkv_w2': nrm((DEPTH, RWKV_W_LORA, BRANCH_W), 0.1 * RWKV_W_LORA ** -0.5),
        'rwkv_a0': nrm((DEPTH, BRANCH_W), 0.1),
        'rwkv_a2': nrm((DEPTH, RWKV_A_LORA, BRANCH_W), 0.1 * RWKV_A_LORA ** -0.5),
        'rwkv_g2': nrm((DEPTH, RWKV_G_LORA, BRANCH_W), RWKV_G_LORA ** -0.5),
        'rwkv_k_k': 0.85 + nrm((DEPTH, BRANCH_W), 0.02),
        'rwkv_k_a': 1.0 + nrm((DEPTH, BRANCH_W), 0.02),
        'rwkv_r_k': nrm((DEPTH, N_HEADS, HEAD_DIM), 0.1),
        'rwkv_ln_w': 1.0 + nrm((DEPTH, BRANCH_W), 0.02),
        'rwkv_ln_b': nrm((DEPTH, BRANCH_W), 0.01),
        'w_branch': nrm((DEPTH, N_BRANCH, BRANCH_W, D_MODEL), BRANCH_W ** -0.5),
        'w_out': nrm((DEPTH, D_MODEL, D_MODEL), 0.5 * D_MODEL ** -0.5),
        'norm2_g': 1.0 + nrm((DEPTH, D_MODEL), 0.02),
        'ffn_w_up': nrm((DEPTH, D_MODEL, 2 * D_FF), D_MODEL ** -0.5),
        'ffn_conv_w': nrm((DEPTH, FFN_CONV, D_FF), FFN_CONV ** -0.5),
        'ffn_conv_b': nrm((DEPTH, D_FF), 0.01),
        'ffn_w_down': nrm((DEPTH, D_FF, D_MODEL), 0.5 * D_FF ** -0.5),
        'rel_bias': nrm((NUM_BUCKETS, 2 * N_HEADS), 0.5),
        'final_g': 1.0 + nrm((D_MODEL,), 0.02),
    }


def reference(x_prompt, x_sample, cache_moba_k, cache_moba_v, cache_dsa_k, cache_dsa_v, cache_dsa_kidx, page_table,
              state_ssm, state_ssm_conv, state_rwkv, state_rwkv_shift, state_ffn_conv,
              norm1_g, w_in, ssm_conv_w, ssm_conv_b, ssm_dt_bias, ssm_A_log, ssm_D, ssm_norm_g,
              rwkv_mu, rwkv_w0, rwkv_w2, rwkv_a0, rwkv_a2, rwkv_g2, rwkv_k_k, rwkv_k_a, rwkv_r_k, rwkv_ln_w, rwkv_ln_b,
              w_branch, w_out, norm2_g, ffn_w_up, ffn_conv_w, ffn_conv_b, ffn_w_down, rel_bias, final_g):
    bp = x_prompt.shape[0]
    dt = x_prompt.dtype
    past_len = page_table.shape[1] * cache_moba_k.shape[2]
    zp = lambda *shape: jnp.zeros((bp,) + shape, dt)
    prompt_past = (zp(0, N_HEADS, HEAD_DIM), zp(0, N_HEADS, HEAD_DIM), zp(0, N_HEADS, HEAD_DIM),
                   zp(0, N_HEADS, HEAD_DIM), zp(0, IDX_DIM),
                   zp(N_HEADS, HEAD_DIM, SSM_DSTATE), zp(SSM_CONV - 1, SSM_CONV_DIM),
                   zp(N_HEADS, HEAD_DIM, HEAD_DIM), zp(RWKV_IN), zp(FFN_CONV - 1, D_FF))
    xp, xs = x_prompt, x_sample
    new_p, new_s = [], []
    for l in range(DEPTH):
        lp = (norm1_g[l], w_in[l], ssm_conv_w[l], ssm_conv_b[l], ssm_dt_bias[l], ssm_A_log[l], ssm_D[l], ssm_norm_g[l],
              rwkv_mu[l], rwkv_w0[l], rwkv_w2[l], rwkv_a0[l], rwkv_a2[l], rwkv_g2[l], rwkv_k_k[l], rwkv_k_a[l],
              rwkv_r_k[l], rwkv_ln_w[l], rwkv_ln_b[l], w_branch[l], w_out[l], norm2_g[l],
              ffn_w_up[l], ffn_conv_w[l], ffn_conv_b[l], ffn_w_down[l], rel_bias)
        sample_past = (gather_pages(cache_moba_k[l], page_table), gather_pages(cache_moba_v[l], page_table),
                       gather_pages(cache_dsa_k[l], page_table), gather_pages(cache_dsa_v[l], page_table),
                       gather_pages(cache_dsa_kidx[l], page_table),
                       state_ssm[l], state_ssm_conv[l], state_rwkv[l], state_rwkv_shift[l], state_ffn_conv[l])
        xp, np_l = layer(xp, prompt_past, 0, *lp)
        xs, ns_l = layer(xs, sample_past, past_len, *lp)
        new_p.append(np_l)
        new_s.append(ns_l)
    y_prompt = rms_norm(xp, final_g)
    y_sample = rms_norm(xs, final_g)
    stk = lambda outs, i: jnp.stack([o[i] for o in outs], axis=0)
    return (y_prompt, y_sample,
            stk(new_p, 0), stk(new_s, 0),
            stk(new_p, 1), stk(new_s, 1),
            stk(new_p, 2), stk(new_s, 2),
            stk(new_p, 3), stk(new_s, 3),
            stk(new_p, 4), stk(new_s, 4),
            stk(new_p, 5), stk(new_s, 5),
            stk(new_p, 6), stk(new_s, 6),
            stk(new_p, 7), stk(new_s, 7),
            stk(new_p, 8), stk(new_s, 8),
            stk(new_p, 9), stk(new_s, 9))
```

```python
import functools
import math

import jax
import jax.numpy as jnp
import numpy as np
from jax import lax
from jax.experimental import pallas as pl
from jax.experimental.pallas import tpu as pltpu

D_MODEL = 1024
N_BRANCH = 4
BRANCH_W = D_MODEL // N_BRANCH
HEAD_DIM = 64
N_HEADS = BRANCH_W // HEAD_DIM
SSM_DSTATE = 128
SSM_GROUPS = 2
SSM_CONV = 4
SSM_CHUNK = 128
SSM_CONV_DIM = BRANCH_W + 2 * SSM_GROUPS * SSM_DSTATE
MOBA_BLOCK = 256
MOBA_TOPK = 3
DSA_TOPK = 256
IDX_HEADS = 8
IDX_DIM = 32
RWKV_W_LORA = 64
RWKV_A_LORA = 64
RWKV_G_LORA = 128
RWKV_IN = 3 * BRANCH_W + RWKV_W_LORA + RWKV_A_LORA + RWKV_G_LORA
RWKV_GN_EPS = 64e-5
D_FF = 11 * D_MODEL // 4
FFN_CONV = 3
NUM_BUCKETS = 32
MAX_DISTANCE = 1024
NORM_EPS = 1e-6
IN_SPLITS = (BRANCH_W, SSM_CONV_DIM, N_HEADS,
             BRANCH_W, BRANCH_W, BRANCH_W,
             BRANCH_W, BRANCH_W, BRANCH_W, IDX_HEADS * IDX_DIM, IDX_DIM, IDX_HEADS,
             RWKV_IN, N_BRANCH * D_MODEL)
F32 = jnp.float32
BF16 = jnp.bfloat16

LANES = 128
SUBLANES = 8
VMEM_LIMIT = 56 * 1024 * 1024
NEG_BIG = -1e30
Q_TILE = 128
K_TILE = MOBA_BLOCK


def _cparams(*sem):
    return pltpu.CompilerParams(dimension_semantics=sem, vmem_limit_bytes=VMEM_LIMIT)


def _mm_kernel(x_ref, w_ref, o_ref):
    o_ref[...] = jnp.dot(x_ref[...], w_ref[...], preferred_element_type=F32)


def _pick_tile(n, cap, quantum):
    if n <= cap:
        return n
    best = None
    for t in range(quantum, cap + 1, quantum):
        if n % t == 0:
            best = t
    assert best is not None, (n, cap, quantum)
    return best


def matmul(x, w, *, tm_cap=512, tn_cap=1024):
    m, k = x.shape
    k2, n = w.shape
    assert k == k2
    x = x.astype(BF16)
    w = w.astype(BF16)
    tm = _pick_tile(m, tm_cap, SUBLANES * 2)
    tn = _pick_tile(n, tn_cap, LANES)
    return pl.pallas_call(
        _mm_kernel,
        grid=(n // tn, m // tm),
        in_specs=[pl.BlockSpec((tm, k), lambda j, i: (i, 0)),
                  pl.BlockSpec((k, tn), lambda j, i: (0, j))],
        out_specs=pl.BlockSpec((tm, tn), lambda j, i: (i, j)),
        out_shape=jax.ShapeDtypeStruct((m, n), F32),
        compiler_params=_cparams("parallel", "parallel"),
        name="matmul",
    )(x, w)


def _rel_bucket(dist):
    n = jnp.maximum(dist, 0)
    max_exact = NUM_BUCKETS // 2
    nf = jnp.maximum(n, 1).astype(F32)
    large = max_exact + (jnp.log(nf / max_exact) / math.log(MAX_DISTANCE / max_exact)
                         * (NUM_BUCKETS - max_exact)).astype(jnp.int32)
    large = jnp.minimum(large, NUM_BUCKETS - 1)
    return jnp.where(n < max_exact, n, large)


N_NEAR = -(-(MAX_DISTANCE + K_TILE - 1) // Q_TILE)


def _bias_tiles(tab):
    o = jnp.arange(N_NEAR + 1, dtype=jnp.int32)[:, None, None] * Q_TILE
    j = jnp.arange(K_TILE, dtype=jnp.int32)[None, :, None]
    i = jnp.arange(Q_TILE, dtype=jnp.int32)[None, None, :]
    dist = jnp.where(o >= N_NEAR * Q_TILE, MAX_DISTANCE, o + i - j)
    return jnp.transpose(tab[_rel_bucket(dist)], (3, 0, 1, 2))


def _flash_tile(h, k_ref, vt_ref, bias_ref, qpad_h, k0, off_idx, extra_mask, carry):
    m, l, acc = carry
    pair = h // 2
    kt = k_ref[pl.ds(k0, K_TILE), pair * LANES:(pair + 1) * LANES]
    s = jnp.dot(kt, qpad_h, preferred_element_type=F32)
    s = s + bias_ref[h, off_idx] + extra_mask
    m_new = jnp.maximum(m, jnp.max(s, axis=0, keepdims=True))
    alpha = jnp.exp(m - m_new)
    p = jnp.exp(s - m_new)
    l = alpha * l + jnp.sum(p, axis=0, keepdims=True)
    vt = vt_ref[h * HEAD_DIM:(h + 1) * HEAD_DIM, pl.ds(k0, K_TILE)]
    acc = alpha * acc + jnp.dot(vt, p.astype(BF16), preferred_element_type=F32)
    return m_new, l, acc


def _causal_mask(q0, k0):
    row = lax.broadcasted_iota(jnp.int32, (K_TILE, Q_TILE), 0)
    lane = lax.broadcasted_iota(jnp.int32, (K_TILE, Q_TILE), 1)
    return jnp.where(k0 + row <= q0 + lane, 0.0, NEG_BIG).astype(F32)


def _flash_init():
    return (jnp.full((1, Q_TILE), NEG_BIG, F32), jnp.zeros((1, Q_TILE), F32),
            jnp.zeros((HEAD_DIM, Q_TILE), F32))


def _top_blocks_mask(score, n_valid, n_sel, axis):
    n = score.shape[axis]
    idx = lax.broadcasted_iota(jnp.int32, score.shape, axis)
    low = jnp.float32(-3.0e38)
    s = jnp.where(idx < n_valid, score, low)
    mask = jnp.full(score.shape, NEG_BIG, F32)
    for _ in range(n_sel):
        mx = jnp.max(s, axis=axis, keepdims=True)
        first = jnp.min(jnp.where(s == mx, idx, n), axis=axis, keepdims=True)
        pick = idx == jnp.where(mx > low, first, -1)
        mask = jnp.where(pick, 0.0, mask)
        s = jnp.where(pick, low, s)
    return mask


def _moba_prompt_kernel(q_ref, k_ref, vt_ref, kmean_ref, bias_ref, o_ref, sel_scr):
    qi = pl.program_id(1)
    q0 = qi * Q_TILE
    own = q0 // K_TILE
    nb = kmean_ref.shape[1]
    for h in range(N_HEADS):
        score = jnp.dot(kmean_ref[h], q_ref[h], preferred_element_type=F32,
                        precision=lax.Precision.HIGHEST)
        sel_scr[h] = _top_blocks_mask(score, own, min(MOBA_TOPK, nb), 0)
    qb = [q_ref[h].astype(BF16) for h in range(N_HEADS)]

    def body(kj, carry):
        k0 = pl.multiple_of(kj * K_TILE, K_TILE)
        off = jnp.minimum((q0 - k0) // Q_TILE, N_NEAR)
        return tuple(_flash_tile(h, k_ref, vt_ref, bias_ref, qb[h], k0, off,
                                 sel_scr[h, pl.ds(kj, 1), :], carry[h]) for h in range(N_HEADS))

    carry = lax.fori_loop(0, own, body, tuple(_flash_init() for _ in range(N_HEADS)))
    k0 = pl.multiple_of(own * K_TILE, K_TILE)
    cmask = _causal_mask(q0, k0)
    for h in range(N_HEADS):
        m, l, acc = _flash_tile(h, k_ref, vt_ref, bias_ref, qb[h], k0, (q0 - k0) // Q_TILE, cmask, carry[h])
        o_ref[h * HEAD_DIM:(h + 1) * HEAD_DIM, :] = acc / l


def _pad_heads_t(q):
    b, t, _ = q.shape
    qt = jnp.transpose(q.reshape(b, t, N_HEADS, HEAD_DIM), (0, 2, 3, 1))
    z = jnp.zeros_like(qt)
    even = jnp.concatenate([qt, z], axis=2)
    odd = jnp.concatenate([z, qt], axis=2)
    is_even = (jnp.arange(N_HEADS) % 2 == 0)[None, :, None, None]
    return jnp.where(is_even, even, odd)


def _block_mean_kernel(k_ref, o_ref):
    o_ref[...] = jnp.mean(k_ref[...], axis=0, keepdims=True)


def block_mean(k):
    b, t, c = k.shape
    nb = t // MOBA_BLOCK
    out = pl.pallas_call(
        _block_mean_kernel,
        grid=(b, nb),
        in_specs=[pl.BlockSpec((None, MOBA_BLOCK, c), lambda i, j: (i, j, 0))],
        out_specs=pl.BlockSpec((None, None, 1, c), lambda i, j: (i, j, 0, 0)),
        out_shape=jax.ShapeDtypeStruct((b, nb, 1, c), F32),
        compiler_params=_cparams("parallel", "parallel"),
        name="block_mean",
    )(k)
    return out.reshape(b, nb, c)


def moba_prompt(q, k, v, bias_tiles):
    b, t, w = q.shape
    assert t % K_TILE == 0 and K_TILE % Q_TILE == 0
    nb = t // MOBA_BLOCK
    qp = _pad_heads_t(q * HEAD_DIM ** -0.5)
    kmean = block_mean(k)
    kmp = jnp.transpose(_pad_heads_t(kmean), (0, 1, 3, 2))
    kb = k.astype(BF16)
    vtb = jnp.transpose(v, (0, 2, 1)).astype(BF16)
    ot = pl.pallas_call(
        _moba_prompt_kernel,
        grid=(b, t // Q_TILE),
        in_specs=[pl.BlockSpec((None, N_HEADS, LANES, Q_TILE), lambda i, j: (i, 0, 0, j)),
                  pl.BlockSpec((None, t, w), lambda i, j: (i, 0, 0)),
                  pl.BlockSpec((None, w, t), lambda i, j: (i, 0, 0)),
                  pl.BlockSpec((None, N_HEADS, nb, LANES), lambda i, j: (i, 0, 0, 0)),
                  pl.BlockSpec(bias_tiles.shape, lambda i, j: (0, 0, 0, 0))],
        out_specs=pl.BlockSpec((None, w, Q_TILE), lambda i, j: (i, 0, j)),
        out_shape=jax.ShapeDtypeStruct((b, w, t), F32),
        scratch_shapes=[pltpu.VMEM((N_HEADS, nb, Q_TILE), F32)],
        compiler_params=_cparams("parallel", "arbitrary"),
        name="moba_prompt",
    )(qp, kb, vtb, kmp, bias_tiles)
    return jnp.transpose(ot, (0, 2, 1))


def _split_bf16(x):
    hi = x.astype(BF16)
    lo = (x - hi.astype(F32)).astype(BF16)
    return hi, lo


def _ordered_key(x):
    x = jnp.where(x == 0.0, 0.0, x)
    bits = lax.bitcast_convert_type(x, jnp.int32)
    return bits ^ ((bits >> 31) & jnp.int32(0x7FFFFFFF))


INT_MIN = -2 ** 31


def _kth_largest_key(count_ge, kcount):
    int_min = jnp.int32(INT_MIN)
    thr = jnp.where(count_ge(jnp.zeros_like(kcount, jnp.int32)) >= kcount, jnp.int32(0), int_min)

    def bit_body(i, thr):
        cand = thr | lax.shift_left(jnp.int32(1), 30 - i)
        return jnp.where(count_ge(cand) >= kcount, cand, thr)

    return lax.fori_loop(0, 31, bit_body, thr)


def _dsa_prompt_kernel(q_ref, k_ref, vt_ref, kic_ref, qic_ref, wi_ref, bias_ref, o_ref,
                       key_scr, mask_scr, *, n_keep):
    qi = pl.program_id(1)
    q0 = qi * Q_TILE
    own = q0 // K_TILE
    n_tiles = own + 1
    qpos = q0 + lax.broadcasted_iota(jnp.int32, (1, Q_TILE), 1)
    row = lax.broadcasted_iota(jnp.int32, (K_TILE, Q_TILE), 0)

    def tile_start(kj):
        return pl.multiple_of(kj * K_TILE, K_TILE)

    def score_body(kj, _):
        k0 = tile_start(kj)
        kic = kic_ref[pl.ds(k0, K_TILE), :]
        sc = jnp.zeros((K_TILE, Q_TILE), F32)
        for h in range(IDX_HEADS):
            d = jnp.dot(kic, qic_ref[h], preferred_element_type=F32)
            sc = sc + wi_ref[h:h + 1, :] * jnp.maximum(d, 0.0)
        key = jnp.where(k0 + row <= qpos, _ordered_key(sc), jnp.int32(INT_MIN))
        key_scr[pl.ds(k0, K_TILE), :] = key
        return 0

    lax.fori_loop(0, n_tiles, score_body, 0)

    def count(pred):
        def body(kj, c):
            t = key_scr[pl.ds(tile_start(kj), K_TILE), :]
            return c + jnp.sum(jnp.where(pred(t), 1.0, 0.0), axis=0, keepdims=True)
        return lax.fori_loop(0, n_tiles, body, jnp.zeros((1, Q_TILE), F32))

    kcount = jnp.minimum(n_keep, qpos + 1).astype(F32)
    thr = _kth_largest_key(lambda c: count(lambda t: t >= c), kcount)
    need = kcount - count(lambda t: t > thr)
    tri = jnp.where(lax.broadcasted_iota(jnp.int32, (K_TILE, K_TILE), 0)
                    >= lax.broadcasted_iota(jnp.int32, (K_TILE, K_TILE), 1), 1.0, 0.0).astype(BF16)

    def mask_body(kj, seen):
        k0 = tile_start(kj)
        t = key_scr[pl.ds(k0, K_TILE), :]
        eq = jnp.where(t == thr, 1.0, 0.0)
        rank = jnp.dot(tri, eq.astype(BF16), preferred_element_type=F32) + seen
        keep = jnp.where(t > thr, 1.0, jnp.where(rank <= need, eq, 0.0))
        mask_scr[pl.ds(k0, K_TILE), :] = jnp.where(keep > 0.5, 0.0, NEG_BIG).astype(BF16)
        return seen + jnp.sum(eq, axis=0, keepdims=True)

    lax.fori_loop(0, n_tiles, mask_body, jnp.zeros((1, Q_TILE), F32))
    qb = [q_ref[h].astype(BF16) for h in range(N_HEADS)]

    def body(kj, carry):
        k0 = tile_start(kj)
        off = jnp.minimum((q0 - k0) // Q_TILE, N_NEAR)
        extra = mask_scr[pl.ds(k0, K_TILE), :].astype(F32)
        return tuple(_flash_tile(h, k_ref, vt_ref, bias_ref, qb[h], k0, off, extra, carry[h])
                     for h in range(N_HEADS))

    carry = lax.fori_loop(0, n_tiles, body, tuple(_flash_init() for _ in range(N_HEADS)))
    for h in range(N_HEADS):
        m, l, acc = carry[h]
        o_ref[h * HEAD_DIM:(h + 1) * HEAD_DIM, :] = acc / l


def dsa_prompt(q, k, v, qi, ki, wi, bias_tiles):
    b, t, w = q.shape
    assert t % K_TILE == 0 and K_TILE % Q_TILE == 0
    n_keep = min(DSA_TOPK, t // 4)
    qp = _pad_heads_t(q * HEAD_DIM ** -0.5)
    kb = k.astype(BF16)
    vtb = jnp.transpose(v, (0, 2, 1)).astype(BF16)
    ki_hi, ki_lo = _split_bf16(ki)
    kic = jnp.concatenate([ki_hi, ki_hi, ki_lo], axis=-1)
    qis = jnp.transpose((qi * IDX_DIM ** -0.5).reshape(b, t, IDX_HEADS, IDX_DIM), (0, 2, 3, 1))
    qi_hi, qi_lo = _split_bf16(qis)
    qic = jnp.concatenate([qi_hi, qi_lo, qi_hi], axis=2)
    wit = jnp.transpose(wi * IDX_HEADS ** -0.5, (0, 2, 1))
    ot = pl.pallas_call(
        functools.partial(_dsa_prompt_kernel, n_keep=n_keep),
        grid=(b, t // Q_TILE),
        in_specs=[pl.BlockSpec((None, N_HEADS, LANES, Q_TILE), lambda i, j: (i, 0, 0, j)),
                  pl.BlockSpec((None, t, w), lambda i, j: (i, 0, 0)),
                  pl.BlockSpec((None, w, t), lambda i, j: (i, 0, 0)),
                  pl.BlockSpec((None, t, 3 * IDX_DIM), lambda i, j: (i, 0, 0)),
                  pl.BlockSpec((None, IDX_HEADS, 3 * IDX_DIM, Q_TILE), lambda i, j: (i, 0, 0, j)),
                  pl.BlockSpec((None, IDX_HEADS, Q_TILE), lambda i, j: (i, 0, j)),
                  pl.BlockSpec(bias_tiles.shape, lambda i, j: (0, 0, 0, 0))],
        out_specs=pl.BlockSpec((None, w, Q_TILE), lambda i, j: (i, 0, j)),
        out_shape=jax.ShapeDtypeStruct((b, w, t), F32),
        scratch_shapes=[pltpu.VMEM((t, Q_TILE), jnp.int32), pltpu.VMEM((t, Q_TILE), BF16)],
        compiler_params=_cparams("parallel", "arbitrary"),
        name="dsa_prompt",
    )(qp, kb, vtb, kic, qic, wit, bias_tiles)
    return jnp.transpose(ot, (0, 2, 1))


HEADS_PER_GROUP = N_HEADS // SSM_GROUPS
PAIR_W = HEADS_PER_GROUP * HEAD_DIM
assert PAIR_W == LANES and SSM_DSTATE == LANES and SSM_CHUNK == LANES


def _ssd_prompt_kernel(xdt_ref, b_ref, c_ref, dax_ref, dat_ref, h0_ref, y_ref, hout_ref, st_scr):
    ci = pl.program_id(1)

    @pl.when(ci == 0)
    def _():
        st_scr[...] = h0_ref[...]

    q = SSM_CHUNK
    row = lax.broadcasted_iota(jnp.int32, (q, q), 0)
    col = lax.broadcasted_iota(jnp.int32, (q, q), 1)
    lower = row >= col
    hi = lax.Precision.HIGHEST
    cs_x = jnp.dot(jnp.where(lower, 1.0, 0.0), dax_ref[...], preferred_element_type=F32, precision=hi)
    cs_t = jnp.dot(dat_ref[...], jnp.where(row <= col, 1.0, 0.0), preferred_element_type=F32, precision=hi)
    lane_lo = col < HEAD_DIM
    for g in range(SSM_GROUPS):
        sl = slice(g * PAIR_W, (g + 1) * PAIR_W)
        cg = c_ref[:, g * SSM_DSTATE:(g + 1) * SSM_DSTATE]
        bg = b_ref[:, g * SSM_DSTATE:(g + 1) * SSM_DSTATE]
        xdt = xdt_ref[:, sl]
        cs_pair = cs_x[:, sl]
        cs_swap = pltpu.roll(cs_pair, HEAD_DIM, 1)
        cb = lax.dot_general(cg, bg, (((1,), (1,)), ((), ())), preferred_element_type=F32)
        y_pair = []
        for a in range(HEADS_PER_GROUP):
            h = g * HEADS_PER_GROUP + a
            own_half = lane_lo if a == 0 else jnp.logical_not(lane_lo)
            cs_col = jnp.where(own_half, cs_pair, cs_swap)
            seg = jnp.exp(jnp.minimum(cs_col - cs_t[h:h + 1, :], 0.0))
            scores = cb * jnp.where(lower, seg, 0.0)
            y_pair.append(jnp.dot(scores, xdt, preferred_element_type=F32))
        y_diag = jnp.where(lane_lo, y_pair[0], y_pair[1])
        st = st_scr[sl, :]
        y_off = lax.dot_general(cg, st, (((1,), (1,)), ((), ())), preferred_element_type=F32)
        y_ref[:, sl] = y_diag + y_off * jnp.exp(cs_pair)
        cs_last = cs_pair[q - 1:q, :]
        xdtd = xdt * jnp.exp(cs_last - cs_pair)
        upd = jnp.dot(xdtd.T, bg, preferred_element_type=F32)
        last_a = cs_last[:, 0:1]
        last_b = cs_last[:, HEAD_DIM:HEAD_DIM + 1]
        st_scr[sl, :] = st * jnp.exp(jnp.where(row < HEAD_DIM, last_a, last_b)) + upd

    @pl.when(ci == pl.num_programs(1) - 1)
    def _():
        hout_ref[...] = st_scr[...]


def ssd_prompt(xs, dt, a_neg, bm, cm, h0):
    b, t, w = xs.shape
    assert t % SSM_CHUNK == 0
    rep = lambda z: jnp.repeat(z, HEAD_DIM, axis=-1)
    xdt = xs * rep(dt)
    da = dt * a_neg
    dat = jnp.pad(jnp.transpose(da, (0, 2, 1)), ((0, 0), (0, SUBLANES - N_HEADS), (0, 0)))
    hp = N_HEADS * HEAD_DIM
    chunk = lambda width: pl.BlockSpec((None, SSM_CHUNK, width), lambda i, j: (i, j, 0))
    state = pl.BlockSpec((None, hp, SSM_DSTATE), lambda i, j: (i, 0, 0))
    y, h = pl.pallas_call(
        _ssd_prompt_kernel,
        grid=(b, t // SSM_CHUNK),
        in_specs=[chunk(w), chunk(bm.shape[-1]), chunk(cm.shape[-1]), chunk(w),
                  pl.BlockSpec((None, SUBLANES, SSM_CHUNK), lambda i, j: (i, 0, j)), state],
        out_specs=[chunk(w), state],
        out_shape=[jax.ShapeDtypeStruct((b, t, w), F32), jax.ShapeDtypeStruct((b, hp, SSM_DSTATE), F32)],
        scratch_shapes=[pltpu.VMEM((hp, SSM_DSTATE), F32)],
        compiler_params=_cparams("parallel", "arbitrary"),
        name="ssd_prompt",
    )(xdt, bm, cm, rep(da), dat, h0.reshape(b, hp, SSM_DSTATE))
    return y, h.reshape(b, N_HEADS, HEAD_DIM, SSM_DSTATE)


RWKV_CHUNK = 128


def _rwkv_prompt_kernel(r_ref, w_ref, k_ref, kk_ref, kka_ref, vt_ref, s0_ref, ot_ref, sout_ref, s_scr):
    ci = pl.program_id(0)
    n_bh = s_scr.shape[0]

    @pl.when(ci == 0)
    def _():
        s_scr[...] = s0_ref[...]

    lane_t = lax.broadcasted_iota(jnp.int32, (1, RWKV_CHUNK), 1)
    ot_ref[...] = jnp.zeros(ot_ref.shape, F32)

    def step(t, _):
        at_t = lane_t == t
        for bh in range(n_bh):
            s = s_scr[bh]
            row = lambda ref: ref[t, bh:bh + 1, :]
            sa = -jnp.sum(s * row(kk_ref), axis=1, keepdims=True)
            v_col = jnp.sum(jnp.where(at_t, vt_ref[bh], 0.0), axis=1, keepdims=True)
            s = s * row(w_ref) + sa * row(kka_ref) + v_col * row(k_ref)
            s_scr[bh] = s
            o_col = jnp.sum(s * row(r_ref), axis=1, keepdims=True)
            ot_ref[bh] = jnp.where(at_t, o_col, ot_ref[bh])
        return 0

    lax.fori_loop(0, RWKV_CHUNK, step, 0)

    @pl.when(ci == pl.num_programs(0) - 1)
    def _():
        sout_ref[...] = s_scr[...]


def rwkv_prompt(r, w, k, v, kk, kka, s0):
    b, t, hw = r.shape
    assert t % RWKV_CHUNK == 0
    bh = b * N_HEADS
    rows = lambda z: jnp.transpose(z.reshape(b, t, N_HEADS, HEAD_DIM), (1, 0, 2, 3)).reshape(t, bh, HEAD_DIM)
    vt = jnp.transpose(v.reshape(b, t, N_HEADS, HEAD_DIM), (0, 2, 3, 1)).reshape(bh, HEAD_DIM, t)
    row_spec = pl.BlockSpec((RWKV_CHUNK, bh, HEAD_DIM), lambda c: (c, 0, 0))
    col_spec = pl.BlockSpec((bh, HEAD_DIM, RWKV_CHUNK), lambda c: (0, 0, c))
    st_spec = pl.BlockSpec((bh, HEAD_DIM, HEAD_DIM), lambda c: (0, 0, 0))
    ot, s = pl.pallas_call(
        _rwkv_prompt_kernel,
        grid=(t // RWKV_CHUNK,),
        in_specs=[row_spec] * 5 + [col_spec, st_spec],
        out_specs=[col_spec, st_spec],
        out_shape=[jax.ShapeDtypeStruct((bh, HEAD_DIM, t), F32),
                   jax.ShapeDtypeStruct((bh, HEAD_DIM, HEAD_DIM), F32)],
        scratch_shapes=[pltpu.VMEM((bh, HEAD_DIM, HEAD_DIM), F32)],
        compiler_params=_cparams("arbitrary"),
        name="rwkv_prompt",
    )(rows(r), rows(w), rows(k), rows(kk), rows(kka), vt, s0.reshape(bh, HEAD_DIM, HEAD_DIM))
    o = jnp.transpose(ot.reshape(b, N_HEADS, HEAD_DIM, t), (0, 3, 1, 2)).reshape(b, t, hw)
    return o, s.reshape(b, N_HEADS, HEAD_DIM, HEAD_DIM)


STEP_SEQS = 8
NT_DIMS = (((1,), (1,)), ((), ()))


def _row_dot(row, mat):
    r8 = jnp.broadcast_to(row, (SUBLANES, row.shape[1]))
    return lax.dot_general(r8, mat, NT_DIMS, preferred_element_type=F32,
                           precision=lax.Precision.HIGHEST)[0:1, :]


def _ssd_step_kernel(h0_ref, xdtb_ref, dab_ref, b_ref, c_ref, h_ref, y_ref):
    for s in range(h0_ref.shape[0]):
        for h in range(N_HEADS):
            hn = h0_ref[s, h] * jnp.exp(dab_ref[s, h]) + xdtb_ref[s, h] * b_ref[s, h]
            h_ref[s, h] = hn
            y_ref[s, h] = _row_dot(c_ref[s, h], hn)


def ssd_step(xs, dt, a_neg, bm, cm, h0):
    s = xs.shape[0]
    assert s % STEP_SEQS == 0
    xdtb = jnp.broadcast_to((xs.reshape(s, N_HEADS, HEAD_DIM) * dt[:, :, None])[..., None],
                            (s, N_HEADS, HEAD_DIM, SSM_DSTATE))
    dab = jnp.broadcast_to((dt * a_neg)[:, :, None, None], (s, N_HEADS, 1, SSM_DSTATE))
    per_head = lambda z: jnp.repeat(z.reshape(s, SSM_GROUPS, 1, SSM_DSTATE), HEADS_PER_GROUP, axis=1)
    big = pl.BlockSpec((STEP_SEQS, N_HEADS, HEAD_DIM, SSM_DSTATE), lambda i: (i, 0, 0, 0))
    row = pl.BlockSpec((STEP_SEQS, N_HEADS, 1, SSM_DSTATE), lambda i: (i, 0, 0, 0))
    h, y = pl.pallas_call(
        _ssd_step_kernel,
        grid=(s // STEP_SEQS,),
        in_specs=[big, big, row, row, row],
        out_specs=[big, pl.BlockSpec((STEP_SEQS, N_HEADS, 1, HEAD_DIM), lambda i: (i, 0, 0, 0))],
        out_shape=[jax.ShapeDtypeStruct(h0.shape, F32), jax.ShapeDtypeStruct((s, N_HEADS, 1, HEAD_DIM), F32)],
        compiler_params=_cparams("parallel"),
        name="ssd_step",
    )(h0, xdtb, dab, per_head(bm), per_head(cm))
    return y.reshape(s, N_HEADS * HEAD_DIM), h


def _rwkv_step_kernel(s0_ref, vb_ref, w_ref, kk_ref, kka_ref, k_ref, r_ref, s_ref, o_ref):
    for s in range(s0_ref.shape[0]):
        for h in range(N_HEADS):
            st = s0_ref[s, h]
            sa = -jnp.sum(st * kk_ref[s, h], axis=1, keepdims=True)
            st = st * w_ref[s, h] + sa * kka_ref[s, h] + vb_ref[s, h] * k_ref[s, h]
            s_ref[s, h] = st
            o_ref[s, h] = _row_dot(r_ref[s, h], st)


def rwkv_step(r, w, k, v, kk, kka, s0):
    s = r.shape[0]
    assert s % STEP_SEQS == 0
    rows = lambda z: z.reshape(s, N_HEADS, 1, HEAD_DIM)
    vb = jnp.broadcast_to(v.reshape(s, N_HEADS, HEAD_DIM, 1), (s, N_HEADS, HEAD_DIM, HEAD_DIM))
    big = pl.BlockSpec((STEP_SEQS, N_HEADS, HEAD_DIM, HEAD_DIM), lambda i: (i, 0, 0, 0))
    row = pl.BlockSpec((STEP_SEQS, N_HEADS, 1, HEAD_DIM), lambda i: (i, 0, 0, 0))
    st, o = pl.pallas_call(
        _rwkv_step_kernel,
        grid=(s // STEP_SEQS,),
        in_specs=[big, big, row, row, row, row, row],
        out_specs=[big, row],
        out_shape=[jax.ShapeDtypeStruct(s0.shape, F32), jax.ShapeDtypeStruct((s, N_HEADS, 1, HEAD_DIM), F32)],
        compiler_params=_cparams("parallel"),
        name="rwkv_step",
    )(s0, vb, rows(w), rows(kk), rows(kka), rows(k), rows(r))
    return o.reshape(s, N_HEADS * HEAD_DIM), st


def _head_rows_to_row(o):
    r = lax.broadcasted_iota(jnp.int32, o.shape, 0)
    c = lax.broadcasted_iota(jnp.int32, o.shape, 1)
    return jnp.sum(jnp.where(c // HEAD_DIM == r, o, 0.0), axis=0, keepdims=True)


def _sample_softmax_pv(logit_pages, s_new, v_pages, vnew):
    m = s_new
    for s in logit_pages:
        m = jnp.maximum(m, jnp.max(s, axis=1, keepdims=True))
    p_new = jnp.exp(s_new - m)
    l = p_new
    acc = p_new * vnew
    for s, v_ref in zip(logit_pages, v_pages):
        p = jnp.exp(s - m)
        l = l + jnp.sum(p, axis=1, keepdims=True)
        acc = acc + jnp.dot(p.astype(BF16), v_ref[...].astype(BF16), preferred_element_type=F32)
    return _head_rows_to_row(acc / l)


def _moba_sample_kernel(pt_ref, qbd_ref, knew_ref, vnew_ref, bias_ref, biasn_ref, *rest, n_pages, page):
    del pt_ref
    k_pages, v_pages, o_ref = rest[:n_pages], rest[n_pages:2 * n_pages], rest[2 * n_pages]
    qbd = qbd_ref[...]
    qb = qbd.astype(BF16)
    ppb = MOBA_BLOCK // page
    n_blocks = n_pages // ppb
    raw, ksum = [], []
    for p in range(n_pages):
        kp = k_pages[p][...]
        raw.append(lax.dot_general(qb, kp.astype(BF16), NT_DIMS, preferred_element_type=F32))
        ksum.append(jnp.sum(kp, axis=0, keepdims=True))
    kmean = jnp.concatenate([sum(ksum[n * ppb:(n + 1) * ppb]) * (1.0 / MOBA_BLOCK) for n in range(n_blocks)]
                            + [jnp.zeros_like(ksum[0])] * (SUBLANES - n_blocks % SUBLANES if n_blocks % SUBLANES else 0),
                            axis=0)
    score = lax.dot_general(qbd, kmean, NT_DIMS, preferred_element_type=F32,
                            precision=lax.Precision.HIGHEST)
    sel = _top_blocks_mask(score, n_blocks, min(MOBA_TOPK, n_blocks + 1), 1)
    logits = [raw[p] + bias_ref[:, p * page:(p + 1) * page] + sel[:, p // ppb:p // ppb + 1]
              for p in range(n_pages)]
    s_new = jnp.sum(qbd * knew_ref[...], axis=1, keepdims=True) + biasn_ref[:, 0:1]
    o_ref[...] = _sample_softmax_pv(logits, s_new, v_pages, vnew_ref[...])


def _block_diag_q(q):
    s = q.shape[0]
    eye = jnp.eye(N_HEADS, dtype=q.dtype)
    bd = (q.reshape(s, 1, N_HEADS, HEAD_DIM) * eye[None, :, :, None]).reshape(s, N_HEADS, N_HEADS * HEAD_DIM)
    return jnp.pad(bd, ((0, 0), (0, SUBLANES - N_HEADS), (0, 0)))


def _sample_bias(tab, past_len):
    dist = past_len - jnp.arange(past_len, dtype=jnp.int32)
    past = jnp.pad(tab[_rel_bucket(dist)].T, ((0, SUBLANES - N_HEADS), (0, 0)))
    new = jnp.pad(jnp.broadcast_to(tab[_rel_bucket(jnp.zeros((), jnp.int32))][:, None], (N_HEADS, LANES)),
                  ((0, SUBLANES - N_HEADS), (0, 0)))
    return past, new


def _page_specs(layer, n_pages, page, width):
    return [pl.BlockSpec((None, None, page, width), functools.partial(
        lambda s, pt, p: (layer, pt[s, p], 0, 0), p=p)) for p in range(n_pages)]


def moba_sample(q, knew, vnew, cache_k, cache_v, layer, page_table, tab):
    s, w = q.shape
    n_pages, page = page_table.shape[1], cache_k.shape[2]
    past_len = n_pages * page
    assert past_len % MOBA_BLOCK == 0 and MOBA_BLOCK % page == 0
    bias_p, bias_n = _sample_bias(tab, past_len)
    row = pl.BlockSpec((None, 1, w), lambda i, pt: (i, 0, 0))
    const = lambda shape: pl.BlockSpec(shape, lambda i, pt: (0, 0))
    out = pl.pallas_call(
        functools.partial(_moba_sample_kernel, n_pages=n_pages, page=page),
        grid_spec=pltpu.PrefetchScalarGridSpec(
            num_scalar_prefetch=1, grid=(s,),
            in_specs=[pl.BlockSpec((None, SUBLANES, w), lambda i, pt: (i, 0, 0)), row, row,
                      const(bias_p.shape), const(bias_n.shape)]
            + _page_specs(layer, n_pages, page, w) * 2,
            out_specs=row),
        out_shape=jax.ShapeDtypeStruct((s, 1, w), F32),
        compiler_params=_cparams("arbitrary"),
        name="moba_sample",
    )(page_table, _block_diag_q(q * HEAD_DIM ** -0.5), knew[:, None], vnew[:, None], bias_p, bias_n,
      *([cache_k] * n_pages), *([cache_v] * n_pages))
    return out.reshape(s, w)


def _dsa_sample_kernel(pt_ref, qbd_ref, knew_ref, vnew_ref, bias_ref, biasn_ref, qi_ref, wi_ref, kin_ref,
                       *rest, n_pages, page, n_keep):
    del pt_ref
    k_pages, v_pages = rest[:n_pages], rest[n_pages:2 * n_pages]
    ki_pages, o_ref = rest[2 * n_pages:3 * n_pages], rest[3 * n_pages]
    qi = qi_ref[...]
    wi = wi_ref[:, 0:1]
    hi = lax.Precision.HIGHEST
    keys = []
    for p in range(n_pages):
        d = lax.dot_general(qi, ki_pages[p][...], NT_DIMS, preferred_element_type=F32, precision=hi)
        keys.append(_ordered_key(jnp.sum(wi * jnp.maximum(d, 0.0), axis=0, keepdims=True)))
    d_new = jnp.sum(qi * kin_ref[...], axis=1, keepdims=True)
    key_new = _ordered_key(jnp.sum(wi * jnp.maximum(d_new, 0.0), axis=0, keepdims=True))

    def count(pred):
        c = jnp.where(pred(key_new), 1.0, 0.0)
        for kp in keys:
            c = c + jnp.sum(jnp.where(pred(kp), 1.0, 0.0), axis=1, keepdims=True)
        return c

    kcount = jnp.full((1, 1), float(n_keep), F32)
    thr = _kth_largest_key(lambda c: count(lambda t: t >= c), kcount)
    need = kcount - count(lambda t: t > thr)
    tri = jnp.where(lax.broadcasted_iota(jnp.int32, (page, page), 0)
                    <= lax.broadcasted_iota(jnp.int32, (page, page), 1), 1.0, 0.0).astype(BF16)
    seen = jnp.zeros((1, 1), F32)
    masks = []
    for kp in keys:
        eq = jnp.where(kp == thr, 1.0, 0.0)
        eq8 = jnp.broadcast_to(eq, (SUBLANES, page)).astype(BF16)
        rank = jnp.dot(eq8, tri, preferred_element_type=F32)[0:1, :] + seen
        keep = jnp.where(kp > thr, 1.0, jnp.where(rank <= need, eq, 0.0))
        masks.append(jnp.where(keep > 0.5, 0.0, NEG_BIG))
        seen = seen + jnp.sum(eq, axis=1, keepdims=True)
    keep_new = jnp.where(key_new > thr, 1.0, jnp.where(key_new == thr, jnp.where(seen + 1.0 <= need, 1.0, 0.0), 0.0))
    mask_new = jnp.where(keep_new > 0.5, 0.0, NEG_BIG)
    qbd = qbd_ref[...]
    qb = qbd.astype(BF16)
    logits = [lax.dot_general(qb, k_pages[p][...].astype(BF16), NT_DIMS, preferred_element_type=F32)
              + bias_ref[:, p * page:(p + 1) * page] + masks[p] for p in range(n_pages)]
    s_new = jnp.sum(qbd * knew_ref[...], axis=1, keepdims=True) + biasn_ref[:, 0:1] + mask_new
    o_ref[...] = _sample_softmax_pv(logits, s_new, v_pages, vnew_ref[...])


def dsa_sample(q, knew, vnew, qi, wi, kinew, cache_k, cache_v, cache_ki, layer, page_table, tab):
    s, w = q.shape
    n_pages, page = page_table.shape[1], cache_k.shape[2]
    past_len = n_pages * page
    n_keep = min(DSA_TOPK, (past_len + 1) // 4)
    bias_p, bias_n = _sample_bias(tab, past_len)
    row = pl.BlockSpec((None, 1, w), lambda i, pt: (i, 0, 0))
    const = lambda shape: pl.BlockSpec(shape, lambda i, pt: (0, 0))
    qis = (qi * IDX_DIM ** -0.5).reshape(s, IDX_HEADS, IDX_DIM)
    wib = jnp.broadcast_to((wi * IDX_HEADS ** -0.5)[:, :, None], (s, IDX_HEADS, LANES))
    out = pl.pallas_call(
        functools.partial(_dsa_sample_kernel, n_pages=n_pages, page=page, n_keep=n_keep),
        grid_spec=pltpu.PrefetchScalarGridSpec(
            num_scalar_prefetch=1, grid=(s,),
            in_specs=[pl.BlockSpec((None, SUBLANES, w), lambda i, pt: (i, 0, 0)), row, row,
                      const(bias_p.shape), const(bias_n.shape),
                      pl.BlockSpec((None, IDX_HEADS, IDX_DIM), lambda i, pt: (i, 0, 0)),
                      pl.BlockSpec((None, IDX_HEADS, LANES), lambda i, pt: (i, 0, 0)),
                      pl.BlockSpec((None, 1, IDX_DIM), lambda i, pt: (i, 0, 0))]
            + _page_specs(layer, n_pages, page, w) * 2 + _page_specs(layer, n_pages, page, IDX_DIM),
            out_specs=row),
        out_shape=jax.ShapeDtypeStruct((s, 1, w), F32),
        compiler_params=_cparams("arbitrary"),
        name="dsa_sample",
    )(page_table, _block_diag_q(q * HEAD_DIM ** -0.5), knew[:, None], vnew[:, None], bias_p, bias_n,
      qis, wib, kinew[:, None], *([cache_k] * n_pages), *([cache_v] * n_pages), *([cache_ki] * n_pages))
    return out.reshape(s, w)


N_IN = sum(IN_SPLITS)
N_IN_PAD = -(-N_IN // LANES) * LANES


def _rms_norm(x, g):
    return x * lax.rsqrt(jnp.mean(x * x, axis=-1, keepdims=True) + NORM_EPS) * g


def _split_cols(a, sizes):
    return jnp.split(a, [int(i) for i in np.cumsum(sizes)[:-1]], axis=-1)


def _causal_dwconv(x, buf, w, b):
    width = w.shape[0]
    length = x.shape[1]
    xp = jnp.concatenate([buf, x], axis=1)
    y = b + sum(xp[:, j:j + length] * w[j] for j in range(width))
    return y, xp[:, length:]


def _tokens_matmul(x, w):
    b, t, k = x.shape
    return matmul(x.reshape(b * t, k), w).reshape(b, t, w.shape[1])


def _mamba(pz, pxbc, pdt, conv_buf, h0, conv_w, conv_b, dt_bias, a_log, d_skip, norm_g):
    b, t, _ = pz.shape
    xbc, new_buf = _causal_dwconv(pxbc, conv_buf, conv_w, conv_b)
    xbc = jax.nn.silu(xbc)
    xs, bm, cm = jnp.split(xbc, [BRANCH_W, BRANCH_W + SSM_GROUPS * SSM_DSTATE], axis=-1)
    dt = jax.nn.softplus(pdt + dt_bias)
    a_neg = -jnp.exp(a_log)
    if t == 1:
        y, h_new = ssd_step(xs[:, 0], dt[:, 0], a_neg, bm[:, 0], cm[:, 0], h0)
        y = y[:, None]
    else:
        y, h_new = ssd_prompt(xs, dt, a_neg, bm, cm, h0)
    y = y + jnp.repeat(d_skip, HEAD_DIM) * xs
    return _rms_norm(y * jax.nn.silu(pz), norm_g), new_buf, h_new


def _rwkv(p, shift_prev, s0, mu, w0, w2, a0, a2, g2, k_k, k_a, r_k, ln_w, ln_b):
    b, t, _ = p.shape
    p_prev = jnp.concatenate([shift_prev[:, None], p[:, :-1]], axis=1)
    pm = p + (p_prev - p) * mu
    r, k, v, xw, xa, xg = _split_cols(pm, (BRANCH_W, BRANCH_W, BRANCH_W, RWKV_W_LORA, RWKV_A_LORA, RWKV_G_LORA))
    logw = -jax.nn.softplus(-(w0 + _tokens_matmul(jnp.tanh(xw), w2))) - 0.5
    decay = jnp.exp(-jnp.exp(logw))
    a = jax.nn.sigmoid(a0 + _tokens_matmul(xa, a2))
    g = _tokens_matmul(jax.nn.sigmoid(xg), g2)
    heads = lambda z: z.reshape(b, t, N_HEADS, HEAD_DIM)
    kk = heads(k * k_k)
    kk = (kk * lax.rsqrt(jnp.maximum(jnp.sum(kk * kk, axis=-1, keepdims=True), 1e-24))).reshape(b, t, BRANCH_W)
    k = k * (1.0 + (a - 1.0) * k_a)
    if t == 1:
        o, s_new = rwkv_step(r[:, 0], decay[:, 0], k[:, 0], v[:, 0], kk[:, 0], (kk * a)[:, 0], s0)
        o = o[:, None]
    else:
        o, s_new = rwkv_prompt(r, decay, k, v, kk, kk * a, s0)
    o = heads(o)
    mean = jnp.mean(o, axis=-1, keepdims=True)
    var = jnp.mean(jnp.square(o - mean), axis=-1, keepdims=True)
    o = ((o - mean) * lax.rsqrt(var + RWKV_GN_EPS)).reshape(b, t, BRANCH_W) * ln_w + ln_b
    bonus = jnp.sum(heads(r) * heads(k) * r_k, axis=-1, keepdims=True) * heads(v)
    return (o + bonus.reshape(b, t, BRANCH_W)) * g, p[:, -1], s_new


def _layer(x, l, past, prm, attn):
    ssm_h, ssm_buf, rw_s, rw_shift, ffn_buf = past
    b, t, _ = x.shape
    h = _rms_norm(x, prm["norm1_g"][l])
    w_in = jnp.pad(prm["w_in"][l], ((0, 0), (0, N_IN_PAD - N_IN)))
    proj = _tokens_matmul(h, w_in)[..., :N_IN]
    (s_z, s_xbc, s_dt, m_q, m_k, m_v, d_q, d_k, d_v, d_qi, d_ki, d_wi, p_rwkv, p_gate) = _split_cols(proj, IN_SPLITS)
    o_a, ssm_buf_new, ssm_h_new = _mamba(s_z, s_xbc, s_dt, ssm_buf, ssm_h, prm["ssm_conv_w"][l], prm["ssm_conv_b"][l],
                                         prm["ssm_dt_bias"][l], prm["ssm_A_log"][l], prm["ssm_D"][l],
                                         prm["ssm_norm_g"][l])
    o_b = attn("moba", l, m_q, m_k, m_v)
    o_c = attn("dsa", l, d_q, d_k, d_v, d_qi, d_ki, d_wi)
    o_d, rw_shift_new, rw_s_new = _rwkv(p_rwkv, rw_shift, rw_s, prm["rwkv_mu"][l], prm["rwkv_w0"][l], prm["rwkv_w2"][l],
                                        prm["rwkv_a0"][l], prm["rwkv_a2"][l], prm["rwkv_g2"][l], prm["rwkv_k_k"][l],
                                        prm["rwkv_k_a"][l], prm["rwkv_r_k"][l], prm["rwkv_ln_w"][l], prm["rwkv_ln_b"][l])
    gates = jax.nn.sigmoid(p_gate.reshape(b, t, N_BRANCH, D_MODEL))
    mix = sum(gates[:, :, n] * _tokens_matmul(o, prm["w_branch"][l, n]) for n, o in enumerate((o_a, o_b, o_c, o_d)))
    x = x + _tokens_matmul(mix, prm["w_out"][l])
    up = _tokens_matmul(_rms_norm(x, prm["norm2_g"][l]), prm["ffn_w_up"][l])
    a, bv = jnp.split(up, 2, axis=-1)
    a, ffn_buf_new = _causal_dwconv(a, ffn_buf, prm["ffn_conv_w"][l], prm["ffn_conv_b"][l])
    x = x + _tokens_matmul(jax.nn.silu(a) * bv, prm["ffn_w_down"][l])
    heads = lambda z: z.reshape(b, t, N_HEADS, HEAD_DIM)
    new = (heads(m_k), heads(m_v), heads(d_k), heads(d_v), d_ki, ssm_h_new, ssm_buf_new, rw_s_new, rw_shift_new,
           ffn_buf_new)
    return x, new


def kernel(x_prompt, x_sample, cache_moba_k, cache_moba_v, cache_dsa_k, cache_dsa_v, cache_dsa_kidx, page_table,
           state_ssm, state_ssm_conv, state_rwkv, state_rwkv_shift, state_ffn_conv,
           norm1_g, w_in, ssm_conv_w, ssm_conv_b, ssm_dt_bias, ssm_A_log, ssm_D, ssm_norm_g,
           rwkv_mu, rwkv_w0, rwkv_w2, rwkv_a0, rwkv_a2, rwkv_g2, rwkv_k_k, rwkv_k_a, rwkv_r_k, rwkv_ln_w, rwkv_ln_b,
           w_branch, w_out, norm2_g, ffn_w_up, ffn_conv_w, ffn_conv_b, ffn_w_down, rel_bias, final_g):
    prm = dict(norm1_g=norm1_g, w_in=w_in, ssm_conv_w=ssm_conv_w, ssm_conv_b=ssm_conv_b, ssm_dt_bias=ssm_dt_bias,
               ssm_A_log=ssm_A_log, ssm_D=ssm_D, ssm_norm_g=ssm_norm_g, rwkv_mu=rwkv_mu, rwkv_w0=rwkv_w0,
               rwkv_w2=rwkv_w2, rwkv_a0=rwkv_a0, rwkv_a2=rwkv_a2, rwkv_g2=rwkv_g2, rwkv_k_k=rwkv_k_k,
               rwkv_k_a=rwkv_k_a, rwkv_r_k=rwkv_r_k, rwkv_ln_w=rwkv_ln_w, rwkv_ln_b=rwkv_ln_b, w_branch=w_branch,
               w_out=w_out, norm2_g=norm2_g, ffn_w_up=ffn_w_up, ffn_conv_w=ffn_conv_w, ffn_conv_b=ffn_conv_b,
               ffn_w_down=ffn_w_down)
    depth = w_in.shape[0]
    bp = x_prompt.shape[0]
    ns = x_sample.shape[0]
    assert x_sample.shape[1] == 1
    tab_moba, tab_dsa = rel_bias[:, :N_HEADS], rel_bias[:, N_HEADS:]
    tiles_moba, tiles_dsa = _bias_tiles(tab_moba), _bias_tiles(tab_dsa)
    paged = lambda c: c.reshape(c.shape[:3] + (-1,))
    ck_m, cv_m, ck_d, cv_d = paged(cache_moba_k), paged(cache_moba_v), paged(cache_dsa_k), paged(cache_dsa_v)

    def attn_prompt(kind, l, q, k, v, qi=None, ki=None, wi=None):
        if kind == "moba":
            return moba_prompt(q, k, v, tiles_moba)
        return dsa_prompt(q, k, v, qi, ki, wi, tiles_dsa)

    def attn_sample(kind, l, q, k, v, qi=None, ki=None, wi=None):
        if kind == "moba":
            return moba_sample(q[:, 0], k[:, 0], v[:, 0], ck_m, cv_m, l, page_table, tab_moba)[:, None]
        return dsa_sample(q[:, 0], k[:, 0], v[:, 0], qi[:, 0], wi[:, 0], ki[:, 0], ck_d, cv_d, cache_dsa_kidx,
                          l, page_table, tab_dsa)[:, None]

    zp = lambda *shape: jnp.zeros((bp,) + shape, F32)
    prompt_past = (zp(N_HEADS, HEAD_DIM, SSM_DSTATE), zp(SSM_CONV - 1, SSM_CONV_DIM),
                   zp(N_HEADS, HEAD_DIM, HEAD_DIM), zp(RWKV_IN), zp(FFN_CONV - 1, D_FF))
    xp, xs = x_prompt, x_sample
    new_p, new_s = [], []
    for l in range(depth):
        sample_past = (state_ssm[l], state_ssm_conv[l], state_rwkv[l], state_rwkv_shift[l], state_ffn_conv[l])
        xp, np_l = _layer(xp, l, prompt_past, prm, attn_prompt)
        xs, ns_l = _layer(xs, l, sample_past, prm, attn_sample)
        new_p.append(np_l)
        new_s.append(ns_l)
    outs = [_rms_norm(xp, final_g), _rms_norm(xs, final_g)]
    for i in range(10):
        outs.append(jnp.stack([o[i] for o in new_p], axis=0))
        outs.append(jnp.stack([o[i] for o in new_s], axis=0))
    return tuple(outs)
```

```python
import functools
import math

import jax
import jax.numpy as jnp
import numpy as np
from jax import lax
from jax.experimental import pallas as pl
from jax.experimental.pallas import tpu as pltpu

D_MODEL = 1024
N_BRANCH = 4
BRANCH_W = D_MODEL // N_BRANCH
HEAD_DIM = 64
N_HEADS = BRANCH_W // HEAD_DIM
SSM_DSTATE = 128
SSM_GROUPS = 2
SSM_CONV = 4
SSM_CHUNK = 128
SSM_CONV_DIM = BRANCH_W + 2 * SSM_GROUPS * SSM_DSTATE
MOBA_BLOCK = 256
MOBA_TOPK = 3
DSA_TOPK = 256
IDX_HEADS = 8
IDX_DIM = 32
RWKV_W_LORA = 64
RWKV_A_LORA = 64
RWKV_G_LORA = 128
RWKV_IN = 3 * BRANCH_W + RWKV_W_LORA + RWKV_A_LORA + RWKV_G_LORA
RWKV_GN_EPS = 64e-5
D_FF = 11 * D_MODEL // 4
FFN_CONV = 3
NUM_BUCKETS = 32
MAX_DISTANCE = 1024
NORM_EPS = 1e-6
IN_SPLITS = (BRANCH_W, SSM_CONV_DIM, N_HEADS,
             BRANCH_W, BRANCH_W, BRANCH_W,
             BRANCH_W, BRANCH_W, BRANCH_W, IDX_HEADS * IDX_DIM, IDX_DIM, IDX_HEADS,
             RWKV_IN, N_BRANCH * D_MODEL)
F32 = jnp.float32
BF16 = jnp.bfloat16

LANES = 128
SUBLANES = 8
VMEM_LIMIT = 56 * 1024 * 1024
NEG_BIG = -1e30
Q_TILE = 128
K_TILE = MOBA_BLOCK


def _cparams(*sem):
    return pltpu.CompilerParams(dimension_semantics=sem, vmem_limit_bytes=VMEM_LIMIT)


def _mm_kernel(x_ref, w_ref, o_ref):
    o_ref[...] = jnp.dot(x_ref[...], w_ref[...], preferred_element_type=F32)


def _pick_tile(n, cap, quantum):
    if n <= cap:
        return n
    best = None
    for t in range(quantum, cap + 1, quantum):
        if n % t == 0:
            best = t
    assert best is not None, (n, cap, quantum)
    return best


def matmul(x, w, *, tm_cap=512, tn_cap=1024):
    m, k = x.shape
    k2, n = w.shape
    assert k == k2
    x = x.astype(BF16)
    w = w.astype(BF16)
    tm = _pick_tile(m, tm_cap, SUBLANES * 2)
    tn = _pick_tile(n, tn_cap, LANES)
    return pl.pallas_call(
        _mm_kernel,
        grid=(n // tn, m // tm),
        in_specs=[pl.BlockSpec((tm, k), lambda j, i: (i, 0)),
                  pl.BlockSpec((k, tn), lambda j, i: (0, j))],
        out_specs=pl.BlockSpec((tm, tn), lambda j, i: (i, j)),
        out_shape=jax.ShapeDtypeStruct((m, n), F32),
        compiler_params=_cparams("parallel", "parallel"),
        name="matmul",
    )(x, w)


def _rel_bucket(dist):
    n = jnp.maximum(dist, 0)
    max_exact = NUM_BUCKETS // 2
    nf = jnp.maximum(n, 1).astype(F32)
    large = max_exact + (jnp.log(nf / max_exact) / math.log(MAX_DISTANCE / max_exact)
                         * (NUM_BUCKETS - max_exact)).astype(jnp.int32)
    large = jnp.minimum(large, NUM_BUCKETS - 1)
    return jnp.where(n < max_exact, n, large)


N_NEAR = -(-(MAX_DISTANCE + K_TILE - 1) // Q_TILE)


def _bias_tiles(tab):
    o = jnp.arange(N_NEAR + 1, dtype=jnp.int32)[:, None, None] * Q_TILE
    j = jnp.arange(K_TILE, dtype=jnp.int32)[None, :, None]
    i = jnp.arange(Q_TILE, dtype=jnp.int32)[None, None, :]
    dist = jnp.where(o >= N_NEAR * Q_TILE, MAX_DISTANCE, o + i - j)
    return _bucket_lookup(tab, _rel_bucket(dist))


def _bucket_lookup(tab, bucket):
    hit = bucket[..., None] == jnp.arange(NUM_BUCKETS, dtype=jnp.int32)
    cols = tab.T.reshape((tab.shape[1],) + (1,) * bucket.ndim + (NUM_BUCKETS,))
    return jnp.sum(jnp.where(hit, cols, 0.0), axis=-1)


FLASH_GROUP = 2


class _FlashScratch:
    shapes = staticmethod(lambda: [
        pltpu.VMEM((FLASH_GROUP, N_HEADS, K_TILE, Q_TILE), F32),
        pltpu.VMEM((FLASH_GROUP, N_HEADS, K_TILE, Q_TILE), BF16),
        pltpu.VMEM((SUBLANES, Q_TILE), F32),
        pltpu.VMEM((SUBLANES, Q_TILE), F32),
        pltpu.VMEM((N_HEADS, HEAD_DIM, Q_TILE), F32)])

    def __init__(self, s, p, m, l, acc):
        self.s, self.p, self.m, self.l, self.acc = s, p, m, l, acc

    def init(self):
        self.m[...] = jnp.full(self.m.shape, NEG_BIG, F32)
        self.l[...] = jnp.zeros(self.l.shape, F32)
        self.acc[...] = jnp.zeros(self.acc.shape, F32)

    def write(self, o_ref):
        for h in range(N_HEADS):
            o_ref[h * HEAD_DIM:(h + 1) * HEAD_DIM, :] = self.acc[h] / self.l[h:h + 1, :]


REDUCE_CHAINS = 8


def _fold_rows(x, op):
    r, c = x.shape
    x = x.reshape(REDUCE_CHAINS, r // (REDUCE_CHAINS * SUBLANES), SUBLANES, c)
    return op(op(x, axis=1), axis=0)


def _flash_group(k_ref, vt_ref, bias_ref, qpair, tiles, fs):
    for u, (k0, _, _, _) in enumerate(tiles):
        for p in range(N_HEADS // 2):
            kt = k_ref[pl.ds(k0, K_TILE), p * LANES:(p + 1) * LANES]
            s2 = jnp.dot(kt, qpair[p], preferred_element_type=F32)
            fs.s[u, 2 * p] = s2[:, :Q_TILE]
            fs.s[u, 2 * p + 1] = s2[:, Q_TILE:]
    for u, (k0, off_idx, mask_tile, mask_rows) in enumerate(tiles):
        for h in range(N_HEADS):
            s = fs.s[u, h] + bias_ref[h, off_idx]
            if mask_tile is not None:
                s = s + mask_tile
            m_old = fs.m[h:h + 1, :]
            m_tile = jnp.max(_fold_rows(s, jnp.max), axis=0, keepdims=True)
            if mask_rows is not None:
                m_tile = m_tile + mask_rows[h]
            m_new = jnp.maximum(m_old, m_tile)
            shift = m_new if mask_rows is None else m_new - mask_rows[h]
            p = jnp.exp(s - shift)
            alpha = jnp.exp(m_old - m_new)
            fs.l[h:h + 1, :] = alpha * fs.l[h:h + 1, :] + jnp.sum(_fold_rows(p, jnp.sum), axis=0, keepdims=True)
            fs.m[h:h + 1, :] = m_new
            fs.p[u, h] = p.astype(BF16)
            vt = vt_ref[h * HEAD_DIM:(h + 1) * HEAD_DIM, pl.ds(k0, K_TILE)]
            fs.acc[h] = alpha * fs.acc[h] + jnp.dot(vt, fs.p[u, h], preferred_element_type=F32)


def _query_pairs(q_ref):
    return [jnp.concatenate([q_ref[2 * p].astype(BF16), q_ref[2 * p + 1].astype(BF16)], axis=1)
            for p in range(N_HEADS // 2)]


def _causal_mask(q0, k0):
    row = lax.broadcasted_iota(jnp.int32, (K_TILE, Q_TILE), 0)
    lane = lax.broadcasted_iota(jnp.int32, (K_TILE, Q_TILE), 1)
    return jnp.where(k0 + row <= q0 + lane, 0.0, NEG_BIG).astype(F32)


def _top_blocks_mask(score, n_valid, n_sel, axis):
    n = score.shape[axis]
    idx = lax.broadcasted_iota(jnp.int32, score.shape, axis)
    low = jnp.float32(-3.0e38)
    s = jnp.where(idx < n_valid, score, low)
    mask = jnp.full(score.shape, NEG_BIG, F32)
    for _ in range(n_sel):
        mx = jnp.max(s, axis=axis, keepdims=True)
        first = jnp.min(jnp.where(s == mx, idx, n), axis=axis, keepdims=True)
        pick = idx == jnp.where(mx > low, first, -1)
        mask = jnp.where(pick, 0.0, mask)
        s = jnp.where(pick, low, s)
    return mask


def _moba_prompt_kernel(q_ref, k_ref, vt_ref, kmean_ref, bias_ref, o_ref, sel_scr, *flash_scr):
    qi = pl.program_id(1)
    q0 = qi * Q_TILE
    own = q0 // K_TILE
    nb = kmean_ref.shape[1]
    fs = _FlashScratch(*flash_scr)
    fs.init()
    for h in range(N_HEADS):
        score = jnp.dot(kmean_ref[h], q_ref[h], preferred_element_type=F32,
                        precision=lax.Precision.HIGHEST)
        sel_scr[h] = _top_blocks_mask(score, own, min(MOBA_TOPK, nb), 0)
    qpair = _query_pairs(q_ref)
    k_own = pl.multiple_of(own * K_TILE, K_TILE)
    _flash_group(k_ref, vt_ref, bias_ref, qpair,
                 [(k_own, (q0 - k_own) // Q_TILE, _causal_mask(q0, k_own), None)], fs)

    def body(g, _):
        tiles = []
        for u in range(FLASH_GROUP):
            kj = g * FLASH_GROUP + u
            k0 = pl.multiple_of(kj * K_TILE, K_TILE)
            off = jnp.clip((q0 - k0) // Q_TILE, 0, N_NEAR)
            tiles.append((k0, off, None, [sel_scr[h, pl.ds(kj, 1), :] for h in range(N_HEADS)]))
        _flash_group(k_ref, vt_ref, bias_ref, qpair, tiles, fs)
        return 0

    lax.fori_loop(0, (own + FLASH_GROUP - 1) // FLASH_GROUP, body, 0)
    fs.write(o_ref)


def _pad_heads_t(q):
    b, t, _ = q.shape
    qt = jnp.transpose(q.reshape(b, t, N_HEADS, HEAD_DIM), (0, 2, 3, 1))
    z = jnp.zeros_like(qt)
    even = jnp.concatenate([qt, z], axis=2)
    odd = jnp.concatenate([z, qt], axis=2)
    is_even = (jnp.arange(N_HEADS) % 2 == 0)[None, :, None, None]
    return jnp.where(is_even, even, odd)


def _block_mean_kernel(k_ref, o_ref):
    o_ref[...] = jnp.mean(k_ref[...], axis=0, keepdims=True)


def block_mean(k):
    b, t, c = k.shape
    nb = t // MOBA_BLOCK
    out = pl.pallas_call(
        _block_mean_kernel,
        grid=(b, nb),
        in_specs=[pl.BlockSpec((None, MOBA_BLOCK, c), lambda i, j: (i, j, 0))],
        out_specs=pl.BlockSpec((None, None, 1, c), lambda i, j: (i, j, 0, 0)),
        out_shape=jax.ShapeDtypeStruct((b, nb, 1, c), F32),
        compiler_params=_cparams("parallel", "parallel"),
        name="block_mean",
    )(k)
    return out.reshape(b, nb, c)


def moba_prompt(q, k, v, bias_tiles):
    b, t, w = q.shape
    assert t % K_TILE == 0 and K_TILE % Q_TILE == 0
    nb = t // MOBA_BLOCK
    qp = _pad_heads_t(q * HEAD_DIM ** -0.5)
    kmean = block_mean(k)
    kmp = jnp.transpose(_pad_heads_t(kmean), (0, 1, 3, 2))
    kb = k.astype(BF16)
    vtb = jnp.transpose(v, (0, 2, 1)).astype(BF16)
    ot = pl.pallas_call(
        _moba_prompt_kernel,
        grid=(b, t // Q_TILE),
        in_specs=[pl.BlockSpec((None, N_HEADS, LANES, Q_TILE), lambda i, j: (i, 0, 0, j)),
                  pl.BlockSpec((None, t, w), lambda i, j: (i, 0, 0)),
                  pl.BlockSpec((None, w, t), lambda i, j: (i, 0, 0)),
                  pl.BlockSpec((None, N_HEADS, nb, LANES), lambda i, j: (i, 0, 0, 0)),
                  pl.BlockSpec(bias_tiles.shape, lambda i, j: (0, 0, 0, 0))],
        out_specs=pl.BlockSpec((None, w, Q_TILE), lambda i, j: (i, 0, j)),
        out_shape=jax.ShapeDtypeStruct((b, w, t), F32),
        scratch_shapes=[pltpu.VMEM((N_HEADS, nb, Q_TILE), F32)] + _FlashScratch.shapes(),
        compiler_params=_cparams("parallel", "arbitrary"),
        name="moba_prompt",
    )(qp, kb, vtb, kmp, bias_tiles)
    return jnp.transpose(ot, (0, 2, 1))


def _split_bf16(x):
    hi = x.astype(BF16)
    lo = (x - hi.astype(F32)).astype(BF16)
    return hi, lo


def _ordered_key(x):
    x = jnp.where(x == 0.0, 0.0, x)
    bits = lax.bitcast_convert_type(x, jnp.int32)
    return bits ^ ((bits >> 31) & jnp.int32(0x7FFFFFFF))


INT_MIN = -2 ** 31


def _kth_largest_key(count_ge, kcount):
    int_min = jnp.int32(INT_MIN)
    thr = jnp.where(count_ge(jnp.zeros_like(kcount, jnp.int32)) >= kcount, jnp.int32(0), int_min)

    def bit_body(i, thr):
        cand = thr | lax.shift_left(jnp.int32(1), 30 - i)
        return jnp.where(count_ge(cand) >= kcount, cand, thr)

    return lax.fori_loop(0, 31, bit_body, thr)


def _dsa_prompt_kernel(q_ref, k_ref, vt_ref, kic_ref, qic_ref, wi_ref, bias_ref, o_ref,
                       key_scr, mask_scr, *flash_scr, n_keep):
    qi = pl.program_id(1)
    q0 = qi * Q_TILE
    own = q0 // K_TILE
    n_tiles = own + 1
    qpos = q0 + lax.broadcasted_iota(jnp.int32, (1, Q_TILE), 1)
    row = lax.broadcasted_iota(jnp.int32, (K_TILE, Q_TILE), 0)
    fs = _FlashScratch(*flash_scr)
    fs.init()

    def tile_start(kj):
        return pl.multiple_of(kj * K_TILE, K_TILE)

    def score_body(kj, _):
        k0 = tile_start(kj)
        kic = kic_ref[pl.ds(k0, K_TILE), :]
        sc = jnp.zeros((K_TILE, Q_TILE), F32)
        for j in range(IDX_HEADS // 2):
            q2 = jnp.concatenate([qic_ref[2 * j], qic_ref[2 * j + 1]], axis=1)
            d = jnp.maximum(jnp.dot(kic, q2, preferred_element_type=F32), 0.0)
            sc = sc + wi_ref[2 * j:2 * j + 1, :] * d[:, :Q_TILE]
            sc = sc + wi_ref[2 * j + 1:2 * j + 2, :] * d[:, Q_TILE:]
        key = jnp.where(k0 + row <= qpos, _ordered_key(sc), jnp.int32(INT_MIN))
        key_scr[pl.ds(k0, K_TILE), :] = key
        return 0

    lax.fori_loop(0, n_tiles, score_body, 0)

    @pl.when(n_tiles % 2 == 1)
    def _():
        key_scr[pl.ds(tile_start(n_tiles), K_TILE), :] = jnp.full((K_TILE, Q_TILE), INT_MIN, jnp.int32)

    n_pairs = (n_tiles + 1) // 2
    groups = 2 * K_TILE // SUBLANES

    def count(pred):
        def body(kp, c):
            t = key_scr[pl.ds(pl.multiple_of(kp * 2 * K_TILE, 2 * K_TILE), 2 * K_TILE), :]
            return c + _fold_rows(jnp.where(pred(t), 1.0, 0.0), jnp.sum)
        c = lax.fori_loop(0, n_pairs, body, jnp.zeros((SUBLANES, Q_TILE), F32))
        return jnp.sum(c, axis=0, keepdims=True)

    kcount = jnp.minimum(n_keep, qpos + 1).astype(F32)
    thr = _kth_largest_key(lambda c: count(lambda t: t >= c), kcount)
    need = kcount - count(lambda t: t > thr)
    tri = jnp.where(lax.broadcasted_iota(jnp.int32, (K_TILE, K_TILE), 0)
                    >= lax.broadcasted_iota(jnp.int32, (K_TILE, K_TILE), 1), 1.0, 0.0).astype(BF16)

    def mask_body(kj, seen):
        k0 = tile_start(kj)
        t = key_scr[pl.ds(k0, K_TILE), :]
        eq = jnp.where(t == thr, 1.0, 0.0)
        rank = jnp.dot(tri, eq.astype(BF16), preferred_element_type=F32) + seen
        keep = jnp.where(t > thr, 1.0, jnp.where(rank <= need, eq, 0.0))
        mask_scr[pl.ds(k0, K_TILE), :] = jnp.where(keep > 0.5, 0.0, NEG_BIG).astype(BF16)
        return seen + jnp.sum(eq, axis=0, keepdims=True)

    lax.fori_loop(0, n_tiles, mask_body, jnp.zeros((1, Q_TILE), F32))
    qpair = _query_pairs(q_ref)

    @pl.when(n_tiles % 2 == 1)
    def _():
        mask_scr[pl.ds(tile_start(n_tiles), K_TILE), :] = jnp.full((K_TILE, Q_TILE), NEG_BIG, BF16)

    def body(g, _):
        tiles = []
        for u in range(FLASH_GROUP):
            k0 = tile_start(g * FLASH_GROUP + u)
            off = jnp.clip((q0 - k0) // Q_TILE, 0, N_NEAR)
            tiles.append((k0, off, mask_scr[pl.ds(k0, K_TILE), :].astype(F32), None))
        _flash_group(k_ref, vt_ref, bias_ref, qpair, tiles, fs)
        return 0

    lax.fori_loop(0, n_pairs, body, 0)
    fs.write(o_ref)


def dsa_prompt(q, k, v, qi, ki, wi, bias_tiles):
    b, t, w = q.shape
    assert t % (2 * K_TILE) == 0 and K_TILE % Q_TILE == 0 and FLASH_GROUP == 2
    n_keep = min(DSA_TOPK, t // 4)
    qp = _pad_heads_t(q * HEAD_DIM ** -0.5)
    kb = k.astype(BF16)
    vtb = jnp.transpose(v, (0, 2, 1)).astype(BF16)
    ki_hi, ki_lo = _split_bf16(ki)
    kic = jnp.concatenate([ki_hi, ki_hi, ki_lo], axis=-1)
    qis = jnp.transpose((qi * IDX_DIM ** -0.5).reshape(b, t, IDX_HEADS, IDX_DIM), (0, 2, 3, 1))
    qi_hi, qi_lo = _split_bf16(qis)
    qic = jnp.concatenate([qi_hi, qi_lo, qi_hi], axis=2)
    wit = jnp.transpose(wi * IDX_HEADS ** -0.5, (0, 2, 1))
    ot = pl.pallas_call(
        functools.partial(_dsa_prompt_kernel, n_keep=n_keep),
        grid=(b, t // Q_TILE),
        in_specs=[pl.BlockSpec((None, N_HEADS, LANES, Q_TILE), lambda i, j: (i, 0, 0, j)),
                  pl.BlockSpec((None, t, w), lambda i, j: (i, 0, 0)),
                  pl.BlockSpec((None, w, t), lambda i, j: (i, 0, 0)),
                  pl.BlockSpec((None, t, 3 * IDX_DIM), lambda i, j: (i, 0, 0)),
                  pl.BlockSpec((None, IDX_HEADS, 3 * IDX_DIM, Q_TILE), lambda i, j: (i, 0, 0, j)),
                  pl.BlockSpec((None, IDX_HEADS, Q_TILE), lambda i, j: (i, 0, j)),
                  pl.BlockSpec(bias_tiles.shape, lambda i, j: (0, 0, 0, 0))],
        out_specs=pl.BlockSpec((None, w, Q_TILE), lambda i, j: (i, 0, j)),
        out_shape=jax.ShapeDtypeStruct((b, w, t), F32),
        scratch_shapes=[pltpu.VMEM((t, Q_TILE), jnp.int32), pltpu.VMEM((t, Q_TILE), BF16)]
        + _FlashScratch.shapes(),
        compiler_params=_cparams("parallel", "arbitrary"),
        name="dsa_prompt",
    )(qp, kb, vtb, kic, qic, wit, bias_tiles)
    return jnp.transpose(ot, (0, 2, 1))


HEADS_PER_GROUP = N_HEADS // SSM_GROUPS
PAIR_W = HEADS_PER_GROUP * HEAD_DIM
assert PAIR_W == LANES and SSM_DSTATE == LANES and SSM_CHUNK == LANES


def _ssd_prompt_kernel(xdt_ref, b_ref, c_ref, dax_ref, dat_ref, h0_ref, y_ref, hout_ref, st_scr):
    ci = pl.program_id(1)

    @pl.when(ci == 0)
    def _():
        st_scr[...] = h0_ref[...]

    q = SSM_CHUNK
    row = lax.broadcasted_iota(jnp.int32, (q, q), 0)
    col = lax.broadcasted_iota(jnp.int32, (q, q), 1)
    lower = row >= col
    hi = lax.Precision.HIGHEST
    cs_x = jnp.dot(jnp.where(lower, 1.0, 0.0), dax_ref[...], preferred_element_type=F32, precision=hi)
    cs_t = jnp.dot(dat_ref[...], jnp.where(row <= col, 1.0, 0.0), preferred_element_type=F32, precision=hi)
    lane_lo = col < HEAD_DIM
    for g in range(SSM_GROUPS):
        sl = slice(g * PAIR_W, (g + 1) * PAIR_W)
        cg = c_ref[:, g * SSM_DSTATE:(g + 1) * SSM_DSTATE]
        bg = b_ref[:, g * SSM_DSTATE:(g + 1) * SSM_DSTATE]
        xdt = xdt_ref[:, sl]
        cs_pair = cs_x[:, sl]
        cs_swap = pltpu.roll(cs_pair, HEAD_DIM, 1)
        cb = lax.dot_general(cg, bg, (((1,), (1,)), ((), ())), preferred_element_type=F32)
        y_pair = []
        for a in range(HEADS_PER_GROUP):
            h = g * HEADS_PER_GROUP + a
            own_half = lane_lo if a == 0 else jnp.logical_not(lane_lo)
            cs_col = jnp.where(own_half, cs_pair, cs_swap)
            seg = jnp.exp(jnp.minimum(cs_col - cs_t[h:h + 1, :], 0.0))
            scores = cb * jnp.where(lower, seg, 0.0)
            y_pair.append(jnp.dot(scores, xdt, preferred_element_type=F32))
        y_diag = jnp.where(lane_lo, y_pair[0], y_pair[1])
        st = st_scr[sl, :]
        y_off = lax.dot_general(cg, st, (((1,), (1,)), ((), ())), preferred_element_type=F32)
        y_ref[:, sl] = y_diag + y_off * jnp.exp(cs_pair)
        cs_last = cs_pair[q - 1:q, :]
        xdtd = xdt * jnp.exp(cs_last - cs_pair)
        upd = jnp.dot(xdtd.T, bg, preferred_element_type=F32)
        last_a = cs_last[:, 0:1]
        last_b = cs_last[:, HEAD_DIM:HEAD_DIM + 1]
        st_scr[sl, :] = st * jnp.exp(jnp.where(row < HEAD_DIM, last_a, last_b)) + upd

    @pl.when(ci == pl.num_programs(1) - 1)
    def _():
        hout_ref[...] = st_scr[...]


def ssd_prompt(xs, dt, a_neg, bm, cm, h0):
    b, t, w = xs.shape
    assert t % SSM_CHUNK == 0
    rep = lambda z: jnp.repeat(z, HEAD_DIM, axis=-1)
    xdt = xs * rep(dt)
    da = dt * a_neg
    dat = jnp.pad(jnp.transpose(da, (0, 2, 1)), ((0, 0), (0, SUBLANES - N_HEADS), (0, 0)))
    hp = N_HEADS * HEAD_DIM
    chunk = lambda width: pl.BlockSpec((None, SSM_CHUNK, width), lambda i, j: (i, j, 0))
    state = pl.BlockSpec((None, hp, SSM_DSTATE), lambda i, j: (i, 0, 0))
    y, h = pl.pallas_call(
        _ssd_prompt_kernel,
        grid=(b, t // SSM_CHUNK),
        in_specs=[chunk(w), chunk(bm.shape[-1]), chunk(cm.shape[-1]), chunk(w),
                  pl.BlockSpec((None, SUBLANES, SSM_CHUNK), lambda i, j: (i, 0, j)), state],
        out_specs=[chunk(w), state],
        out_shape=[jax.ShapeDtypeStruct((b, t, w), F32), jax.ShapeDtypeStruct((b, hp, SSM_DSTATE), F32)],
        scratch_shapes=[pltpu.VMEM((hp, SSM_DSTATE), F32)],
        compiler_params=_cparams("parallel", "arbitrary"),
        name="ssd_prompt",
    )(xdt, bm, cm, rep(da), dat, h0.reshape(b, hp, SSM_DSTATE))
    return y, h.reshape(b, N_HEADS, HEAD_DIM, SSM_DSTATE)


RWKV_CHUNK = 128
HEAD_PAIRS = N_HEADS // 2


def _split_cat(x):
    hi, lo = _split_bf16(x)
    return jnp.concatenate([hi, lo], axis=1)


def _rwkv_prompt_kernel(r_ref, w_ref, k_ref, kk_ref, kka_ref, v_ref, s0_ref, o_ref, sout_ref,
                        s_scr, vt_scr, ot_scr):
    ci = pl.program_id(0)
    n_b = r_ref.shape[0]
    tiles = [(b, p) for b in range(n_b) for p in range(HEAD_PAIRS)]
    lanes = lambda p: slice(p * LANES, (p + 1) * LANES)

    @pl.when(ci == 0)
    def _():
        s_scr[...] = s0_ref[...]

    for i, (b, p) in enumerate(tiles):
        vt_scr[i] = v_ref[b, :, lanes(p)].T
    ot_scr[...] = jnp.zeros(ot_scr.shape, F32)
    r2 = lax.broadcasted_iota(jnp.int32, (2 * LANES, LANES), 0)
    c2 = lax.broadcasted_iota(jnp.int32, (2 * LANES, LANES), 1)
    ones_same = jnp.where((r2 % LANES) // HEAD_DIM == c2 // HEAD_DIM, 1.0, 0.0).astype(BF16)
    r3 = lax.broadcasted_iota(jnp.int32, (2 * LANES, 2 * LANES), 0)
    c3 = lax.broadcasted_iota(jnp.int32, (2 * LANES, 2 * LANES), 1)
    ones_out = jnp.where((r3 % LANES) // HEAD_DIM == c3 // LANES, 1.0, 0.0).astype(BF16)
    lane_t = lax.broadcasted_iota(jnp.int32, (1, RWKV_CHUNK), 1)
    head_a = lax.broadcasted_iota(jnp.int32, (HEAD_DIM, LANES), 1) < HEAD_DIM

    def step(t, rows):
        at_t = lane_t == t
        row = lambda ref, b, p: rows(ref, b, p)
        states = [s_scr[i] for i in range(len(tiles))]
        x = jnp.concatenate([_split_cat(s * row(kk_ref, b, p)) for s, (b, p) in zip(states, tiles)], axis=0)
        sa_all = -jnp.dot(x, ones_same, preferred_element_type=F32)
        new_states = []
        for i, (b, p) in enumerate(tiles):
            sa = sa_all[i * HEAD_DIM:(i + 1) * HEAD_DIM]
            v_a = jnp.sum(jnp.where(at_t, vt_scr[i, :HEAD_DIM, :], 0.0), axis=1, keepdims=True)
            v_b = jnp.sum(jnp.where(at_t, vt_scr[i, HEAD_DIM:, :], 0.0), axis=1, keepdims=True)
            s = (states[i] * row(w_ref, b, p) + sa * row(kka_ref, b, p)
                 + jnp.where(head_a, v_a, v_b) * row(k_ref, b, p))
            s_scr[i] = s
            new_states.append(s)
        y = jnp.concatenate([_split_cat(s * row(r_ref, b, p)) for s, (b, p) in zip(new_states, tiles)], axis=0)
        o_all = jnp.dot(y, ones_out, preferred_element_type=F32)
        for i in range(len(tiles)):
            o = o_all[i * HEAD_DIM:(i + 1) * HEAD_DIM]
            ot_scr[i, :HEAD_DIM, :] = jnp.where(at_t, o[:, :LANES], ot_scr[i, :HEAD_DIM, :])
            ot_scr[i, HEAD_DIM:, :] = jnp.where(at_t, o[:, LANES:], ot_scr[i, HEAD_DIM:, :])

    def step_group(g, _):
        base = pl.multiple_of(g * SUBLANES, SUBLANES)
        refs = (r_ref, w_ref, k_ref, kk_ref, kka_ref)
        blocks = {(id(ref), b, p): ref[b, pl.ds(base, SUBLANES), lanes(p)] for ref in refs for b, p in tiles}
        for j in range(SUBLANES):
            step(base + j, lambda ref, b, p, j=j: blocks[(id(ref), b, p)][j:j + 1, :])
        return 0

    lax.fori_loop(0, RWKV_CHUNK // SUBLANES, step_group, 0)
    for i, (b, p) in enumerate(tiles):
        o_ref[b, :, lanes(p)] = ot_scr[i].T

    @pl.when(ci == pl.num_programs(0) - 1)
    def _():
        sout_ref[...] = s_scr[...]


def rwkv_prompt(r, w, k, v, kk, kka, s0):
    b, t, hw = r.shape
    assert t % RWKV_CHUNK == 0 and RWKV_CHUNK == LANES and 2 * HEAD_DIM == LANES
    n_tiles = b * HEAD_PAIRS
    pack = lambda s: jnp.transpose(s.reshape(b, HEAD_PAIRS, 2, HEAD_DIM, HEAD_DIM), (0, 1, 3, 2, 4)).reshape(
        n_tiles, HEAD_DIM, LANES)
    unpack = lambda s: jnp.transpose(s.reshape(b, HEAD_PAIRS, HEAD_DIM, 2, HEAD_DIM), (0, 1, 3, 2, 4)).reshape(
        b, N_HEADS, HEAD_DIM, HEAD_DIM)
    seq_spec = pl.BlockSpec((b, RWKV_CHUNK, hw), lambda c: (0, c, 0))
    st_spec = pl.BlockSpec((n_tiles, HEAD_DIM, LANES), lambda c: (0, 0, 0))
    o, s = pl.pallas_call(
        _rwkv_prompt_kernel,
        grid=(t // RWKV_CHUNK,),
        in_specs=[seq_spec] * 6 + [st_spec],
        out_specs=[seq_spec, st_spec],
        out_shape=[jax.ShapeDtypeStruct((b, t, hw), F32), jax.ShapeDtypeStruct((n_tiles, HEAD_DIM, LANES), F32)],
        scratch_shapes=[pltpu.VMEM((n_tiles, HEAD_DIM, LANES), F32),
                        pltpu.VMEM((n_tiles, LANES, RWKV_CHUNK), F32),
                        pltpu.VMEM((n_tiles, LANES, RWKV_CHUNK), F32)],
        compiler_params=_cparams("arbitrary"),
        name="rwkv_prompt",
    )(r, w, k, kk, kka, v, pack(s0))
    return o, unpack(s)


STEP_SEQS = 8
NT_DIMS = (((1,), (1,)), ((), ()))


def _row_dot(row, mat):
    r8 = jnp.broadcast_to(row, (SUBLANES, row.shape[1]))
    return lax.dot_general(r8, mat, NT_DIMS, preferred_element_type=F32,
                           precision=lax.Precision.HIGHEST)[0:1, :]


def _ssd_step_kernel(h0_ref, xdtb_ref, dab_ref, b_ref, c_ref, h_ref, y_ref):
    for s in range(h0_ref.shape[0]):
        for h in range(N_HEADS):
            hn = h0_ref[s, h] * jnp.exp(dab_ref[s, h]) + xdtb_ref[s, h] * b_ref[s, h]
            h_ref[s, h] = hn
            y_ref[s, h] = _row_dot(c_ref[s, h], hn)


def ssd_step(xs, dt, a_neg, bm, cm, h0):
    s = xs.shape[0]
    assert s % STEP_SEQS == 0
    xdtb = jnp.broadcast_to((xs.reshape(s, N_HEADS, HEAD_DIM) * dt[:, :, None])[..., None],
                            (s, N_HEADS, HEAD_DIM, SSM_DSTATE))
    dab = jnp.broadcast_to((dt * a_neg)[:, :, None, None], (s, N_HEADS, 1, SSM_DSTATE))
    per_head = lambda z: jnp.repeat(z.reshape(s, SSM_GROUPS, 1, SSM_DSTATE), HEADS_PER_GROUP, axis=1)
    big = pl.BlockSpec((STEP_SEQS, N_HEADS, HEAD_DIM, SSM_DSTATE), lambda i: (i, 0, 0, 0))
    row = pl.BlockSpec((STEP_SEQS, N_HEADS, 1, SSM_DSTATE), lambda i: (i, 0, 0, 0))
    h, y = pl.pallas_call(
        _ssd_step_kernel,
        grid=(s // STEP_SEQS,),
        in_specs=[big, big, row, row, row],
        out_specs=[big, pl.BlockSpec((STEP_SEQS, N_HEADS, 1, HEAD_DIM), lambda i: (i, 0, 0, 0))],
        out_shape=[jax.ShapeDtypeStruct(h0.shape, F32), jax.ShapeDtypeStruct((s, N_HEADS, 1, HEAD_DIM), F32)],
        compiler_params=_cparams("parallel"),
        name="ssd_step",
    )(h0, xdtb, dab, per_head(bm), per_head(cm))
    return y.reshape(s, N_HEADS * HEAD_DIM), h


def _rwkv_step_kernel(s0_ref, vb_ref, w_ref, kk_ref, kka_ref, k_ref, r_ref, s_ref, o_ref):
    for s in range(s0_ref.shape[0]):
        for h in range(N_HEADS):
            st = s0_ref[s, h]
            sa = -jnp.sum(st * kk_ref[s, h], axis=1, keepdims=True)
            st = st * w_ref[s, h] + sa * kka_ref[s, h] + vb_ref[s, h] * k_ref[s, h]
            s_ref[s, h] = st
            o_ref[s, h] = _row_dot(r_ref[s, h], st)


def rwkv_step(r, w, k, v, kk, kka, s0):
    s = r.shape[0]
    assert s % STEP_SEQS == 0
    rows = lambda z: z.reshape(s, N_HEADS, 1, HEAD_DIM)
    vb = jnp.broadcast_to(v.reshape(s, N_HEADS, HEAD_DIM, 1), (s, N_HEADS, HEAD_DIM, HEAD_DIM))
    big = pl.BlockSpec((STEP_SEQS, N_HEADS, HEAD_DIM, HEAD_DIM), lambda i: (i, 0, 0, 0))
    row = pl.BlockSpec((STEP_SEQS, N_HEADS, 1, HEAD_DIM), lambda i: (i, 0, 0, 0))
    st, o = pl.pallas_call(
        _rwkv_step_kernel,
        grid=(s // STEP_SEQS,),
        in_specs=[big, big, row, row, row, row, row],
        out_specs=[big, row],
        out_shape=[jax.ShapeDtypeStruct(s0.shape, F32), jax.ShapeDtypeStruct((s, N_HEADS, 1, HEAD_DIM), F32)],
        compiler_params=_cparams("parallel"),
        name="rwkv_step",
    )(s0, vb, rows(w), rows(kk), rows(kka), rows(k), rows(r))
    return o.reshape(s, N_HEADS * HEAD_DIM), st


def _head_rows_to_row(o):
    r = lax.broadcasted_iota(jnp.int32, o.shape, 0)
    c = lax.broadcasted_iota(jnp.int32, o.shape, 1)
    return jnp.sum(jnp.where(c // HEAD_DIM == r, o, 0.0), axis=0, keepdims=True)


def _sample_softmax_pv(logit_pages, s_new, vt_pages, vnew):
    m = s_new
    for s in logit_pages:
        m = jnp.maximum(m, jnp.max(s, axis=1, keepdims=True))
    p_new = jnp.exp(s_new - m)
    l = p_new
    acc = p_new * vnew
    for s, vt_ref in zip(logit_pages, vt_pages):
        p = jnp.exp(s - m)
        l = l + jnp.sum(p, axis=1, keepdims=True)
        acc = acc + lax.dot_general(p.astype(BF16), vt_ref[...].astype(BF16), NT_DIMS, preferred_element_type=F32)
    return _head_rows_to_row(acc / l)


def _moba_sample_kernel(pt_ref, qbd_ref, knew_ref, vnew_ref, bias_ref, biasn_ref, *rest, n_pages, page):
    del pt_ref
    kt_pages, vt_pages, o_ref = rest[:n_pages], rest[n_pages:2 * n_pages], rest[2 * n_pages]
    qbd = qbd_ref[...]
    q_hi, q_lo = _split_bf16(qbd)
    q_both = jnp.concatenate([q_hi, q_lo], axis=0)
    ppb = MOBA_BLOCK // page
    n_blocks = n_pages // ppb
    lane = lax.broadcasted_iota(jnp.int32, (SUBLANES, LANES), 1)
    raw = []
    score = jnp.zeros((SUBLANES, LANES), F32)
    for p in range(n_pages):
        k_hi, k_lo = _split_bf16(kt_pages[p][...])
        r = jnp.dot(q_both, k_hi, preferred_element_type=F32)
        r = r[:SUBLANES] + r[SUBLANES:] + jnp.dot(q_hi, k_lo, preferred_element_type=F32)
        raw.append(r)
        score = score + jnp.where(lane == p // ppb, jnp.sum(r, axis=1, keepdims=True) * (1.0 / MOBA_BLOCK), 0.0)
    sel = _top_blocks_mask(score, n_blocks, min(MOBA_TOPK, n_blocks + 1), 1)
    logits = [raw[p] + bias_ref[:, p * page:(p + 1) * page] + sel[:, p // ppb:p // ppb + 1]
              for p in range(n_pages)]
    s_new = jnp.sum(qbd * knew_ref[...], axis=1, keepdims=True) + biasn_ref[:, 0:1]
    o_ref[...] = _sample_softmax_pv(logits, s_new, vt_pages, vnew_ref[...])


def _block_diag_q(q):
    s = q.shape[0]
    eye = jnp.eye(N_HEADS, dtype=q.dtype)
    bd = (q.reshape(s, 1, N_HEADS, HEAD_DIM) * eye[None, :, :, None]).reshape(s, N_HEADS, N_HEADS * HEAD_DIM)
    return jnp.pad(bd, ((0, 0), (0, SUBLANES - N_HEADS), (0, 0)))


def _sample_bias(tab, past_len):
    dist = past_len - jnp.arange(past_len, dtype=jnp.int32)
    pad = ((0, SUBLANES - N_HEADS), (0, 0))
    past = jnp.pad(_bucket_lookup(tab, _rel_bucket(dist)), pad)
    new = jnp.pad(_bucket_lookup(tab, _rel_bucket(jnp.zeros((LANES,), jnp.int32))), pad)
    return past, new


def _page_specs(layer, n_pages, rows, page):
    return [pl.BlockSpec((None, None, rows, page), functools.partial(
        lambda s, pt, p: (layer, pt[s, p], 0, 0), p=p)) for p in range(n_pages)]


def _transposed_pages(cache):
    c = cache.reshape(cache.shape[:3] + (-1,))
    return jnp.transpose(c, (0, 1, 3, 2))


def moba_sample(q, knew, vnew, cache_kt, cache_vt, layer, page_table, tab):
    s, w = q.shape
    n_pages, page = page_table.shape[1], cache_kt.shape[3]
    past_len = n_pages * page
    assert past_len % MOBA_BLOCK == 0 and MOBA_BLOCK % page == 0
    bias_p, bias_n = _sample_bias(tab, past_len)
    row = pl.BlockSpec((None, 1, w), lambda i, pt: (i, 0, 0))
    const = lambda shape: pl.BlockSpec(shape, lambda i, pt: (0, 0))
    out = pl.pallas_call(
        functools.partial(_moba_sample_kernel, n_pages=n_pages, page=page),
        grid_spec=pltpu.PrefetchScalarGridSpec(
            num_scalar_prefetch=1, grid=(s,),
            in_specs=[pl.BlockSpec((None, SUBLANES, w), lambda i, pt: (i, 0, 0)), row, row,
                      const(bias_p.shape), const(bias_n.shape)]
            + _page_specs(layer, n_pages, w, page) * 2,
            out_specs=row),
        out_shape=jax.ShapeDtypeStruct((s, 1, w), F32),
        compiler_params=_cparams("arbitrary"),
        name="moba_sample",
    )(page_table, _block_diag_q(q * HEAD_DIM ** -0.5), knew[:, None], vnew[:, None], bias_p, bias_n,
      *([cache_kt] * n_pages), *([cache_vt] * n_pages))
    return out.reshape(s, w)


def _dsa_sample_kernel(pt_ref, qbd_ref, knew_ref, vnew_ref, bias_ref, biasn_ref, qi_ref, wi_ref, kin_ref,
                       *rest, n_pages, page, n_keep):
    del pt_ref
    kt_pages, vt_pages = rest[:n_pages], rest[n_pages:2 * n_pages]
    kit_pages, o_ref = rest[2 * n_pages:3 * n_pages], rest[3 * n_pages]
    qi = qi_ref[...]
    wi = wi_ref[:, 0:1]
    hi = lax.Precision.HIGHEST
    keys = []
    for p in range(n_pages):
        d = jnp.dot(qi, kit_pages[p][...], preferred_element_type=F32, precision=hi)
        keys.append(_ordered_key(jnp.sum(wi * jnp.maximum(d, 0.0), axis=0, keepdims=True)))
    d_new = jnp.sum(qi * kin_ref[...], axis=1, keepdims=True)
    key_new = _ordered_key(jnp.sum(wi * jnp.maximum(d_new, 0.0), axis=0, keepdims=True))

    def count(pred):
        c = jnp.where(pred(key_new), 1.0, 0.0)
        for kp in keys:
            c = c + jnp.sum(jnp.where(pred(kp), 1.0, 0.0), axis=1, keepdims=True)
        return c

    kcount = jnp.full((1, 1), float(n_keep), F32)
    thr = _kth_largest_key(lambda c: count(lambda t: t >= c), kcount)
    need = kcount - count(lambda t: t > thr)
    tri = jnp.where(lax.broadcasted_iota(jnp.int32, (page, page), 0)
                    <= lax.broadcasted_iota(jnp.int32, (page, page), 1), 1.0, 0.0).astype(BF16)
    seen = jnp.zeros((1, 1), F32)
    masks = []
    for kp in keys:
        eq = jnp.where(kp == thr, 1.0, 0.0)
        eq8 = jnp.broadcast_to(eq, (SUBLANES, page)).astype(BF16)
        rank = jnp.dot(eq8, tri, preferred_element_type=F32)[0:1, :] + seen
        keep = jnp.where(kp > thr, 1.0, jnp.where(rank <= need, eq, 0.0))
        masks.append(jnp.where(keep > 0.5, 0.0, NEG_BIG))
        seen = seen + jnp.sum(eq, axis=1, keepdims=True)
    keep_new = jnp.where(key_new > thr, 1.0, jnp.where(key_new == thr, jnp.where(seen + 1.0 <= need, 1.0, 0.0), 0.0))
    mask_new = jnp.where(keep_new > 0.5, 0.0, NEG_BIG)
    qbd = qbd_ref[...]
    qb = qbd.astype(BF16)
    logits = [jnp.dot(qb, kt_pages[p][...].astype(BF16), preferred_element_type=F32)
              + bias_ref[:, p * page:(p + 1) * page] + masks[p] for p in range(n_pages)]
    s_new = jnp.sum(qbd * knew_ref[...], axis=1, keepdims=True) + biasn_ref[:, 0:1] + mask_new
    o_ref[...] = _sample_softmax_pv(logits, s_new, vt_pages, vnew_ref[...])


def dsa_sample(q, knew, vnew, qi, wi, kinew, cache_kt, cache_vt, cache_kit, layer, page_table, tab):
    s, w = q.shape
    n_pages, page = page_table.shape[1], cache_kt.shape[3]
    past_len = n_pages * page
    n_keep = min(DSA_TOPK, (past_len + 1) // 4)
    bias_p, bias_n = _sample_bias(tab, past_len)
    row = pl.BlockSpec((None, 1, w), lambda i, pt: (i, 0, 0))
    const = lambda shape: pl.BlockSpec(shape, lambda i, pt: (0, 0))
    qis = (qi * IDX_DIM ** -0.5).reshape(s, IDX_HEADS, IDX_DIM)
    wib = jnp.broadcast_to((wi * IDX_HEADS ** -0.5)[:, :, None], (s, IDX_HEADS, LANES))
    out = pl.pallas_call(
        functools.partial(_dsa_sample_kernel, n_pages=n_pages, page=page, n_keep=n_keep),
        grid_spec=pltpu.PrefetchScalarGridSpec(
            num_scalar_prefetch=1, grid=(s,),
            in_specs=[pl.BlockSpec((None, SUBLANES, w), lambda i, pt: (i, 0, 0)), row, row,
                      const(bias_p.shape), const(bias_n.shape),
                      pl.BlockSpec((None, IDX_HEADS, IDX_DIM), lambda i, pt: (i, 0, 0)),
                      pl.BlockSpec((None, IDX_HEADS, LANES), lambda i, pt: (i, 0, 0)),
                      pl.BlockSpec((None, 1, IDX_DIM), lambda i, pt: (i, 0, 0))]
            + _page_specs(layer, n_pages, w, page) * 2 + _page_specs(layer, n_pages, IDX_DIM, page),
            out_specs=row),
        out_shape=jax.ShapeDtypeStruct((s, 1, w), F32),
        compiler_params=_cparams("arbitrary"),
        name="dsa_sample",
    )(page_table, _block_diag_q(q * HEAD_DIM ** -0.5), knew[:, None], vnew[:, None], bias_p, bias_n,
      qis, wib, kinew[:, None], *([cache_kt] * n_pages), *([cache_vt] * n_pages), *([cache_kit] * n_pages))
    return out.reshape(s, w)


N_IN = sum(IN_SPLITS)
N_IN_PAD = -(-N_IN // LANES) * LANES


def _rms_norm(x, g):
    return x * lax.rsqrt(jnp.mean(x * x, axis=-1, keepdims=True) + NORM_EPS) * g


def _split_cols(a, sizes):
    return jnp.split(a, [int(i) for i in np.cumsum(sizes)[:-1]], axis=-1)


def _causal_dwconv(x, buf, w, b):
    width = w.shape[0]
    length = x.shape[1]
    xp = jnp.concatenate([buf, x], axis=1)
    y = b + sum(xp[:, j:j + length] * w[j] for j in range(width))
    return y, xp[:, length:]


def _tokens_matmul(x, w):
    b, t, k = x.shape
    return matmul(x.reshape(b * t, k), w).reshape(b, t, w.shape[1])


def _mamba(pz, pxbc, pdt, conv_buf, h0, conv_w, conv_b, dt_bias, a_log, d_skip, norm_g):
    b, t, _ = pz.shape
    xbc, new_buf = _causal_dwconv(pxbc, conv_buf, conv_w, conv_b)
    xbc = jax.nn.silu(xbc)
    xs, bm, cm = jnp.split(xbc, [BRANCH_W, BRANCH_W + SSM_GROUPS * SSM_DSTATE], axis=-1)
    dt = jax.nn.softplus(pdt + dt_bias)
    a_neg = -jnp.exp(a_log)
    if t == 1:
        y, h_new = ssd_step(xs[:, 0], dt[:, 0], a_neg, bm[:, 0], cm[:, 0], h0)
        y = y[:, None]
    else:
        y, h_new = ssd_prompt(xs, dt, a_neg, bm, cm, h0)
    y = y + jnp.repeat(d_skip, HEAD_DIM) * xs
    return _rms_norm(y * jax.nn.silu(pz), norm_g), new_buf, h_new


def _rwkv(p, shift_prev, s0, mu, w0, w2, a0, a2, g2, k_k, k_a, r_k, ln_w, ln_b):
    b, t, _ = p.shape
    p_prev = jnp.concatenate([shift_prev[:, None], p[:, :-1]], axis=1)
    pm = p + (p_prev - p) * mu
    r, k, v, xw, xa, xg = _split_cols(pm, (BRANCH_W, BRANCH_W, BRANCH_W, RWKV_W_LORA, RWKV_A_LORA, RWKV_G_LORA))
    logw = -jax.nn.softplus(-(w0 + _tokens_matmul(jnp.tanh(xw), w2))) - 0.5
    decay = jnp.exp(-jnp.exp(logw))
    a = jax.nn.sigmoid(a0 + _tokens_matmul(xa, a2))
    g = _tokens_matmul(jax.nn.sigmoid(xg), g2)
    heads = lambda z: z.reshape(b, t, N_HEADS, HEAD_DIM)
    kk = heads(k * k_k)
    kk = (kk * lax.rsqrt(jnp.maximum(jnp.sum(kk * kk, axis=-1, keepdims=True), 1e-24))).reshape(b, t, BRANCH_W)
    k = k * (1.0 + (a - 1.0) * k_a)
    if t == 1:
        o, s_new = rwkv_step(r[:, 0], decay[:, 0], k[:, 0], v[:, 0], kk[:, 0], (kk * a)[:, 0], s0)
        o = o[:, None]
    else:
        o, s_new = rwkv_prompt(r, decay, k, v, kk, kk * a, s0)
    o = heads(o)
    mean = jnp.mean(o, axis=-1, keepdims=True)
    var = jnp.mean(jnp.square(o - mean), axis=-1, keepdims=True)
    o = ((o - mean) * lax.rsqrt(var + RWKV_GN_EPS)).reshape(b, t, BRANCH_W) * ln_w + ln_b
    bonus = jnp.sum(heads(r) * heads(k) * r_k, axis=-1, keepdims=True) * heads(v)
    return (o + bonus.reshape(b, t, BRANCH_W)) * g, p[:, -1], s_new


PROJ_ORDER = (13, 12, 0, 1, 3, 4, 5, 6, 7, 8, 9, 2, 10, 11)
MERGE_ROWS = 256


def _merge_kernel(oa_ref, ob_ref, oc_ref, od_ref, g_ref, wb_ref, wo_ref, x_ref, out_ref):
    mix = None
    for n, o_ref in enumerate((oa_ref, ob_ref, oc_ref, od_ref)):
        pr = jnp.dot(o_ref[...].astype(BF16), wb_ref[n], preferred_element_type=F32)
        term = jax.nn.sigmoid(g_ref[:, n * D_MODEL:(n + 1) * D_MODEL]) * pr
        mix = term if mix is None else mix + term
    out_ref[...] = x_ref[...] + jnp.dot(mix.astype(BF16), wo_ref[...], preferred_element_type=F32)


def merge_branches(x, branches, proj, w_branch, w_out):
    m = x.shape[0]
    tm = _pick_tile(m, MERGE_ROWS, SUBLANES)
    tok = lambda width: pl.BlockSpec((tm, width), lambda i: (i, 0))
    return pl.pallas_call(
        _merge_kernel,
        grid=(m // tm,),
        in_specs=[tok(BRANCH_W)] * N_BRANCH + [tok(N_BRANCH * D_MODEL),
                  pl.BlockSpec((N_BRANCH, BRANCH_W, D_MODEL), lambda i: (0, 0, 0)),
                  pl.BlockSpec((D_MODEL, D_MODEL), lambda i: (0, 0)), tok(D_MODEL)],
        out_specs=tok(D_MODEL),
        out_shape=jax.ShapeDtypeStruct((m, D_MODEL), F32),
        compiler_params=_cparams("parallel"),
        name="merge_branches",
    )(*branches, proj, w_branch.astype(BF16), w_out.astype(BF16), x)


def _layer(x, l, past, prm, attn):
    ssm_h, ssm_buf, rw_s, rw_shift, ffn_buf = past
    b, t, _ = x.shape
    h = _rms_norm(x, prm["norm1_g"][l])
    w_cols = _split_cols(prm["w_in"][l], IN_SPLITS)
    w_in = jnp.concatenate([w_cols[i] for i in PROJ_ORDER] + [jnp.zeros((D_MODEL, N_IN_PAD - N_IN), F32)], axis=1)
    proj = _tokens_matmul(h, w_in)
    parts = dict(zip(PROJ_ORDER, _split_cols(proj[..., :N_IN], [IN_SPLITS[i] for i in PROJ_ORDER])))
    (s_z, s_xbc, s_dt, m_q, m_k, m_v, d_q, d_k, d_v, d_qi, d_ki, d_wi, p_rwkv, _) = (parts[i] for i in range(14))
    o_a, ssm_buf_new, ssm_h_new = _mamba(s_z, s_xbc, s_dt, ssm_buf, ssm_h, prm["ssm_conv_w"][l], prm["ssm_conv_b"][l],
                                         prm["ssm_dt_bias"][l], prm["ssm_A_log"][l], prm["ssm_D"][l],
                                         prm["ssm_norm_g"][l])
    o_b = attn("moba", l, m_q, m_k, m_v)
    o_c = attn("dsa", l, d_q, d_k, d_v, d_qi, d_ki, d_wi)
    o_d, rw_shift_new, rw_s_new = _rwkv(p_rwkv, rw_shift, rw_s, prm["rwkv_mu"][l], prm["rwkv_w0"][l], prm["rwkv_w2"][l],
                                        prm["rwkv_a0"][l], prm["rwkv_a2"][l], prm["rwkv_g2"][l], prm["rwkv_k_k"][l],
                                        prm["rwkv_k_a"][l], prm["rwkv_r_k"][l], prm["rwkv_ln_w"][l], prm["rwkv_ln_b"][l])
    flat = lambda z: z.reshape(b * t, z.shape[-1])
    x = merge_branches(flat(x), [flat(o) for o in (o_a, o_b, o_c, o_d)], flat(proj), prm["w_branch"][l],
                       prm["w_out"][l]).reshape(b, t, D_MODEL)
    up = _tokens_matmul(_rms_norm(x, prm["norm2_g"][l]), prm["ffn_w_up"][l])
    a, bv = jnp.split(up, 2, axis=-1)
    a, ffn_buf_new = _causal_dwconv(a, ffn_buf, prm["ffn_conv_w"][l], prm["ffn_conv_b"][l])
    x = x + _tokens_matmul(jax.nn.silu(a) * bv, prm["ffn_w_down"][l])
    heads = lambda z: z.reshape(b, t, N_HEADS, HEAD_DIM)
    new = (heads(m_k), heads(m_v), heads(d_k), heads(d_v), d_ki, ssm_h_new, ssm_buf_new, rw_s_new, rw_shift_new,
           ffn_buf_new)
    return x, new


def kernel(x_prompt, x_sample, cache_moba_k, cache_moba_v, cache_dsa_k, cache_dsa_v, cache_dsa_kidx, page_table,
           state_ssm, state_ssm_conv, state_rwkv, state_rwkv_shift, state_ffn_conv,
           norm1_g, w_in, ssm_conv_w, ssm_conv_b, ssm_dt_bias, ssm_A_log, ssm_D, ssm_norm_g,
           rwkv_mu, rwkv_w0, rwkv_w2, rwkv_a0, rwkv_a2, rwkv_g2, rwkv_k_k, rwkv_k_a, rwkv_r_k, rwkv_ln_w, rwkv_ln_b,
           w_branch, w_out, norm2_g, ffn_w_up, ffn_conv_w, ffn_conv_b, ffn_w_down, rel_bias, final_g):
    prm = dict(norm1_g=norm1_g, w_in=w_in, ssm_conv_w=ssm_conv_w, ssm_conv_b=ssm_conv_b, ssm_dt_bias=ssm_dt_bias,
               ssm_A_log=ssm_A_log, ssm_D=ssm_D, ssm_norm_g=ssm_norm_g, rwkv_mu=rwkv_mu, rwkv_w0=rwkv_w0,
               rwkv_w2=rwkv_w2, rwkv_a0=rwkv_a0, rwkv_a2=rwkv_a2, rwkv_g2=rwkv_g2, rwkv_k_k=rwkv_k_k,
               rwkv_k_a=rwkv_k_a, rwkv_r_k=rwkv_r_k, rwkv_ln_w=rwkv_ln_w, rwkv_ln_b=rwkv_ln_b, w_branch=w_branch,
               w_out=w_out, norm2_g=norm2_g, ffn_w_up=ffn_w_up, ffn_conv_w=ffn_conv_w, ffn_conv_b=ffn_conv_b,
               ffn_w_down=ffn_w_down)
    depth = w_in.shape[0]
    bp = x_prompt.shape[0]
    ns = x_sample.shape[0]
    assert x_sample.shape[1] == 1
    tab_moba, tab_dsa = rel_bias[:, :N_HEADS], rel_bias[:, N_HEADS:]
    tiles_moba, tiles_dsa = _bias_tiles(tab_moba), _bias_tiles(tab_dsa)
    ck_m, cv_m, ck_d, cv_d, cki_d = (_transposed_pages(c) for c in (cache_moba_k, cache_moba_v, cache_dsa_k,
                                                                    cache_dsa_v, cache_dsa_kidx))

    def attn_prompt(kind, l, q, k, v, qi=None, ki=None, wi=None):
        if kind == "moba":
            return moba_prompt(q, k, v, tiles_moba)
        return dsa_prompt(q, k, v, qi, ki, wi, tiles_dsa)

    def attn_sample(kind, l, q, k, v, qi=None, ki=None, wi=None):
        if kind == "moba":
            return moba_sample(q[:, 0], k[:, 0], v[:, 0], ck_m, cv_m, l, page_table, tab_moba)[:, None]
        return dsa_sample(q[:, 0], k[:, 0], v[:, 0], qi[:, 0], wi[:, 0], ki[:, 0], ck_d, cv_d, cki_d,
                          l, page_table, tab_dsa)[:, None]

    zp = lambda *shape: jnp.zeros((bp,) + shape, F32)
    prompt_past = (zp(N_HEADS, HEAD_DIM, SSM_DSTATE), zp(SSM_CONV - 1, SSM_CONV_DIM),
                   zp(N_HEADS, HEAD_DIM, HEAD_DIM), zp(RWKV_IN), zp(FFN_CONV - 1, D_FF))
    xp, xs = x_prompt, x_sample
    new_p, new_s = [], []
    for l in range(depth):
        sample_past = (state_ssm[l], state_ssm_conv[l], state_rwkv[l], state_rwkv_shift[l], state_ffn_conv[l])
        xp, np_l = _layer(xp, l, prompt_past, prm, attn_prompt)
        xs, ns_l = _layer(xs, l, sample_past, prm, attn_sample)
        new_p.append(np_l)
        new_s.append(ns_l)
    outs = [_rms_norm(xp, final_g), _rms_norm(xs, final_g)]
    for i in range(10):
        outs.append(jnp.stack([o[i] for o in new_p], axis=0))
        outs.append(jnp.stack([o[i] for o in new_s], axis=0))
    return tuple(outs)
```

```python
import functools
import math

import jax
import jax.numpy as jnp
import numpy as np
from jax import lax
from jax.experimental import pallas as pl
from jax.experimental.pallas import tpu as pltpu

D_MODEL = 1024
N_BRANCH = 4
BRANCH_W = D_MODEL // N_BRANCH
HEAD_DIM = 64
N_HEADS = BRANCH_W // HEAD_DIM
SSM_DSTATE = 128
SSM_GROUPS = 2
SSM_CONV = 4
SSM_CHUNK = 128
SSM_CONV_DIM = BRANCH_W + 2 * SSM_GROUPS * SSM_DSTATE
MOBA_BLOCK = 256
MOBA_TOPK = 3
DSA_TOPK = 256
IDX_HEADS = 8
IDX_DIM = 32
RWKV_W_LORA = 64
RWKV_A_LORA = 64
RWKV_G_LORA = 128
RWKV_IN = 3 * BRANCH_W + RWKV_W_LORA + RWKV_A_LORA + RWKV_G_LORA
RWKV_GN_EPS = 64e-5
D_FF = 11 * D_MODEL // 4
FFN_CONV = 3
NUM_BUCKETS = 32
MAX_DISTANCE = 1024
NORM_EPS = 1e-6
IN_SPLITS = (BRANCH_W, SSM_CONV_DIM, N_HEADS,
             BRANCH_W, BRANCH_W, BRANCH_W,
             BRANCH_W, BRANCH_W, BRANCH_W, IDX_HEADS * IDX_DIM, IDX_DIM, IDX_HEADS,
             RWKV_IN, N_BRANCH * D_MODEL)
F32 = jnp.float32
BF16 = jnp.bfloat16

LANES = 128
SUBLANES = 8
VMEM_LIMIT = 56 * 1024 * 1024
NEG_BIG = -1e30
LOG2E = math.log2(math.e)
PROMPT_Q_SCALE = HEAD_DIM ** -0.5 * LOG2E
Q_TILE = 128
K_TILE = MOBA_BLOCK


def _cparams(*sem):
    return pltpu.CompilerParams(dimension_semantics=sem, vmem_limit_bytes=VMEM_LIMIT)


def _mm_kernel(x_ref, w_ref, o_ref):
    o_ref[...] = jnp.dot(x_ref[...], w_ref[...], preferred_element_type=F32)


def _pick_tile(n, cap, quantum):
    if n <= cap:
        return n
    best = None
    for t in range(quantum, cap + 1, quantum):
        if n % t == 0:
            best = t
    assert best is not None, (n, cap, quantum)
    return best


def matmul(x, w, *, tm_cap=1024, tn_cap=1408):
    m, k = x.shape
    k2, n = w.shape
    assert k == k2
    x = x.astype(BF16)
    w = w.astype(BF16)
    tm = _pick_tile(m, tm_cap, SUBLANES * 2)
    tn = _pick_tile(n, tn_cap, LANES)
    return pl.pallas_call(
        _mm_kernel,
        grid=(n // tn, m // tm),
        in_specs=[pl.BlockSpec((tm, k), lambda j, i: (i, 0)),
                  pl.BlockSpec((k, tn), lambda j, i: (0, j))],
        out_specs=pl.BlockSpec((tm, tn), lambda j, i: (i, j)),
        out_shape=jax.ShapeDtypeStruct((m, n), F32),
        compiler_params=_cparams("parallel", "parallel"),
        name="matmul",
    )(x, w)


def _rel_bucket(dist):
    n = jnp.maximum(dist, 0)
    max_exact = NUM_BUCKETS // 2
    nf = jnp.maximum(n, 1).astype(F32)
    large = max_exact + (jnp.log(nf / max_exact) / math.log(MAX_DISTANCE / max_exact)
                         * (NUM_BUCKETS - max_exact)).astype(jnp.int32)
    large = jnp.minimum(large, NUM_BUCKETS - 1)
    return jnp.where(n < max_exact, n, large)


N_NEAR = -(-(MAX_DISTANCE + K_TILE - 1) // Q_TILE)


def _bias_tiles(tab):
    o = jnp.arange(N_NEAR + 1, dtype=jnp.int32)[:, None, None] * Q_TILE
    j = jnp.arange(K_TILE, dtype=jnp.int32)[None, :, None]
    i = jnp.arange(Q_TILE, dtype=jnp.int32)[None, None, :]
    dist = jnp.where(o >= N_NEAR * Q_TILE, MAX_DISTANCE, o + i - j)
    return _bucket_lookup(tab, _rel_bucket(dist))


def _bucket_lookup(tab, bucket):
    hit = bucket[..., None] == jnp.arange(NUM_BUCKETS, dtype=jnp.int32)
    cols = tab.T.reshape((tab.shape[1],) + (1,) * bucket.ndim + (NUM_BUCKETS,))
    return jnp.sum(jnp.where(hit, cols, 0.0), axis=-1)


FLASH_GROUP = 2


class _FlashScratch:
    shapes = staticmethod(lambda: [
        pltpu.VMEM((FLASH_GROUP, N_HEADS, K_TILE, Q_TILE), F32),
        pltpu.VMEM((FLASH_GROUP, N_HEADS, K_TILE, Q_TILE), BF16),
        pltpu.VMEM((SUBLANES, Q_TILE), F32),
        pltpu.VMEM((SUBLANES, Q_TILE), F32),
        pltpu.VMEM((N_HEADS, HEAD_DIM, Q_TILE), F32)])

    def __init__(self, s, p, m, l, acc):
        self.s, self.p, self.m, self.l, self.acc = s, p, m, l, acc

    def init(self):
        self.m[...] = jnp.full(self.m.shape, NEG_BIG, F32)
        self.l[...] = jnp.zeros(self.l.shape, F32)
        self.acc[...] = jnp.zeros(self.acc.shape, F32)

    def write(self, o_ref):
        for h in range(N_HEADS):
            o_ref[h * HEAD_DIM:(h + 1) * HEAD_DIM, :] = self.acc[h] / self.l[h:h + 1, :]


REDUCE_CHAINS = 8


def _fold_rows(x, op):
    r, c = x.shape
    x = x.reshape(REDUCE_CHAINS, r // (REDUCE_CHAINS * SUBLANES), SUBLANES, c)
    return op(op(x, axis=1), axis=0)


def _flash_group(k_ref, vt_ref, bias_ref, qpair, tiles, fs):
    for u, (k0, _, _, _) in enumerate(tiles):
        for p in range(N_HEADS // 2):
            kt = k_ref[pl.ds(k0, K_TILE), p * LANES:(p + 1) * LANES]
            s2 = jnp.dot(kt, qpair[p], preferred_element_type=F32)
            fs.s[u, 2 * p] = s2[:, :Q_TILE]
            fs.s[u, 2 * p + 1] = s2[:, Q_TILE:]
    for u, (k0, off_idx, mask_tile, mask_rows) in enumerate(tiles):
        for h in range(N_HEADS):
            s = fs.s[u, h] + bias_ref[h, off_idx]
            if mask_tile is not None:
                s = s + mask_tile
            m_old = fs.m[h:h + 1, :]
            m_tile = jnp.max(_fold_rows(s, jnp.max), axis=0, keepdims=True)
            if mask_rows is not None:
                m_tile = m_tile + mask_rows[h]
            m_new = jnp.maximum(m_old, m_tile)
            shift = m_new if mask_rows is None else m_new - mask_rows[h]
            p = jnp.exp2(s - shift)
            alpha = jnp.exp2(m_old - m_new)
            fs.l[h:h + 1, :] = alpha * fs.l[h:h + 1, :] + jnp.sum(_fold_rows(p, jnp.sum), axis=0, keepdims=True)
            fs.m[h:h + 1, :] = m_new
            fs.p[u, h] = p.astype(BF16)
            vt = vt_ref[h * HEAD_DIM:(h + 1) * HEAD_DIM, pl.ds(k0, K_TILE)]
            fs.acc[h] = alpha * fs.acc[h] + jnp.dot(vt, fs.p[u, h], preferred_element_type=F32)


def _query_pairs(q_ref):
    return [jnp.concatenate([q_ref[2 * p].astype(BF16), q_ref[2 * p + 1].astype(BF16)], axis=1)
            for p in range(N_HEADS // 2)]


def _causal_mask(q0, k0):
    row = lax.broadcasted_iota(jnp.int32, (K_TILE, Q_TILE), 0)
    lane = lax.broadcasted_iota(jnp.int32, (K_TILE, Q_TILE), 1)
    return jnp.where(k0 + row <= q0 + lane, 0.0, NEG_BIG).astype(F32)


def _top_blocks_mask(score, n_valid, n_sel, axis):
    n = score.shape[axis]
    idx = lax.broadcasted_iota(jnp.int32, score.shape, axis)
    low = jnp.float32(-3.0e38)
    s = jnp.where(idx < n_valid, score, low)
    mask = jnp.full(score.shape, NEG_BIG, F32)
    for _ in range(n_sel):
        mx = jnp.max(s, axis=axis, keepdims=True)
        first = jnp.min(jnp.where(s == mx, idx, n), axis=axis, keepdims=True)
        pick = idx == jnp.where(mx > low, first, -1)
        mask = jnp.where(pick, 0.0, mask)
        s = jnp.where(pick, low, s)
    return mask


def _moba_prompt_kernel(q_ref, k_ref, vt_ref, kmean_ref, bias_ref, o_ref, sel_scr, *flash_scr):
    qi = pl.program_id(1)
    q0 = qi * Q_TILE
    own = q0 // K_TILE
    nb = kmean_ref.shape[1]
    fs = _FlashScratch(*flash_scr)
    fs.init()
    for h in range(N_HEADS):
        score = jnp.dot(kmean_ref[h], q_ref[h], preferred_element_type=F32,
                        precision=lax.Precision.HIGHEST)
        sel_scr[h] = _top_blocks_mask(score, own, min(MOBA_TOPK, nb), 0)
    qpair = _query_pairs(q_ref)
    k_own = pl.multiple_of(own * K_TILE, K_TILE)
    _flash_group(k_ref, vt_ref, bias_ref, qpair,
                 [(k_own, (q0 - k_own) // Q_TILE, _causal_mask(q0, k_own), None)], fs)

    def body(g, _):
        tiles = []
        for u in range(FLASH_GROUP):
            kj = g * FLASH_GROUP + u
            k0 = pl.multiple_of(kj * K_TILE, K_TILE)
            off = jnp.clip((q0 - k0) // Q_TILE, 0, N_NEAR)
            tiles.append((k0, off, None, [sel_scr[h, pl.ds(kj, 1), :] for h in range(N_HEADS)]))
        _flash_group(k_ref, vt_ref, bias_ref, qpair, tiles, fs)
        return 0

    lax.fori_loop(0, (own + FLASH_GROUP - 1) // FLASH_GROUP, body, 0)
    fs.write(o_ref)


def _pad_heads_t(q):
    b, t, _ = q.shape
    qt = jnp.transpose(q.reshape(b, t, N_HEADS, HEAD_DIM), (0, 2, 3, 1))
    z = jnp.zeros_like(qt)
    even = jnp.concatenate([qt, z], axis=2)
    odd = jnp.concatenate([z, qt], axis=2)
    is_even = (jnp.arange(N_HEADS) % 2 == 0)[None, :, None, None]
    return jnp.where(is_even, even, odd)


def _block_mean_kernel(k_ref, o_ref):
    o_ref[...] = jnp.mean(k_ref[...], axis=0, keepdims=True)


def block_mean(k):
    b, t, c = k.shape
    nb = t // MOBA_BLOCK
    out = pl.pallas_call(
        _block_mean_kernel,
        grid=(b, nb),
        in_specs=[pl.BlockSpec((None, MOBA_BLOCK, c), lambda i, j: (i, j, 0))],
        out_specs=pl.BlockSpec((None, None, 1, c), lambda i, j: (i, j, 0, 0)),
        out_shape=jax.ShapeDtypeStruct((b, nb, 1, c), F32),
        compiler_params=_cparams("parallel", "parallel"),
        name="block_mean",
    )(k)
    return out.reshape(b, nb, c)


def moba_prompt(q, k, v, bias_tiles):
    b, t, w = q.shape
    assert t % K_TILE == 0 and K_TILE % Q_TILE == 0
    nb = t // MOBA_BLOCK
    qp = _pad_heads_t(q * PROMPT_Q_SCALE)
    kmean = block_mean(k)
    kmp = jnp.transpose(_pad_heads_t(kmean), (0, 1, 3, 2))
    kb = k.astype(BF16)
    vtb = jnp.transpose(v, (0, 2, 1)).astype(BF16)
    ot = pl.pallas_call(
        _moba_prompt_kernel,
        grid=(b, t // Q_TILE),
        in_specs=[pl.BlockSpec((None, N_HEADS, LANES, Q_TILE), lambda i, j: (i, 0, 0, j)),
                  pl.BlockSpec((None, t, w), lambda i, j: (i, 0, 0)),
                  pl.BlockSpec((None, w, t), lambda i, j: (i, 0, 0)),
                  pl.BlockSpec((None, N_HEADS, nb, LANES), lambda i, j: (i, 0, 0, 0)),
                  pl.BlockSpec(bias_tiles.shape, lambda i, j: (0, 0, 0, 0))],
        out_specs=pl.BlockSpec((None, w, Q_TILE), lambda i, j: (i, 0, j)),
        out_shape=jax.ShapeDtypeStruct((b, w, t), F32),
        scratch_shapes=[pltpu.VMEM((N_HEADS, nb, Q_TILE), F32)] + _FlashScratch.shapes(),
        compiler_params=_cparams("parallel", "arbitrary"),
        name="moba_prompt",
    )(qp, kb, vtb, kmp, bias_tiles)
    return jnp.transpose(ot, (0, 2, 1))


def _split_bf16(x):
    hi = x.astype(BF16)
    lo = (x - hi.astype(F32)).astype(BF16)
    return hi, lo


def _ordered_key(x):
    x = jnp.where(x == 0.0, 0.0, x)
    bits = lax.bitcast_convert_type(x, jnp.int32)
    return bits ^ ((bits >> 31) & jnp.int32(0x7FFFFFFF))


INT_MIN = -2 ** 31


def _kth_largest_key(count_ge, kcount):
    int_min = jnp.int32(INT_MIN)
    thr = jnp.where(count_ge(jnp.zeros_like(kcount, jnp.int32)) >= kcount, jnp.int32(0), int_min)

    def bit_body(i, thr):
        cand = thr | lax.shift_left(jnp.int32(1), 30 - i)
        return jnp.where(count_ge(cand) >= kcount, cand, thr)

    return lax.fori_loop(0, 31, bit_body, thr)


RADIX_BITS = 4


def _kth_largest_key_radix(count_ge, kcount):
    thr = jnp.full(kcount.shape, INT_MIN, jnp.int32)
    for shift in range(32 - RADIX_BITS, -1, -RADIX_BITS):
        digit = jnp.zeros(kcount.shape, jnp.int32)
        for j in range(1, 2 ** RADIX_BITS):
            cand = thr + jnp.int32(np.array(j << shift, np.uint32).astype(np.int32))
            digit = digit + jnp.where(count_ge(cand) >= kcount, 1, 0)
        thr = thr + lax.shift_left(digit, shift)
    return thr


def _dsa_prompt_kernel(q_ref, k_ref, vt_ref, kic_ref, qic_ref, wi_ref, bias_ref, o_ref,
                       key_scr, mask_scr, *flash_scr, n_keep):
    qi = pl.program_id(1)
    q0 = qi * Q_TILE
    own = q0 // K_TILE
    n_tiles = own + 1
    qpos = q0 + lax.broadcasted_iota(jnp.int32, (1, Q_TILE), 1)
    row = lax.broadcasted_iota(jnp.int32, (K_TILE, Q_TILE), 0)
    fs = _FlashScratch(*flash_scr)
    fs.init()

    def tile_start(kj):
        return pl.multiple_of(kj * K_TILE, K_TILE)

    def score_body(kj, _):
        k0 = tile_start(kj)
        kic = kic_ref[pl.ds(k0, K_TILE), :]
        sc = jnp.zeros((K_TILE, Q_TILE), F32)
        for j in range(IDX_HEADS // 2):
            q2 = jnp.concatenate([qic_ref[2 * j], qic_ref[2 * j + 1]], axis=1)
            d = jnp.maximum(jnp.dot(kic, q2, preferred_element_type=F32), 0.0)
            sc = sc + wi_ref[2 * j:2 * j + 1, :] * d[:, :Q_TILE]
            sc = sc + wi_ref[2 * j + 1:2 * j + 2, :] * d[:, Q_TILE:]
        key = jnp.where(k0 + row <= qpos, _ordered_key(sc), jnp.int32(INT_MIN))
        key_scr[pl.ds(k0, K_TILE), :] = key
        return 0

    lax.fori_loop(0, n_tiles, score_body, 0)

    @pl.when(n_tiles % 2 == 1)
    def _():
        key_scr[pl.ds(tile_start(n_tiles), K_TILE), :] = jnp.full((K_TILE, Q_TILE), INT_MIN, jnp.int32)

    n_pairs = (n_tiles + 1) // 2
    groups = 2 * K_TILE // SUBLANES

    def count(pred):
        def body(kp, c):
            t = key_scr[pl.ds(pl.multiple_of(kp * 2 * K_TILE, 2 * K_TILE), 2 * K_TILE), :]
            return c + _fold_rows(jnp.where(pred(t), 1.0, 0.0), jnp.sum)
        c = lax.fori_loop(0, n_pairs, body, jnp.zeros((SUBLANES, Q_TILE), F32))
        return jnp.sum(c, axis=0, keepdims=True)

    kcount = jnp.minimum(n_keep, qpos + 1).astype(F32)
    thr = _kth_largest_key(lambda c: count(lambda t: t >= c), kcount)
    need = kcount - count(lambda t: t > thr)
    tri = jnp.where(lax.broadcasted_iota(jnp.int32, (K_TILE, K_TILE), 0)
                    >= lax.broadcasted_iota(jnp.int32, (K_TILE, K_TILE), 1), 1.0, 0.0).astype(BF16)

    def mask_body(kj, seen):
        k0 = tile_start(kj)
        t = key_scr[pl.ds(k0, K_TILE), :]
        eq = jnp.where(t == thr, 1.0, 0.0)
        rank = jnp.dot(tri, eq.astype(BF16), preferred_element_type=F32) + seen
        keep = jnp.where(t > thr, 1.0, jnp.where(rank <= need, eq, 0.0))
        mask_scr[pl.ds(k0, K_TILE), :] = jnp.where(keep > 0.5, 0.0, NEG_BIG).astype(BF16)
        return seen + jnp.sum(eq, axis=0, keepdims=True)

    lax.fori_loop(0, n_tiles, mask_body, jnp.zeros((1, Q_TILE), F32))
    qpair = _query_pairs(q_ref)

    @pl.when(n_tiles % 2 == 1)
    def _():
        mask_scr[pl.ds(tile_start(n_tiles), K_TILE), :] = jnp.full((K_TILE, Q_TILE), NEG_BIG, BF16)

    def body(g, _):
        tiles = []
        for u in range(FLASH_GROUP):
            k0 = tile_start(g * FLASH_GROUP + u)
            off = jnp.clip((q0 - k0) // Q_TILE, 0, N_NEAR)
            tiles.append((k0, off, mask_scr[pl.ds(k0, K_TILE), :].astype(F32), None))
        _flash_group(k_ref, vt_ref, bias_ref, qpair, tiles, fs)
        return 0

    lax.fori_loop(0, n_pairs, body, 0)
    fs.write(o_ref)


def dsa_prompt(q, k, v, qi, ki, wi, bias_tiles):
    b, t, w = q.shape
    assert t % (2 * K_TILE) == 0 and K_TILE % Q_TILE == 0 and FLASH_GROUP == 2
    n_keep = min(DSA_TOPK, t // 4)
    qp = _pad_heads_t(q * PROMPT_Q_SCALE)
    kb = k.astype(BF16)
    vtb = jnp.transpose(v, (0, 2, 1)).astype(BF16)
    ki_hi, ki_lo = _split_bf16(ki)
    kic = jnp.concatenate([ki_hi, ki_hi, ki_lo], axis=-1)
    qis = jnp.transpose((qi * IDX_DIM ** -0.5).reshape(b, t, IDX_HEADS, IDX_DIM), (0, 2, 3, 1))
    qi_hi, qi_lo = _split_bf16(qis)
    qic = jnp.concatenate([qi_hi, qi_lo, qi_hi], axis=2)
    wit = jnp.transpose(wi * IDX_HEADS ** -0.5, (0, 2, 1))
    ot = pl.pallas_call(
        functools.partial(_dsa_prompt_kernel, n_keep=n_keep),
        grid=(b, t // Q_TILE),
        in_specs=[pl.BlockSpec((None, N_HEADS, LANES, Q_TILE), lambda i, j: (i, 0, 0, j)),
                  pl.BlockSpec((None, t, w), lambda i, j: (i, 0, 0)),
                  pl.BlockSpec((None, w, t), lambda i, j: (i, 0, 0)),
                  pl.BlockSpec((None, t, 3 * IDX_DIM), lambda i, j: (i, 0, 0)),
                  pl.BlockSpec((None, IDX_HEADS, 3 * IDX_DIM, Q_TILE), lambda i, j: (i, 0, 0, j)),
                  pl.BlockSpec((None, IDX_HEADS, Q_TILE), lambda i, j: (i, 0, j)),
                  pl.BlockSpec(bias_tiles.shape, lambda i, j: (0, 0, 0, 0))],
        out_specs=pl.BlockSpec((None, w, Q_TILE), lambda i, j: (i, 0, j)),
        out_shape=jax.ShapeDtypeStruct((b, w, t), F32),
        scratch_shapes=[pltpu.VMEM((t, Q_TILE), jnp.int32), pltpu.VMEM((t, Q_TILE), BF16)]
        + _FlashScratch.shapes(),
        compiler_params=_cparams("parallel", "arbitrary"),
        name="dsa_prompt",
    )(qp, kb, vtb, kic, qic, wit, bias_tiles)
    return jnp.transpose(ot, (0, 2, 1))


HEADS_PER_GROUP = N_HEADS // SSM_GROUPS
PAIR_W = HEADS_PER_GROUP * HEAD_DIM
assert PAIR_W == LANES and SSM_DSTATE == LANES and SSM_CHUNK == LANES


def _ssd_prompt_kernel(xdt_ref, b_ref, c_ref, dax_ref, dat_ref, h0_ref, y_ref, hout_ref, st_scr):
    ci = pl.program_id(1)

    @pl.when(ci == 0)
    def _():
        st_scr[...] = h0_ref[...]

    q = SSM_CHUNK
    row = lax.broadcasted_iota(jnp.int32, (q, q), 0)
    col = lax.broadcasted_iota(jnp.int32, (q, q), 1)
    lower = row >= col
    hi = lax.Precision.HIGHEST
    cs_x = jnp.dot(jnp.where(lower, 1.0, 0.0), dax_ref[...], preferred_element_type=F32, precision=hi)
    cs_t = jnp.dot(dat_ref[...], jnp.where(row <= col, 1.0, 0.0), preferred_element_type=F32, precision=hi)
    lane_lo = col < HEAD_DIM
    for g in range(SSM_GROUPS):
        sl = slice(g * PAIR_W, (g + 1) * PAIR_W)
        cg = c_ref[:, g * SSM_DSTATE:(g + 1) * SSM_DSTATE]
        bg = b_ref[:, g * SSM_DSTATE:(g + 1) * SSM_DSTATE]
        xdt = xdt_ref[:, sl]
        cs_pair = cs_x[:, sl]
        cs_swap = pltpu.roll(cs_pair, HEAD_DIM, 1)
        cb = lax.dot_general(cg, bg, (((1,), (1,)), ((), ())), preferred_element_type=F32)
        y_pair = []
        for a in range(HEADS_PER_GROUP):
            h = g * HEADS_PER_GROUP + a
            own_half = lane_lo if a == 0 else jnp.logical_not(lane_lo)
            cs_col = jnp.where(own_half, cs_pair, cs_swap)
            seg = jnp.exp(jnp.minimum(cs_col - cs_t[h:h + 1, :], 0.0))
            scores = cb * jnp.where(lower, seg, 0.0)
            y_pair.append(jnp.dot(scores, xdt, preferred_element_type=F32))
        y_diag = jnp.where(lane_lo, y_pair[0], y_pair[1])
        st = st_scr[sl, :]
        y_off = lax.dot_general(cg, st, (((1,), (1,)), ((), ())), preferred_element_type=F32)
        y_ref[:, sl] = y_diag + y_off * jnp.exp(cs_pair)
        cs_last = cs_pair[q - 1:q, :]
        xdtd = xdt * jnp.exp(cs_last - cs_pair)
        upd = jnp.dot(xdtd.T, bg, preferred_element_type=F32)
        last_a = cs_last[:, 0:1]
        last_b = cs_last[:, HEAD_DIM:HEAD_DIM + 1]
        st_scr[sl, :] = st * jnp.exp(jnp.where(row < HEAD_DIM, last_a, last_b)) + upd

    @pl.when(ci == pl.num_programs(1) - 1)
    def _():
        hout_ref[...] = st_scr[...]


def ssd_prompt(xs, dt, a_neg, bm, cm, h0):
    b, t, w = xs.shape
    assert t % SSM_CHUNK == 0
    rep = lambda z: jnp.repeat(z, HEAD_DIM, axis=-1)
    xdt = xs * rep(dt)
    da = dt * a_neg
    dat = jnp.pad(jnp.transpose(da, (0, 2, 1)), ((0, 0), (0, SUBLANES - N_HEADS), (0, 0)))
    hp = N_HEADS * HEAD_DIM
    chunk = lambda width: pl.BlockSpec((None, SSM_CHUNK, width), lambda i, j: (i, j, 0))
    state = pl.BlockSpec((None, hp, SSM_DSTATE), lambda i, j: (i, 0, 0))
    y, h = pl.pallas_call(
        _ssd_prompt_kernel,
        grid=(b, t // SSM_CHUNK),
        in_specs=[chunk(w), chunk(bm.shape[-1]), chunk(cm.shape[-1]), chunk(w),
                  pl.BlockSpec((None, SUBLANES, SSM_CHUNK), lambda i, j: (i, 0, j)), state],
        out_specs=[chunk(w), state],
        out_shape=[jax.ShapeDtypeStruct((b, t, w), F32), jax.ShapeDtypeStruct((b, hp, SSM_DSTATE), F32)],
        scratch_shapes=[pltpu.VMEM((hp, SSM_DSTATE), F32)],
        compiler_params=_cparams("parallel", "arbitrary"),
        name="ssd_prompt",
    )(xdt, bm, cm, rep(da), dat, h0.reshape(b, hp, SSM_DSTATE))
    return y, h.reshape(b, N_HEADS, HEAD_DIM, SSM_DSTATE)


RWKV_CHUNK = 128
HEAD_PAIRS = N_HEADS // 2


def _split_cat(x):
    hi, lo = _split_bf16(x)
    return jnp.concatenate([hi, lo], axis=1)


def _rwkv_prompt_kernel(r_ref, w_ref, k_ref, kk_ref, kka_ref, v_ref, s0_ref, o_ref, sout_ref,
                        s_scr, vt_scr, ot_scr):
    ci = pl.program_id(0)
    n_b = r_ref.shape[0]
    tiles = [(b, p) for b in range(n_b) for p in range(HEAD_PAIRS)]
    lanes = lambda p: slice(p * LANES, (p + 1) * LANES)

    @pl.when(ci == 0)
    def _():
        s_scr[...] = s0_ref[...]

    for i, (b, p) in enumerate(tiles):
        vt_scr[i] = v_ref[b, :, lanes(p)].T
    ot_scr[...] = jnp.zeros(ot_scr.shape, F32)
    r2 = lax.broadcasted_iota(jnp.int32, (2 * LANES, LANES), 0)
    c2 = lax.broadcasted_iota(jnp.int32, (2 * LANES, LANES), 1)
    ones_same = jnp.where((r2 % LANES) // HEAD_DIM == c2 // HEAD_DIM, 1.0, 0.0).astype(BF16)
    r3 = lax.broadcasted_iota(jnp.int32, (LANES, 2 * LANES), 0)
    c3 = lax.broadcasted_iota(jnp.int32, (LANES, 2 * LANES), 1)
    ones_out = jnp.where(r3 // HEAD_DIM == c3 // LANES, 1.0, 0.0).astype(BF16)
    lane_t = lax.broadcasted_iota(jnp.int32, (1, RWKV_CHUNK), 1)
    head_a = lax.broadcasted_iota(jnp.int32, (HEAD_DIM, LANES), 1) < HEAD_DIM

    def step(t, rows):
        at_t = lane_t == t
        row = lambda ref, b, p: rows(ref, b, p)
        states = [s_scr[i] for i in range(len(tiles))]
        x = jnp.concatenate([_split_cat(s * row(kk_ref, b, p)) for s, (b, p) in zip(states, tiles)], axis=0)
        sa_all = -jnp.dot(x, ones_same, preferred_element_type=F32)
        new_states = []
        for i, (b, p) in enumerate(tiles):
            sa = sa_all[i * HEAD_DIM:(i + 1) * HEAD_DIM]
            v_a = jnp.sum(jnp.where(at_t, vt_scr[i, :HEAD_DIM, :], 0.0), axis=1, keepdims=True)
            v_b = jnp.sum(jnp.where(at_t, vt_scr[i, HEAD_DIM:, :], 0.0), axis=1, keepdims=True)
            s = (states[i] * row(w_ref, b, p) + sa * row(kka_ref, b, p)
                 + jnp.where(head_a, v_a, v_b) * row(k_ref, b, p))
            s_scr[i] = s
            new_states.append(s)
        y = jnp.concatenate([(s * row(r_ref, b, p)).astype(BF16) for s, (b, p) in zip(new_states, tiles)], axis=0)
        o_all = jnp.dot(y, ones_out, preferred_element_type=F32)
        for i in range(len(tiles)):
            o = o_all[i * HEAD_DIM:(i + 1) * HEAD_DIM]
            ot_scr[i, :HEAD_DIM, :] = jnp.where(at_t, o[:, :LANES], ot_scr[i, :HEAD_DIM, :])
            ot_scr[i, HEAD_DIM:, :] = jnp.where(at_t, o[:, LANES:], ot_scr[i, HEAD_DIM:, :])

    def step_group(g, _):
        base = pl.multiple_of(g * SUBLANES, SUBLANES)
        refs = (r_ref, w_ref, k_ref, kk_ref, kka_ref)
        blocks = {(id(ref), b, p): ref[b, pl.ds(base, SUBLANES), lanes(p)] for ref in refs for b, p in tiles}
        for j in range(SUBLANES):
            step(base + j, lambda ref, b, p, j=j: blocks[(id(ref), b, p)][j:j + 1, :])
        return 0

    lax.fori_loop(0, RWKV_CHUNK // SUBLANES, step_group, 0)
    for i, (b, p) in enumerate(tiles):
        o_ref[b, :, lanes(p)] = ot_scr[i].T

    @pl.when(ci == pl.num_programs(0) - 1)
    def _():
        sout_ref[...] = s_scr[...]


def rwkv_prompt(r, w, k, v, kk, kka, s0):
    b, t, hw = r.shape
    assert t % RWKV_CHUNK == 0 and RWKV_CHUNK == LANES and 2 * HEAD_DIM == LANES
    n_tiles = b * HEAD_PAIRS
    pack = lambda s: jnp.transpose(s.reshape(b, HEAD_PAIRS, 2, HEAD_DIM, HEAD_DIM), (0, 1, 3, 2, 4)).reshape(
        n_tiles, HEAD_DIM, LANES)
    unpack = lambda s: jnp.transpose(s.reshape(b, HEAD_PAIRS, HEAD_DIM, 2, HEAD_DIM), (0, 1, 3, 2, 4)).reshape(
        b, N_HEADS, HEAD_DIM, HEAD_DIM)
    seq_spec = pl.BlockSpec((b, RWKV_CHUNK, hw), lambda c: (0, c, 0))
    st_spec = pl.BlockSpec((n_tiles, HEAD_DIM, LANES), lambda c: (0, 0, 0))
    o, s = pl.pallas_call(
        _rwkv_prompt_kernel,
        grid=(t // RWKV_CHUNK,),
        in_specs=[seq_spec] * 6 + [st_spec],
        out_specs=[seq_spec, st_spec],
        out_shape=[jax.ShapeDtypeStruct((b, t, hw), F32), jax.ShapeDtypeStruct((n_tiles, HEAD_DIM, LANES), F32)],
        scratch_shapes=[pltpu.VMEM((n_tiles, HEAD_DIM, LANES), F32),
                        pltpu.VMEM((n_tiles, LANES, RWKV_CHUNK), F32),
                        pltpu.VMEM((n_tiles, LANES, RWKV_CHUNK), F32)],
        compiler_params=_cparams("arbitrary"),
        name="rwkv_prompt",
    )(r, w, k, kk, kka, v, pack(s0))
    return o, unpack(s)


STEP_SEQS = 8
NT_DIMS = (((1,), (1,)), ((), ()))


def _row_dot(row, mat):
    r8 = jnp.broadcast_to(row, (SUBLANES, row.shape[1]))
    return lax.dot_general(r8, mat, NT_DIMS, preferred_element_type=F32,
                           precision=lax.Precision.HIGHEST)[0:1, :]


def _ssd_step_kernel(h0_ref, xdtb_ref, dab_ref, b_ref, c_ref, h_ref, y_ref):
    for s in range(h0_ref.shape[0]):
        for h in range(N_HEADS):
            hn = h0_ref[s, h] * jnp.exp(dab_ref[s, h]) + xdtb_ref[s, h] * b_ref[s, h]
            h_ref[s, h] = hn
            y_ref[s, h] = _row_dot(c_ref[s, h], hn)


def ssd_step(xs, dt, a_neg, bm, cm, h0):
    s = xs.shape[0]
    assert s % STEP_SEQS == 0
    xdtb = jnp.broadcast_to((xs.reshape(s, N_HEADS, HEAD_DIM) * dt[:, :, None])[..., None],
                            (s, N_HEADS, HEAD_DIM, SSM_DSTATE))
    dab = jnp.broadcast_to((dt * a_neg)[:, :, None, None], (s, N_HEADS, 1, SSM_DSTATE))
    per_head = lambda z: jnp.repeat(z.reshape(s, SSM_GROUPS, 1, SSM_DSTATE), HEADS_PER_GROUP, axis=1)
    big = pl.BlockSpec((STEP_SEQS, N_HEADS, HEAD_DIM, SSM_DSTATE), lambda i: (i, 0, 0, 0))
    row = pl.BlockSpec((STEP_SEQS, N_HEADS, 1, SSM_DSTATE), lambda i: (i, 0, 0, 0))
    h, y = pl.pallas_call(
        _ssd_step_kernel,
        grid=(s // STEP_SEQS,),
        in_specs=[big, big, row, row, row],
        out_specs=[big, pl.BlockSpec((STEP_SEQS, N_HEADS, 1, HEAD_DIM), lambda i: (i, 0, 0, 0))],
        out_shape=[jax.ShapeDtypeStruct(h0.shape, F32), jax.ShapeDtypeStruct((s, N_HEADS, 1, HEAD_DIM), F32)],
        compiler_params=_cparams("parallel"),
        name="ssd_step",
    )(h0, xdtb, dab, per_head(bm), per_head(cm))
    return y.reshape(s, N_HEADS * HEAD_DIM), h


def _rwkv_step_kernel(s0_ref, vb_ref, w_ref, kk_ref, kka_ref, k_ref, r_ref, s_ref, o_ref):
    for s in range(s0_ref.shape[0]):
        for h in range(N_HEADS):
            st = s0_ref[s, h]
            sa = -jnp.sum(st * kk_ref[s, h], axis=1, keepdims=True)
            st = st * w_ref[s, h] + sa * kka_ref[s, h] + vb_ref[s, h] * k_ref[s, h]
            s_ref[s, h] = st
            o_ref[s, h] = _row_dot(r_ref[s, h], st)


def rwkv_step(r, w, k, v, kk, kka, s0):
    s = r.shape[0]
    assert s % STEP_SEQS == 0
    rows = lambda z: z.reshape(s, N_HEADS, 1, HEAD_DIM)
    vb = jnp.broadcast_to(v.reshape(s, N_HEADS, HEAD_DIM, 1), (s, N_HEADS, HEAD_DIM, HEAD_DIM))
    big = pl.BlockSpec((STEP_SEQS, N_HEADS, HEAD_DIM, HEAD_DIM), lambda i: (i, 0, 0, 0))
    row = pl.BlockSpec((STEP_SEQS, N_HEADS, 1, HEAD_DIM), lambda i: (i, 0, 0, 0))
    st, o = pl.pallas_call(
        _rwkv_step_kernel,
        grid=(s // STEP_SEQS,),
        in_specs=[big, big, row, row, row, row, row],
        out_specs=[big, row],
        out_shape=[jax.ShapeDtypeStruct(s0.shape, F32), jax.ShapeDtypeStruct((s, N_HEADS, 1, HEAD_DIM), F32)],
        compiler_params=_cparams("parallel"),
        name="rwkv_step",
    )(s0, vb, rows(w), rows(kk), rows(kka), rows(k), rows(r))
    return o.reshape(s, N_HEADS * HEAD_DIM), st


def _head_rows_to_row(o):
    r = lax.broadcasted_iota(jnp.int32, o.shape, 0)
    c = lax.broadcasted_iota(jnp.int32, o.shape, 1)
    return jnp.sum(jnp.where(c // HEAD_DIM == r, o, 0.0), axis=0, keepdims=True)


def _sample_softmax_pv(logit_pages, s_new, vt_pages, vnew):
    m = s_new
    for s in logit_pages:
        m = jnp.maximum(m, jnp.max(s, axis=1, keepdims=True))
    p_new = jnp.exp(s_new - m)
    l = p_new
    acc = p_new * vnew
    for s, vt_ref in zip(logit_pages, vt_pages):
        p = jnp.exp(s - m)
        l = l + jnp.sum(p, axis=1, keepdims=True)
        acc = acc + lax.dot_general(p.astype(BF16), vt_ref[...].astype(BF16), NT_DIMS, preferred_element_type=F32)
    return _head_rows_to_row(acc / l)


def _moba_sample_kernel(pt_ref, qbd_ref, knew_ref, vnew_ref, bias_ref, biasn_ref, *rest, n_pages, page):
    del pt_ref
    kt_pages, vt_pages, o_ref = rest[:n_pages], rest[n_pages:2 * n_pages], rest[2 * n_pages]
    qbd = qbd_ref[...]
    q_hi, q_lo = _split_bf16(qbd)
    q_both = jnp.concatenate([q_hi, q_lo], axis=0)
    ppb = MOBA_BLOCK // page
    n_blocks = n_pages // ppb
    lane = lax.broadcasted_iota(jnp.int32, (SUBLANES, LANES), 1)
    raw = []
    score = jnp.zeros((SUBLANES, LANES), F32)
    for p in range(n_pages):
        k_hi, k_lo = _split_bf16(kt_pages[p][...])
        r = jnp.dot(q_both, k_hi, preferred_element_type=F32)
        r = r[:SUBLANES] + r[SUBLANES:] + jnp.dot(q_hi, k_lo, preferred_element_type=F32)
        raw.append(r)
        score = score + jnp.where(lane == p // ppb, jnp.sum(r, axis=1, keepdims=True) * (1.0 / MOBA_BLOCK), 0.0)
    sel = _top_blocks_mask(score, n_blocks, min(MOBA_TOPK, n_blocks + 1), 1)
    logits = [raw[p] + bias_ref[:, p * page:(p + 1) * page] + sel[:, p // ppb:p // ppb + 1]
              for p in range(n_pages)]
    s_new = jnp.sum(qbd * knew_ref[...], axis=1, keepdims=True) + biasn_ref[:, 0:1]
    o_ref[...] = _sample_softmax_pv(logits, s_new, vt_pages, vnew_ref[...])


def _block_diag_q(q):
    s = q.shape[0]
    eye = jnp.eye(N_HEADS, dtype=q.dtype)
    bd = (q.reshape(s, 1, N_HEADS, HEAD_DIM) * eye[None, :, :, None]).reshape(s, N_HEADS, N_HEADS * HEAD_DIM)
    return jnp.pad(bd, ((0, 0), (0, SUBLANES - N_HEADS), (0, 0)))


def _sample_bias(tab, past_len):
    dist = past_len - jnp.arange(past_len, dtype=jnp.int32)
    pad = ((0, SUBLANES - N_HEADS), (0, 0))
    past = jnp.pad(_bucket_lookup(tab, _rel_bucket(dist)), pad)
    new = jnp.pad(_bucket_lookup(tab, _rel_bucket(jnp.zeros((LANES,), jnp.int32))), pad)
    return past, new


def _page_specs(layer, n_pages, rows, page):
    return [pl.BlockSpec((None, None, rows, page), functools.partial(
        lambda s, pt, p: (layer, pt[s, p], 0, 0), p=p)) for p in range(n_pages)]


def _transposed_pages(cache):
    c = cache.reshape(cache.shape[:3] + (-1,))
    return jnp.transpose(c, (0, 1, 3, 2))


def moba_sample(q, knew, vnew, cache_kt, cache_vt, layer, page_table, tab):
    s, w = q.shape
    n_pages, page = page_table.shape[1], cache_kt.shape[3]
    past_len = n_pages * page
    assert past_len % MOBA_BLOCK == 0 and MOBA_BLOCK % page == 0
    bias_p, bias_n = _sample_bias(tab, past_len)
    row = pl.BlockSpec((None, 1, w), lambda i, pt: (i, 0, 0))
    const = lambda shape: pl.BlockSpec(shape, lambda i, pt: (0, 0))
    out = pl.pallas_call(
        functools.partial(_moba_sample_kernel, n_pages=n_pages, page=page),
        grid_spec=pltpu.PrefetchScalarGridSpec(
            num_scalar_prefetch=1, grid=(s,),
            in_specs=[pl.BlockSpec((None, SUBLANES, w), lambda i, pt: (i, 0, 0)), row, row,
                      const(bias_p.shape), const(bias_n.shape)]
            + _page_specs(layer, n_pages, w, page) * 2,
            out_specs=row),
        out_shape=jax.ShapeDtypeStruct((s, 1, w), F32),
        compiler_params=_cparams("arbitrary"),
        name="moba_sample",
    )(page_table, _block_diag_q(q * HEAD_DIM ** -0.5), knew[:, None], vnew[:, None], bias_p, bias_n,
      *([cache_kt] * n_pages), *([cache_vt] * n_pages))
    return out.reshape(s, w)


def _dsa_sample_kernel(pt_ref, qbd_ref, knew_ref, vnew_ref, bias_ref, biasn_ref, qi_ref, wi_ref, kin_ref,
                       *rest, n_pages, page, n_keep):
    del pt_ref
    kt_pages, vt_pages = rest[:n_pages], rest[n_pages:2 * n_pages]
    kit_pages, o_ref = rest[2 * n_pages:3 * n_pages], rest[3 * n_pages]
    qi = qi_ref[...]
    wi = wi_ref[:, 0:1]
    hi = lax.Precision.HIGHEST
    keys = []
    for p in range(n_pages):
        d = jnp.dot(qi, kit_pages[p][...], preferred_element_type=F32, precision=hi)
        keys.append(_ordered_key(jnp.sum(wi * jnp.maximum(d, 0.0), axis=0, keepdims=True)))
    d_new = jnp.sum(qi * kin_ref[...], axis=1, keepdims=True)
    key_new = _ordered_key(jnp.sum(wi * jnp.maximum(d_new, 0.0), axis=0, keepdims=True))

    key_all = jnp.concatenate(keys, axis=0)

    def total(x):
        return jnp.sum(jnp.sum(x, axis=0, keepdims=True), axis=1, keepdims=True)

    def count(pred):
        return jnp.where(pred(key_new), 1.0, 0.0) + total(jnp.where(pred(key_all), 1.0, 0.0))

    kcount = jnp.full((1, 1), float(n_keep), F32)
    thr = _kth_largest_key_radix(lambda c: count(lambda t: t >= c), kcount)
    need = kcount - count(lambda t: t > thr)
    eq = jnp.where(key_all == thr, 1.0, 0.0)
    upper = jnp.where(lax.broadcasted_iota(jnp.int32, (page, page), 0)
                      <= lax.broadcasted_iota(jnp.int32, (page, page), 1), 1.0, 0.0).astype(BF16)
    before = jnp.where(lax.broadcasted_iota(jnp.int32, (n_pages, n_pages), 1)
                       < lax.broadcasted_iota(jnp.int32, (n_pages, n_pages), 0), 1.0, 0.0).astype(BF16)
    per_page = jnp.broadcast_to(jnp.sum(eq, axis=1, keepdims=True), eq.shape).astype(BF16)
    rank = (jnp.dot(eq.astype(BF16), upper, preferred_element_type=F32)
            + jnp.dot(before, per_page, preferred_element_type=F32))
    keep = jnp.where(key_all > thr, 1.0, jnp.where(rank <= need, eq, 0.0))
    mask_all = jnp.where(keep > 0.5, 0.0, NEG_BIG)
    masks = [mask_all[p:p + 1, :] for p in range(n_pages)]
    keep_new = jnp.where(key_new > thr, 1.0,
                         jnp.where(key_new == thr, jnp.where(total(eq) + 1.0 <= need, 1.0, 0.0), 0.0))
    mask_new = jnp.where(keep_new > 0.5, 0.0, NEG_BIG)
    qbd = qbd_ref[...]
    qb = qbd.astype(BF16)
    logits = [jnp.dot(qb, kt_pages[p][...].astype(BF16), preferred_element_type=F32)
              + bias_ref[:, p * page:(p + 1) * page] + masks[p] for p in range(n_pages)]
    s_new = jnp.sum(qbd * knew_ref[...], axis=1, keepdims=True) + biasn_ref[:, 0:1] + mask_new
    o_ref[...] = _sample_softmax_pv(logits, s_new, vt_pages, vnew_ref[...])


def dsa_sample(q, knew, vnew, qi, wi, kinew, cache_kt, cache_vt, cache_kit, layer, page_table, tab):
    s, w = q.shape
    n_pages, page = page_table.shape[1], cache_kt.shape[3]
    past_len = n_pages * page
    n_keep = min(DSA_TOPK, (past_len + 1) // 4)
    bias_p, bias_n = _sample_bias(tab, past_len)
    row = pl.BlockSpec((None, 1, w), lambda i, pt: (i, 0, 0))
    const = lambda shape: pl.BlockSpec(shape, lambda i, pt: (0, 0))
    qis = (qi * IDX_DIM ** -0.5).reshape(s, IDX_HEADS, IDX_DIM)
    wib = jnp.broadcast_to((wi * IDX_HEADS ** -0.5)[:, :, None], (s, IDX_HEADS, LANES))
    out = pl.pallas_call(
        functools.partial(_dsa_sample_kernel, n_pages=n_pages, page=page, n_keep=n_keep),
        grid_spec=pltpu.PrefetchScalarGridSpec(
            num_scalar_prefetch=1, grid=(s,),
            in_specs=[pl.BlockSpec((None, SUBLANES, w), lambda i, pt: (i, 0, 0)), row, row,
                      const(bias_p.shape), const(bias_n.shape),
                      pl.BlockSpec((None, IDX_HEADS, IDX_DIM), lambda i, pt: (i, 0, 0)),
                      pl.BlockSpec((None, IDX_HEADS, LANES), lambda i, pt: (i, 0, 0)),
                      pl.BlockSpec((None, 1, IDX_DIM), lambda i, pt: (i, 0, 0))]
            + _page_specs(layer, n_pages, w, page) * 2 + _page_specs(layer, n_pages, IDX_DIM, page),
            out_specs=row),
        out_shape=jax.ShapeDtypeStruct((s, 1, w), F32),
        compiler_params=_cparams("arbitrary"),
        name="dsa_sample",
    )(page_table, _block_diag_q(q * HEAD_DIM ** -0.5), knew[:, None], vnew[:, None], bias_p, bias_n,
      qis, wib, kinew[:, None], *([cache_kt] * n_pages), *([cache_vt] * n_pages), *([cache_kit] * n_pages))
    return out.reshape(s, w)


N_IN = sum(IN_SPLITS)
N_IN_PAD = -(-N_IN // LANES) * LANES


def _rms_norm(x, g):
    return x * lax.rsqrt(jnp.mean(x * x, axis=-1, keepdims=True) + NORM_EPS) * g


def _split_cols(a, sizes):
    return jnp.split(a, [int(i) for i in np.cumsum(sizes)[:-1]], axis=-1)


def _causal_dwconv(x, buf, w, b):
    width = w.shape[0]
    length = x.shape[1]
    xp = jnp.concatenate([buf, x], axis=1)
    y = b + sum(xp[:, j:j + length] * w[j] for j in range(width))
    return y, xp[:, length:]


def _tokens_matmul(x, w):
    b, t, k = x.shape
    return matmul(x.reshape(b * t, k), w).reshape(b, t, w.shape[1])


def _mamba(pz, pxbc, pdt, conv_buf, h0, conv_w, conv_b, dt_bias, a_log, d_skip, norm_g):
    b, t, _ = pz.shape
    xbc, new_buf = _causal_dwconv(pxbc, conv_buf, conv_w, conv_b)
    xbc = jax.nn.silu(xbc)
    xs, bm, cm = jnp.split(xbc, [BRANCH_W, BRANCH_W + SSM_GROUPS * SSM_DSTATE], axis=-1)
    dt = jax.nn.softplus(pdt + dt_bias)
    a_neg = -jnp.exp(a_log)
    if t == 1:
        y, h_new = ssd_step(xs[:, 0], dt[:, 0], a_neg, bm[:, 0], cm[:, 0], h0)
        y = y[:, None]
    else:
        y, h_new = ssd_prompt(xs, dt, a_neg, bm, cm, h0)
    y = y + jnp.repeat(d_skip, HEAD_DIM) * xs
    return _rms_norm(y * jax.nn.silu(pz), norm_g), new_buf, h_new


def _rwkv(p, shift_prev, s0, mu, w0, w2, a0, a2, g2, k_k, k_a, r_k, ln_w, ln_b):
    b, t, _ = p.shape
    p_prev = jnp.concatenate([shift_prev[:, None], p[:, :-1]], axis=1)
    pm = p + (p_prev - p) * mu
    r, k, v, xw, xa, xg = _split_cols(pm, (BRANCH_W, BRANCH_W, BRANCH_W, RWKV_W_LORA, RWKV_A_LORA, RWKV_G_LORA))
    logw = -jax.nn.softplus(-(w0 + _tokens_matmul(jnp.tanh(xw), w2))) - 0.5
    decay = jnp.exp(-jnp.exp(logw))
    a = jax.nn.sigmoid(a0 + _tokens_matmul(xa, a2))
    g = _tokens_matmul(jax.nn.sigmoid(xg), g2)
    heads = lambda z: z.reshape(b, t, N_HEADS, HEAD_DIM)
    kk = heads(k * k_k)
    kk = (kk * lax.rsqrt(jnp.maximum(jnp.sum(kk * kk, axis=-1, keepdims=True), 1e-24))).reshape(b, t, BRANCH_W)
    k = k * (1.0 + (a - 1.0) * k_a)
    if t == 1:
        o, s_new = rwkv_step(r[:, 0], decay[:, 0], k[:, 0], v[:, 0], kk[:, 0], (kk * a)[:, 0], s0)
        o = o[:, None]
    else:
        o, s_new = rwkv_prompt(r, decay, k, v, kk, kk * a, s0)
    o = heads(o)
    mean = jnp.mean(o, axis=-1, keepdims=True)
    var = jnp.mean(jnp.square(o - mean), axis=-1, keepdims=True)
    o = ((o - mean) * lax.rsqrt(var + RWKV_GN_EPS)).reshape(b, t, BRANCH_W) * ln_w + ln_b
    bonus = jnp.sum(heads(r) * heads(k) * r_k, axis=-1, keepdims=True) * heads(v)
    return (o + bonus.reshape(b, t, BRANCH_W)) * g, p[:, -1], s_new


PROJ_ORDER = (13, 12, 0, 1, 3, 4, 5, 6, 7, 8, 9, 2, 10, 11)
MERGE_ROWS = 256


def _merge_kernel(oa_ref, ob_ref, oc_ref, od_ref, g_ref, wb_ref, wo_ref, x_ref, out_ref):
    mix = None
    for n, o_ref in enumerate((oa_ref, ob_ref, oc_ref, od_ref)):
        pr = jnp.dot(o_ref[...].astype(BF16), wb_ref[n], preferred_element_type=F32)
        term = jax.nn.sigmoid(g_ref[:, n * D_MODEL:(n + 1) * D_MODEL]) * pr
        mix = term if mix is None else mix + term
    out_ref[...] = x_ref[...] + jnp.dot(mix.astype(BF16), wo_ref[...], preferred_element_type=F32)


def merge_branches(x, branches, proj, w_branch, w_out):
    m = x.shape[0]
    tm = _pick_tile(m, MERGE_ROWS, SUBLANES)
    tok = lambda width: pl.BlockSpec((tm, width), lambda i: (i, 0))
    return pl.pallas_call(
        _merge_kernel,
        grid=(m // tm,),
        in_specs=[tok(BRANCH_W)] * N_BRANCH + [tok(N_BRANCH * D_MODEL),
                  pl.BlockSpec((N_BRANCH, BRANCH_W, D_MODEL), lambda i: (0, 0, 0)),
                  pl.BlockSpec((D_MODEL, D_MODEL), lambda i: (0, 0)), tok(D_MODEL)],
        out_specs=tok(D_MODEL),
        out_shape=jax.ShapeDtypeStruct((m, D_MODEL), F32),
        compiler_params=_cparams("parallel"),
        name="merge_branches",
    )(*branches, proj, w_branch.astype(BF16), w_out.astype(BF16), x)


FFN_ROWS = 256


def _ffn_down_kernel(a_ref, bv_ref, halo_ref, cw_ref, cb_ref, wd_ref, x_ref, out_ref):
    a = a_ref[...]
    tm = a.shape[0]
    ext = jnp.concatenate([halo_ref[...], a], axis=0)
    y = cb_ref[...] + cw_ref[FFN_CONV - 1:FFN_CONV, :] * a
    for back in range(1, FFN_CONV):
        y = y + cw_ref[FFN_CONV - 1 - back:FFN_CONV - back, :] * ext[SUBLANES - back:SUBLANES - back + tm, :]
    g = (y * jax.nn.sigmoid(y)) * bv_ref[...]
    out_ref[...] = x_ref[...] + jnp.dot(g.astype(BF16), wd_ref[...], preferred_element_type=F32)


def ffn_down(x, up, buf, conv_w, conv_b, w_down):
    b, t, d = x.shape
    tm = _pick_tile(t, FFN_ROWS, SUBLANES)
    nt = t // tm
    keep = FFN_CONV - 1
    a_tail = up[..., :D_FF].reshape(b, nt, tm, D_FF)[:, :, tm - keep:, :]
    before = jnp.concatenate([buf[:, None], a_tail[:, :-1]], axis=1)
    halo = jnp.pad(before, ((0, 0), (0, 0), (SUBLANES - keep, 0), (0, 0))).reshape(b * nt, SUBLANES, D_FF)
    m = b * t
    out = pl.pallas_call(
        _ffn_down_kernel,
        grid=(m // tm,),
        in_specs=[pl.BlockSpec((tm, D_FF), lambda i: (i, 0)), pl.BlockSpec((tm, D_FF), lambda i: (i, 1)),
                  pl.BlockSpec((None, SUBLANES, D_FF), lambda i: (i, 0, 0)),
                  pl.BlockSpec((FFN_CONV, D_FF), lambda i: (0, 0)), pl.BlockSpec((1, D_FF), lambda i: (0, 0)),
                  pl.BlockSpec((D_FF, d), lambda i: (0, 0)), pl.BlockSpec((tm, d), lambda i: (i, 0))],
        out_specs=pl.BlockSpec((tm, d), lambda i: (i, 0)),
        out_shape=jax.ShapeDtypeStruct((m, d), F32),
        compiler_params=_cparams("parallel"),
        name="ffn_down",
    )(up.reshape(m, 2 * D_FF), up.reshape(m, 2 * D_FF), halo, conv_w, conv_b[None], w_down.astype(BF16),
      x.reshape(m, d))
    return out.reshape(b, t, d), up[:, t - keep:, :D_FF]


def _layer(x, l, past, prm, attn):
    ssm_h, ssm_buf, rw_s, rw_shift, ffn_buf = past
    b, t, _ = x.shape
    h = _rms_norm(x, prm["norm1_g"][l])
    w_cols = _split_cols(prm["w_in"][l], IN_SPLITS)
    w_in = jnp.concatenate([w_cols[i] for i in PROJ_ORDER] + [jnp.zeros((D_MODEL, N_IN_PAD - N_IN), F32)], axis=1)
    proj = _tokens_matmul(h, w_in)
    parts = dict(zip(PROJ_ORDER, _split_cols(proj[..., :N_IN], [IN_SPLITS[i] for i in PROJ_ORDER])))
    (s_z, s_xbc, s_dt, m_q, m_k, m_v, d_q, d_k, d_v, d_qi, d_ki, d_wi, p_rwkv, _) = (parts[i] for i in range(14))
    o_a, ssm_buf_new, ssm_h_new = _mamba(s_z, s_xbc, s_dt, ssm_buf, ssm_h, prm["ssm_conv_w"][l], prm["ssm_conv_b"][l],
                                         prm["ssm_dt_bias"][l], prm["ssm_A_log"][l], prm["ssm_D"][l],
                                         prm["ssm_norm_g"][l])
    o_b = attn("moba", l, m_q, m_k, m_v)
    o_c = attn("dsa", l, d_q, d_k, d_v, d_qi, d_ki, d_wi)
    o_d, rw_shift_new, rw_s_new = _rwkv(p_rwkv, rw_shift, rw_s, prm["rwkv_mu"][l], prm["rwkv_w0"][l], prm["rwkv_w2"][l],
                                        prm["rwkv_a0"][l], prm["rwkv_a2"][l], prm["rwkv_g2"][l], prm["rwkv_k_k"][l],
                                        prm["rwkv_k_a"][l], prm["rwkv_r_k"][l], prm["rwkv_ln_w"][l], prm["rwkv_ln_b"][l])
    flat = lambda z: z.reshape(b * t, z.shape[-1])
    x = merge_branches(flat(x), [flat(o) for o in (o_a, o_b, o_c, o_d)], flat(proj), prm["w_branch"][l],
                       prm["w_out"][l]).reshape(b, t, D_MODEL)
    up = _tokens_matmul(_rms_norm(x, prm["norm2_g"][l]), prm["ffn_w_up"][l])
    if t >= FFN_CONV:
        x, ffn_buf_new = ffn_down(x, up, ffn_buf, prm["ffn_conv_w"][l], prm["ffn_conv_b"][l], prm["ffn_w_down"][l])
    else:
        a, bv = jnp.split(up, 2, axis=-1)
        a, ffn_buf_new = _causal_dwconv(a, ffn_buf, prm["ffn_conv_w"][l], prm["ffn_conv_b"][l])
        x = x + _tokens_matmul(jax.nn.silu(a) * bv, prm["ffn_w_down"][l])
    heads = lambda z: z.reshape(b, t, N_HEADS, HEAD_DIM)
    new = (heads(m_k), heads(m_v), heads(d_k), heads(d_v), d_ki, ssm_h_new, ssm_buf_new, rw_s_new, rw_shift_new,
           ffn_buf_new)
    return x, new


def kernel(x_prompt, x_sample, cache_moba_k, cache_moba_v, cache_dsa_k, cache_dsa_v, cache_dsa_kidx, page_table,
           state_ssm, state_ssm_conv, state_rwkv, state_rwkv_shift, state_ffn_conv,
           norm1_g, w_in, ssm_conv_w, ssm_conv_b, ssm_dt_bias, ssm_A_log, ssm_D, ssm_norm_g,
           rwkv_mu, rwkv_w0, rwkv_w2, rwkv_a0, rwkv_a2, rwkv_g2, rwkv_k_k, rwkv_k_a, rwkv_r_k, rwkv_ln_w, rwkv_ln_b,
           w_branch, w_out, norm2_g, ffn_w_up, ffn_conv_w, ffn_conv_b, ffn_w_down, rel_bias, final_g):
    prm = dict(norm1_g=norm1_g, w_in=w_in, ssm_conv_w=ssm_conv_w, ssm_conv_b=ssm_conv_b, ssm_dt_bias=ssm_dt_bias,
               ssm_A_log=ssm_A_log, ssm_D=ssm_D, ssm_norm_g=ssm_norm_g, rwkv_mu=rwkv_mu, rwkv_w0=rwkv_w0,
               rwkv_w2=rwkv_w2, rwkv_a0=rwkv_a0, rwkv_a2=rwkv_a2, rwkv_g2=rwkv_g2, rwkv_k_k=rwkv_k_k,
               rwkv_k_a=rwkv_k_a, rwkv_r_k=rwkv_r_k, rwkv_ln_w=rwkv_ln_w, rwkv_ln_b=rwkv_ln_b, w_branch=w_branch,
               w_out=w_out, norm2_g=norm2_g, ffn_w_up=ffn_w_up, ffn_conv_w=ffn_conv_w, ffn_conv_b=ffn_conv_b,
               ffn_w_down=ffn_w_down)
    depth = w_in.shape[0]
    bp = x_prompt.shape[0]
    ns = x_sample.shape[0]
    assert x_sample.shape[1] == 1
    tab_moba, tab_dsa = rel_bias[:, :N_HEADS], rel_bias[:, N_HEADS:]
    tiles_moba, tiles_dsa = _bias_tiles(tab_moba * LOG2E), _bias_tiles(tab_dsa * LOG2E)
    ck_m, cv_m, ck_d, cv_d, cki_d = (_transposed_pages(c) for c in (cache_moba_k, cache_moba_v, cache_dsa_k,
                                                                    cache_dsa_v, cache_dsa_kidx))

    def attn_prompt(kind, l, q, k, v, qi=None, ki=None, wi=None):
        if kind == "moba":
            return moba_prompt(q, k, v, tiles_moba)
        return dsa_prompt(q, k, v, qi, ki, wi, tiles_dsa)

    def attn_sample(kind, l, q, k, v, qi=None, ki=None, wi=None):
        if kind == "moba":
            return moba_sample(q[:, 0], k[:, 0], v[:, 0], ck_m, cv_m, l, page_table, tab_moba)[:, None]
        return dsa_sample(q[:, 0], k[:, 0], v[:, 0], qi[:, 0], wi[:, 0], ki[:, 0], ck_d, cv_d, cki_d,
                          l, page_table, tab_dsa)[:, None]

    zp = lambda *shape: jnp.zeros((bp,) + shape, F32)
    prompt_past = (zp(N_HEADS, HEAD_DIM, SSM_DSTATE), zp(SSM_CONV - 1, SSM_CONV_DIM),
                   zp(N_HEADS, HEAD_DIM, HEAD_DIM), zp(RWKV_IN), zp(FFN_CONV - 1, D_FF))
    xp, xs = x_prompt, x_sample
    new_p, new_s = [], []
    for l in range(depth):
        sample_past = (state_ssm[l], state_ssm_conv[l], state_rwkv[l], state_rwkv_shift[l], state_ffn_conv[l])
        xp, np_l = _layer(xp, l, prompt_past, prm, attn_prompt)
        xs, ns_l = _layer(xs, l, sample_past, prm, attn_sample)
        new_p.append(np_l)
        new_s.append(ns_l)
    outs = [_rms_norm(xp, final_g), _rms_norm(xs, final_g)]
    for i in range(10):
        outs.append(jnp.stack([o[i] for o in new_p], axis=0))
        outs.append(jnp.stack([o[i] for o in new_s], axis=0))
    return tuple(outs)
```

```python
import functools
import math

import jax
import jax.numpy as jnp
import numpy as np
from jax import lax
from jax.experimental import pallas as pl
from jax.experimental.pallas import tpu as pltpu

D_MODEL = 1024
N_BRANCH = 4
BRANCH_W = D_MODEL // N_BRANCH
HEAD_DIM = 64
N_HEADS = BRANCH_W // HEAD_DIM
SSM_DSTATE = 128
SSM_GROUPS = 2
SSM_CONV = 4
SSM_CHUNK = 128
SSM_CONV_DIM = BRANCH_W + 2 * SSM_GROUPS * SSM_DSTATE
MOBA_BLOCK = 256
MOBA_TOPK = 3
DSA_TOPK = 256
IDX_HEADS = 8
IDX_DIM = 32
RWKV_W_LORA = 64
RWKV_A_LORA = 64
RWKV_G_LORA = 128
RWKV_IN = 3 * BRANCH_W + RWKV_W_LORA + RWKV_A_LORA + RWKV_G_LORA
RWKV_GN_EPS = 64e-5
D_FF = 11 * D_MODEL // 4
FFN_CONV = 3
NUM_BUCKETS = 32
MAX_DISTANCE = 1024
NORM_EPS = 1e-6
IN_SPLITS = (BRANCH_W, SSM_CONV_DIM, N_HEADS,
             BRANCH_W, BRANCH_W, BRANCH_W,
             BRANCH_W, BRANCH_W, BRANCH_W, IDX_HEADS * IDX_DIM, IDX_DIM, IDX_HEADS,
             RWKV_IN, N_BRANCH * D_MODEL)
F32 = jnp.float32
BF16 = jnp.bfloat16

LANES = 128
SUBLANES = 8
VMEM_LIMIT = 56 * 1024 * 1024
NEG_BIG = -1e30
LOG2E = math.log2(math.e)
PROMPT_Q_SCALE = HEAD_DIM ** -0.5 * LOG2E
Q_TILE = 128
K_TILE = MOBA_BLOCK


def _cparams(*sem):
    return pltpu.CompilerParams(dimension_semantics=sem, vmem_limit_bytes=VMEM_LIMIT)


def _mm_kernel(x_ref, w_ref, o_ref):
    o_ref[...] = jnp.dot(x_ref[...], w_ref[...], preferred_element_type=F32)


def _pick_tile(n, cap, quantum):
    if n <= cap:
        return n
    best = None
    for t in range(quantum, cap + 1, quantum):
        if n % t == 0:
            best = t
    assert best is not None, (n, cap, quantum)
    return best


def matmul(x, w, *, tm_cap=1024, tn_cap=1408):
    m, k = x.shape
    k2, n = w.shape
    assert k == k2
    x = x.astype(BF16)
    w = w.astype(BF16)
    tm = _pick_tile(m, tm_cap, SUBLANES * 2)
    tn = _pick_tile(n, tn_cap, LANES)
    return pl.pallas_call(
        _mm_kernel,
        grid=(n // tn, m // tm),
        in_specs=[pl.BlockSpec((tm, k), lambda j, i: (i, 0)),
                  pl.BlockSpec((k, tn), lambda j, i: (0, j))],
        out_specs=pl.BlockSpec((tm, tn), lambda j, i: (i, j)),
        out_shape=jax.ShapeDtypeStruct((m, n), F32),
        compiler_params=_cparams("parallel", "parallel"),
        name="matmul",
    )(x, w)


def _rel_bucket(dist):
    n = jnp.maximum(dist, 0)
    max_exact = NUM_BUCKETS // 2
    nf = jnp.maximum(n, 1).astype(F32)
    large = max_exact + (jnp.log(nf / max_exact) / math.log(MAX_DISTANCE / max_exact)
                         * (NUM_BUCKETS - max_exact)).astype(jnp.int32)
    large = jnp.minimum(large, NUM_BUCKETS - 1)
    return jnp.where(n < max_exact, n, large)


N_NEAR = -(-(MAX_DISTANCE + K_TILE - 1) // Q_TILE)


def _bias_tiles(tab):
    o = jnp.arange(N_NEAR + 1, dtype=jnp.int32)[:, None, None] * Q_TILE
    j = jnp.arange(K_TILE, dtype=jnp.int32)[None, :, None]
    i = jnp.arange(Q_TILE, dtype=jnp.int32)[None, None, :]
    dist = jnp.where(o >= N_NEAR * Q_TILE, MAX_DISTANCE, o + i - j)
    return _bucket_lookup(tab, _rel_bucket(dist))


def _bucket_lookup(tab, bucket):
    hit = bucket[..., None] == jnp.arange(NUM_BUCKETS, dtype=jnp.int32)
    cols = tab.T.reshape((tab.shape[1],) + (1,) * bucket.ndim + (NUM_BUCKETS,))
    return jnp.sum(jnp.where(hit, cols, 0.0), axis=-1)


FLASH_GROUP = 2


class _FlashScratch:
    shapes = staticmethod(lambda: [
        pltpu.VMEM((FLASH_GROUP, N_HEADS, K_TILE, Q_TILE), F32),
        pltpu.VMEM((FLASH_GROUP, N_HEADS, K_TILE, Q_TILE), BF16),
        pltpu.VMEM((SUBLANES, Q_TILE), F32),
        pltpu.VMEM((SUBLANES, Q_TILE), F32),
        pltpu.VMEM((N_HEADS, HEAD_DIM, Q_TILE), F32)])

    def __init__(self, s, p, m, l, acc):
        self.s, self.p, self.m, self.l, self.acc = s, p, m, l, acc

    def init(self):
        self.m[...] = jnp.full(self.m.shape, NEG_BIG, F32)
        self.l[...] = jnp.zeros(self.l.shape, F32)
        self.acc[...] = jnp.zeros(self.acc.shape, F32)

    def write(self, o_ref):
        for h in range(N_HEADS):
            o_ref[h * HEAD_DIM:(h + 1) * HEAD_DIM, :] = self.acc[h] / self.l[h:h + 1, :]


REDUCE_CHAINS = 8


def _fold_rows(x, op):
    r, c = x.shape
    x = x.reshape(REDUCE_CHAINS, r // (REDUCE_CHAINS * SUBLANES), SUBLANES, c)
    return op(op(x, axis=1), axis=0)


def _flash_group(k_ref, vt_ref, bias_ref, qpair, tiles, fs):
    for u, (k0, _, _, _) in enumerate(tiles):
        for p in range(N_HEADS // 2):
            kt = k_ref[pl.ds(k0, K_TILE), p * LANES:(p + 1) * LANES]
            s2 = jnp.dot(kt, qpair[p], preferred_element_type=F32)
            fs.s[u, 2 * p] = s2[:, :Q_TILE]
            fs.s[u, 2 * p + 1] = s2[:, Q_TILE:]
    for u, (k0, off_idx, mask_tile, mask_rows) in enumerate(tiles):
        for h in range(N_HEADS):
            s = fs.s[u, h] + bias_ref[h, off_idx]
            if mask_tile is not None:
                s = s + mask_tile
            m_old = fs.m[h:h + 1, :]
            m_tile = jnp.max(_fold_rows(s, jnp.max), axis=0, keepdims=True)
            if mask_rows is not None:
                m_tile = m_tile + mask_rows[h]
            m_new = jnp.maximum(m_old, m_tile)
            shift = m_new if mask_rows is None else m_new - mask_rows[h]
            p = jnp.exp2(s - shift)
            alpha = jnp.exp2(m_old - m_new)
            fs.l[h:h + 1, :] = alpha * fs.l[h:h + 1, :] + jnp.sum(_fold_rows(p, jnp.sum), axis=0, keepdims=True)
            fs.m[h:h + 1, :] = m_new
            fs.p[u, h] = p.astype(BF16)
            vt = vt_ref[h * HEAD_DIM:(h + 1) * HEAD_DIM, pl.ds(k0, K_TILE)]
            fs.acc[h] = alpha * fs.acc[h] + jnp.dot(vt, fs.p[u, h], preferred_element_type=F32)


def _query_pairs(q_ref):
    return [jnp.concatenate([q_ref[2 * p].astype(BF16), q_ref[2 * p + 1].astype(BF16)], axis=1)
            for p in range(N_HEADS // 2)]


def _causal_mask(q0, k0):
    row = lax.broadcasted_iota(jnp.int32, (K_TILE, Q_TILE), 0)
    lane = lax.broadcasted_iota(jnp.int32, (K_TILE, Q_TILE), 1)
    return jnp.where(k0 + row <= q0 + lane, 0.0, NEG_BIG).astype(F32)


def _top_blocks_mask(score, n_valid, n_sel, axis):
    n = score.shape[axis]
    idx = lax.broadcasted_iota(jnp.int32, score.shape, axis)
    low = jnp.float32(-3.0e38)
    s = jnp.where(idx < n_valid, score, low)
    mask = jnp.full(score.shape, NEG_BIG, F32)
    for _ in range(n_sel):
        mx = jnp.max(s, axis=axis, keepdims=True)
        first = jnp.min(jnp.where(s == mx, idx, n), axis=axis, keepdims=True)
        pick = idx == jnp.where(mx > low, first, -1)
        mask = jnp.where(pick, 0.0, mask)
        s = jnp.where(pick, low, s)
    return mask


def _moba_prompt_kernel(q_ref, k_ref, vt_ref, kmean_ref, bias_ref, o_ref, sel_scr, *flash_scr):
    qi = pl.program_id(1)
    q0 = qi * Q_TILE
    own = q0 // K_TILE
    nb = kmean_ref.shape[1]
    fs = _FlashScratch(*flash_scr)
    fs.init()
    for h in range(N_HEADS):
        score = jnp.dot(kmean_ref[h], q_ref[h], preferred_element_type=F32,
                        precision=lax.Precision.HIGHEST)
        sel_scr[h] = _top_blocks_mask(score, own, min(MOBA_TOPK, nb), 0)
    qpair = _query_pairs(q_ref)
    k_own = pl.multiple_of(own * K_TILE, K_TILE)
    _flash_group(k_ref, vt_ref, bias_ref, qpair,
                 [(k_own, (q0 - k_own) // Q_TILE, _causal_mask(q0, k_own), None)], fs)

    def body(g, _):
        tiles = []
        for u in range(FLASH_GROUP):
            kj = g * FLASH_GROUP + u
            k0 = pl.multiple_of(kj * K_TILE, K_TILE)
            off = jnp.clip((q0 - k0) // Q_TILE, 0, N_NEAR)
            tiles.append((k0, off, None, [sel_scr[h, pl.ds(kj, 1), :] for h in range(N_HEADS)]))
        _flash_group(k_ref, vt_ref, bias_ref, qpair, tiles, fs)
        return 0

    lax.fori_loop(0, (own + FLASH_GROUP - 1) // FLASH_GROUP, body, 0)
    fs.write(o_ref)


def _pad_heads_t(q):
    b, t, _ = q.shape
    qt = jnp.transpose(q.reshape(b, t, N_HEADS, HEAD_DIM), (0, 2, 3, 1))
    z = jnp.zeros_like(qt)
    even = jnp.concatenate([qt, z], axis=2)
    odd = jnp.concatenate([z, qt], axis=2)
    is_even = (jnp.arange(N_HEADS) % 2 == 0)[None, :, None, None]
    return jnp.where(is_even, even, odd)


def _block_mean_kernel(k_ref, o_ref):
    o_ref[...] = jnp.mean(k_ref[...], axis=0, keepdims=True)


def block_mean(k):
    b, t, c = k.shape
    nb = t // MOBA_BLOCK
    out = pl.pallas_call(
        _block_mean_kernel,
        grid=(b, nb),
        in_specs=[pl.BlockSpec((None, MOBA_BLOCK, c), lambda i, j: (i, j, 0))],
        out_specs=pl.BlockSpec((None, None, 1, c), lambda i, j: (i, j, 0, 0)),
        out_shape=jax.ShapeDtypeStruct((b, nb, 1, c), F32),
        compiler_params=_cparams("parallel", "parallel"),
        name="block_mean",
    )(k)
    return out.reshape(b, nb, c)


def moba_prompt(q, k, v, bias_tiles):
    b, t, w = q.shape
    assert t % K_TILE == 0 and K_TILE % Q_TILE == 0
    nb = t // MOBA_BLOCK
    qp = _pad_heads_t(q * PROMPT_Q_SCALE)
    kmean = block_mean(k)
    kmp = jnp.transpose(_pad_heads_t(kmean), (0, 1, 3, 2))
    kb = k.astype(BF16)
    vtb = jnp.transpose(v, (0, 2, 1)).astype(BF16)
    ot = pl.pallas_call(
        _moba_prompt_kernel,
        grid=(b, t // Q_TILE),
        in_specs=[pl.BlockSpec((None, N_HEADS, LANES, Q_TILE), lambda i, j: (i, 0, 0, j)),
                  pl.BlockSpec((None, t, w), lambda i, j: (i, 0, 0)),
                  pl.BlockSpec((None, w, t), lambda i, j: (i, 0, 0)),
                  pl.BlockSpec((None, N_HEADS, nb, LANES), lambda i, j: (i, 0, 0, 0)),
                  pl.BlockSpec(bias_tiles.shape, lambda i, j: (0, 0, 0, 0))],
        out_specs=pl.BlockSpec((None, w, Q_TILE), lambda i, j: (i, 0, j)),
        out_shape=jax.ShapeDtypeStruct((b, w, t), F32),
        scratch_shapes=[pltpu.VMEM((N_HEADS, nb, Q_TILE), F32)] + _FlashScratch.shapes(),
        compiler_params=_cparams("parallel", "arbitrary"),
        name="moba_prompt",
    )(qp, kb, vtb, kmp, bias_tiles)
    return jnp.transpose(ot, (0, 2, 1))


def _split_bf16(x):
    hi = x.astype(BF16)
    lo = (x - hi.astype(F32)).astype(BF16)
    return hi, lo


def _ordered_key(x):
    x = jnp.where(x == 0.0, 0.0, x)
    bits = lax.bitcast_convert_type(x, jnp.int32)
    return bits ^ ((bits >> 31) & jnp.int32(0x7FFFFFFF))


INT_MIN = -2 ** 31


def _kth_largest_key(count_ge, kcount):
    int_min = jnp.int32(INT_MIN)
    thr = jnp.where(count_ge(jnp.zeros_like(kcount, jnp.int32)) >= kcount, jnp.int32(0), int_min)

    def bit_body(i, thr):
        cand = thr | lax.shift_left(jnp.int32(1), 30 - i)
        return jnp.where(count_ge(cand) >= kcount, cand, thr)

    return lax.fori_loop(0, 31, bit_body, thr)


RADIX_BITS = 4


def _kth_largest_key_radix(count_ge, kcount):
    thr = jnp.full(kcount.shape, INT_MIN, jnp.int32)
    for shift in range(32 - RADIX_BITS, -1, -RADIX_BITS):
        digit = jnp.zeros(kcount.shape, jnp.int32)
        for j in range(1, 2 ** RADIX_BITS):
            cand = thr + jnp.int32(np.array(j << shift, np.uint32).astype(np.int32))
            digit = digit + jnp.where(count_ge(cand) >= kcount, 1, 0)
        thr = thr + lax.shift_left(digit, shift)
    return thr


def _dsa_prompt_kernel(q_ref, k_ref, vt_ref, kic_ref, qic_ref, wi_ref, bias_ref, o_ref,
                       key_scr, mask_scr, *flash_scr, n_keep):
    qi = pl.program_id(1)
    q0 = qi * Q_TILE
    own = q0 // K_TILE
    n_tiles = own + 1
    qpos = q0 + lax.broadcasted_iota(jnp.int32, (1, Q_TILE), 1)
    row = lax.broadcasted_iota(jnp.int32, (K_TILE, Q_TILE), 0)
    fs = _FlashScratch(*flash_scr)
    fs.init()

    def tile_start(kj):
        return pl.multiple_of(kj * K_TILE, K_TILE)

    n_pairs = (n_tiles + 1) // 2

    def score_body(kp, _):
        for u in range(2):
            k0 = tile_start(2 * kp + u)
            kic = kic_ref[pl.ds(k0, K_TILE), :]
            sc = jnp.zeros((K_TILE, Q_TILE), F32)
            for j in range(IDX_HEADS // 2):
                q2 = jnp.concatenate([qic_ref[2 * j], qic_ref[2 * j + 1]], axis=1)
                d = jnp.maximum(jnp.dot(kic, q2, preferred_element_type=F32), 0.0)
                sc = sc + wi_ref[2 * j:2 * j + 1, :] * d[:, :Q_TILE]
                sc = sc + wi_ref[2 * j + 1:2 * j + 2, :] * d[:, Q_TILE:]
            key = jnp.where(k0 + row <= qpos, _ordered_key(sc), jnp.int32(INT_MIN))
            key_scr[pl.ds(k0, K_TILE), :] = key
        return 0

    lax.fori_loop(0, n_pairs, score_body, 0)

    def count(pred):
        def body(kp, c):
            t = key_scr[pl.ds(pl.multiple_of(kp * 2 * K_TILE, 2 * K_TILE), 2 * K_TILE), :]
            return c + _fold_rows(jnp.where(pred(t), 1.0, 0.0), jnp.sum)
        c = lax.fori_loop(0, n_pairs, body, jnp.zeros((SUBLANES, Q_TILE), F32))
        return jnp.sum(c, axis=0, keepdims=True)

    kcount = jnp.minimum(n_keep, qpos + 1).astype(F32)
    thr = _kth_largest_key(lambda c: count(lambda t: t >= c), kcount)
    need = kcount - count(lambda t: t > thr)
    tri = jnp.where(lax.broadcasted_iota(jnp.int32, (K_TILE, K_TILE), 0)
                    >= lax.broadcasted_iota(jnp.int32, (K_TILE, K_TILE), 1), 1.0, 0.0).astype(BF16)

    def mask_body(kp, seen):
        for u in range(2):
            k0 = tile_start(2 * kp + u)
            t = key_scr[pl.ds(k0, K_TILE), :]
            eq = jnp.where(t == thr, 1.0, 0.0)
            rank = jnp.dot(tri, eq.astype(BF16), preferred_element_type=F32) + seen
            keep = jnp.where(t > thr, 1.0, jnp.where(rank <= need, eq, 0.0))
            mask_scr[pl.ds(k0, K_TILE), :] = jnp.where(keep > 0.5, 0.0, NEG_BIG).astype(BF16)
            seen = seen + jnp.sum(_fold_rows(eq, jnp.sum), axis=0, keepdims=True)
        return seen

    lax.fori_loop(0, n_pairs, mask_body, jnp.zeros((1, Q_TILE), F32))
    qpair = _query_pairs(q_ref)

    def body(g, _):
        tiles = []
        for u in range(FLASH_GROUP):
            k0 = tile_start(g * FLASH_GROUP + u)
            off = jnp.clip((q0 - k0) // Q_TILE, 0, N_NEAR)
            tiles.append((k0, off, mask_scr[pl.ds(k0, K_TILE), :].astype(F32), None))
        _flash_group(k_ref, vt_ref, bias_ref, qpair, tiles, fs)
        return 0

    lax.fori_loop(0, n_pairs, body, 0)
    fs.write(o_ref)


def dsa_prompt(q, k, v, qi, ki, wi, bias_tiles):
    b, t, w = q.shape
    assert t % (2 * K_TILE) == 0 and K_TILE % Q_TILE == 0 and FLASH_GROUP == 2
    n_keep = min(DSA_TOPK, t // 4)
    qp = _pad_heads_t(q * PROMPT_Q_SCALE)
    kb = k.astype(BF16)
    vtb = jnp.transpose(v, (0, 2, 1)).astype(BF16)
    ki_hi, ki_lo = _split_bf16(ki)
    kic = jnp.concatenate([ki_hi, ki_hi, ki_lo], axis=-1)
    qis = jnp.transpose((qi * IDX_DIM ** -0.5).reshape(b, t, IDX_HEADS, IDX_DIM), (0, 2, 3, 1))
    qi_hi, qi_lo = _split_bf16(qis)
    qic = jnp.concatenate([qi_hi, qi_lo, qi_hi], axis=2)
    wit = jnp.transpose(wi * IDX_HEADS ** -0.5, (0, 2, 1))
    ot = pl.pallas_call(
        functools.partial(_dsa_prompt_kernel, n_keep=n_keep),
        grid=(b, t // Q_TILE),
        in_specs=[pl.BlockSpec((None, N_HEADS, LANES, Q_TILE), lambda i, j: (i, 0, 0, j)),
                  pl.BlockSpec((None, t, w), lambda i, j: (i, 0, 0)),
                  pl.BlockSpec((None, w, t), lambda i, j: (i, 0, 0)),
                  pl.BlockSpec((None, t, 3 * IDX_DIM), lambda i, j: (i, 0, 0)),
                  pl.BlockSpec((None, IDX_HEADS, 3 * IDX_DIM, Q_TILE), lambda i, j: (i, 0, 0, j)),
                  pl.BlockSpec((None, IDX_HEADS, Q_TILE), lambda i, j: (i, 0, j)),
                  pl.BlockSpec(bias_tiles.shape, lambda i, j: (0, 0, 0, 0))],
        out_specs=pl.BlockSpec((None, w, Q_TILE), lambda i, j: (i, 0, j)),
        out_shape=jax.ShapeDtypeStruct((b, w, t), F32),
        scratch_shapes=[pltpu.VMEM((t, Q_TILE), jnp.int32), pltpu.VMEM((t, Q_TILE), BF16)]
        + _FlashScratch.shapes(),
        compiler_params=_cparams("parallel", "arbitrary"),
        name="dsa_prompt",
    )(qp, kb, vtb, kic, qic, wit, bias_tiles)
    return jnp.transpose(ot, (0, 2, 1))


HEADS_PER_GROUP = N_HEADS // SSM_GROUPS
PAIR_W = HEADS_PER_GROUP * HEAD_DIM
assert PAIR_W == LANES and SSM_DSTATE == LANES and SSM_CHUNK == LANES


def _ssd_prompt_kernel(xdt_ref, b_ref, c_ref, dax_ref, dat_ref, h0_ref, y_ref, hout_ref, st_scr):
    ci = pl.program_id(1)

    @pl.when(ci == 0)
    def _():
        st_scr[...] = h0_ref[...]

    q = SSM_CHUNK
    row = lax.broadcasted_iota(jnp.int32, (q, q), 0)
    col = lax.broadcasted_iota(jnp.int32, (q, q), 1)
    lower = row >= col
    hi = lax.Precision.HIGHEST
    cs_x = jnp.dot(jnp.where(lower, 1.0, 0.0), dax_ref[...], preferred_element_type=F32, precision=hi)
    cs_t = jnp.dot(dat_ref[...], jnp.where(row <= col, 1.0, 0.0), preferred_element_type=F32, precision=hi)
    lane_lo = col < HEAD_DIM
    for g in range(SSM_GROUPS):
        sl = slice(g * PAIR_W, (g + 1) * PAIR_W)
        cg = c_ref[:, g * SSM_DSTATE:(g + 1) * SSM_DSTATE]
        bg = b_ref[:, g * SSM_DSTATE:(g + 1) * SSM_DSTATE]
        xdt = xdt_ref[:, sl]
        cs_pair = cs_x[:, sl]
        cs_swap = pltpu.roll(cs_pair, HEAD_DIM, 1)
        cb = lax.dot_general(cg, bg, (((1,), (1,)), ((), ())), preferred_element_type=F32)
        y_pair = []
        for a in range(HEADS_PER_GROUP):
            h = g * HEADS_PER_GROUP + a
            own_half = lane_lo if a == 0 else jnp.logical_not(lane_lo)
            cs_col = jnp.where(own_half, cs_pair, cs_swap)
            seg = jnp.exp(jnp.minimum(cs_col - cs_t[h:h + 1, :], 0.0))
            scores = cb * jnp.where(lower, seg, 0.0)
            y_pair.append(jnp.dot(scores, xdt, preferred_element_type=F32))
        y_diag = jnp.where(lane_lo, y_pair[0], y_pair[1])
        st = st_scr[sl, :]
        y_off = lax.dot_general(cg, st, (((1,), (1,)), ((), ())), preferred_element_type=F32)
        y_ref[:, sl] = y_diag + y_off * jnp.exp(cs_pair)
        cs_last = cs_pair[q - 1:q, :]
        xdtd = xdt * jnp.exp(cs_last - cs_pair)
        upd = jnp.dot(xdtd.T, bg, preferred_element_type=F32)
        last_a = cs_last[:, 0:1]
        last_b = cs_last[:, HEAD_DIM:HEAD_DIM + 1]
        st_scr[sl, :] = st * jnp.exp(jnp.where(row < HEAD_DIM, last_a, last_b)) + upd

    @pl.when(ci == pl.num_programs(1) - 1)
    def _():
        hout_ref[...] = st_scr[...]


def ssd_prompt(xs, dt, a_neg, bm, cm, h0):
    b, t, w = xs.shape
    assert t % SSM_CHUNK == 0
    rep = lambda z: jnp.repeat(z, HEAD_DIM, axis=-1)
    xdt = xs * rep(dt)
    da = dt * a_neg
    dat = jnp.pad(jnp.transpose(da, (0, 2, 1)), ((0, 0), (0, SUBLANES - N_HEADS), (0, 0)))
    hp = N_HEADS * HEAD_DIM
    chunk = lambda width: pl.BlockSpec((None, SSM_CHUNK, width), lambda i, j: (i, j, 0))
    state = pl.BlockSpec((None, hp, SSM_DSTATE), lambda i, j: (i, 0, 0))
    y, h = pl.pallas_call(
        _ssd_prompt_kernel,
        grid=(b, t // SSM_CHUNK),
        in_specs=[chunk(w), chunk(bm.shape[-1]), chunk(cm.shape[-1]), chunk(w),
                  pl.BlockSpec((None, SUBLANES, SSM_CHUNK), lambda i, j: (i, 0, j)), state],
        out_specs=[chunk(w), state],
        out_shape=[jax.ShapeDtypeStruct((b, t, w), F32), jax.ShapeDtypeStruct((b, hp, SSM_DSTATE), F32)],
        scratch_shapes=[pltpu.VMEM((hp, SSM_DSTATE), F32)],
        compiler_params=_cparams("parallel", "arbitrary"),
        name="ssd_prompt",
    )(xdt, bm, cm, rep(da), dat, h0.reshape(b, hp, SSM_DSTATE))
    return y, h.reshape(b, N_HEADS, HEAD_DIM, SSM_DSTATE)


RWKV_CHUNK = 128
HEAD_PAIRS = N_HEADS // 2


def _split_cat(x):
    hi, lo = _split_bf16(x)
    return jnp.concatenate([hi, lo], axis=1)


def _rwkv_prompt_kernel(r_ref, w_ref, k_ref, kk_ref, kka_ref, v_ref, g_ref, c1_ref, c2_ref, s0_ref,
                        o_ref, sout_ref, s_scr, z_scr, vt_scr, ot_scr):
    ci = pl.program_id(0)
    n_b = r_ref.shape[0]
    tiles = [(b, p) for b in range(n_b) for p in range(HEAD_PAIRS)]
    lanes = lambda p: slice(p * LANES, (p + 1) * LANES)

    r2 = lax.broadcasted_iota(jnp.int32, (2 * LANES, LANES), 0)
    c2 = lax.broadcasted_iota(jnp.int32, (2 * LANES, LANES), 1)
    ones_same = jnp.where((r2 % LANES) // HEAD_DIM == c2 // HEAD_DIM, 1.0, 0.0).astype(BF16)
    r3 = lax.broadcasted_iota(jnp.int32, (LANES, 2 * LANES), 0)
    c3 = lax.broadcasted_iota(jnp.int32, (LANES, 2 * LANES), 1)
    ones_out = jnp.where(r3 // HEAD_DIM == c3 // LANES, 1.0, 0.0).astype(BF16)

    def key_sums(xs):
        out = jnp.dot(jnp.concatenate([_split_cat(x) for x in xs], axis=0), ones_same, preferred_element_type=F32)
        return [out[i * HEAD_DIM:(i + 1) * HEAD_DIM] for i in range(len(xs))]

    @pl.when(ci == 0)
    def _():
        s_scr[...] = s0_ref[...]
        z0 = key_sums([s0_ref[i] * kk_ref[b, 0:1, lanes(p)] for i, (b, p) in enumerate(tiles)])
        for i in range(len(tiles)):
            z_scr[i] = z0[i]

    for i, (b, p) in enumerate(tiles):
        vt_scr[i] = v_ref[b, :, lanes(p)].T
    ot_scr[...] = jnp.zeros(ot_scr.shape, F32)
    lane_t = lax.broadcasted_iota(jnp.int32, (1, RWKV_CHUNK), 1)
    head_a = lax.broadcasted_iota(jnp.int32, (HEAD_DIM, LANES), 1) < HEAD_DIM

    def step(t, row):
        at_t = lane_t == t
        states = [s_scr[i] for i in range(len(tiles))]
        ahead = key_sums([s * row(g_ref, b, p) for s, (b, p) in zip(states, tiles)])
        new_states = []
        for i, (b, p) in enumerate(tiles):
            z = z_scr[i]
            v_a = jnp.sum(jnp.where(at_t, vt_scr[i, :HEAD_DIM, :], 0.0), axis=1, keepdims=True)
            v_b = jnp.sum(jnp.where(at_t, vt_scr[i, HEAD_DIM:, :], 0.0), axis=1, keepdims=True)
            v_bc = jnp.where(head_a, v_a, v_b)
            s = states[i] * row(w_ref, b, p) - z * row(kka_ref, b, p) + v_bc * row(k_ref, b, p)
            s_scr[i] = s
            new_states.append(s)
            z_scr[i] = ahead[i] - z * row(c1_ref, b, p) + v_bc * row(c2_ref, b, p)
        y = jnp.concatenate([(s * row(r_ref, b, p)).astype(BF16) for s, (b, p) in zip(new_states, tiles)], axis=0)
        o_all = jnp.dot(y, ones_out, preferred_element_type=F32)
        for i in range(len(tiles)):
            o = o_all[i * HEAD_DIM:(i + 1) * HEAD_DIM]
            ot_scr[i, :HEAD_DIM, :] = jnp.where(at_t, o[:, :LANES], ot_scr[i, :HEAD_DIM, :])
            ot_scr[i, HEAD_DIM:, :] = jnp.where(at_t, o[:, LANES:], ot_scr[i, HEAD_DIM:, :])

    def step_group(g, _):
        base = pl.multiple_of(g * SUBLANES, SUBLANES)
        refs = (r_ref, w_ref, k_ref, kka_ref, g_ref, c1_ref, c2_ref)
        blocks = {(id(ref), b, p): ref[b, pl.ds(base, SUBLANES), lanes(p)] for ref in refs for b, p in tiles}
        for j in range(SUBLANES):
            step(base + j, lambda ref, b, p, j=j: blocks[(id(ref), b, p)][j:j + 1, :])
        return 0

    lax.fori_loop(0, RWKV_CHUNK // SUBLANES, step_group, 0)
    for i, (b, p) in enumerate(tiles):
        o_ref[b, :, lanes(p)] = ot_scr[i].T

    @pl.when(ci == pl.num_programs(0) - 1)
    def _():
        sout_ref[...] = s_scr[...]


def rwkv_prompt(r, w, k, v, kk, kka, s0):
    b, t, hw = r.shape
    assert t % RWKV_CHUNK == 0 and RWKV_CHUNK == LANES and 2 * HEAD_DIM == LANES
    n_tiles = b * HEAD_PAIRS
    pack = lambda s: jnp.transpose(s.reshape(b, HEAD_PAIRS, 2, HEAD_DIM, HEAD_DIM), (0, 1, 3, 2, 4)).reshape(
        n_tiles, HEAD_DIM, LANES)
    unpack = lambda s: jnp.transpose(s.reshape(b, HEAD_PAIRS, HEAD_DIM, 2, HEAD_DIM), (0, 1, 3, 2, 4)).reshape(
        b, N_HEADS, HEAD_DIM, HEAD_DIM)
    kk_next = jnp.concatenate([kk[:, 1:], jnp.zeros_like(kk[:, :1])], axis=1)
    head_dot = lambda x: jnp.repeat(jnp.sum((x * kk_next).reshape(b, t, N_HEADS, HEAD_DIM), axis=-1), HEAD_DIM, axis=-1)
    seq_spec = pl.BlockSpec((b, RWKV_CHUNK, hw), lambda c: (0, c, 0))
    st_spec = pl.BlockSpec((n_tiles, HEAD_DIM, LANES), lambda c: (0, 0, 0))
    o, s = pl.pallas_call(
        _rwkv_prompt_kernel,
        grid=(t // RWKV_CHUNK,),
        in_specs=[seq_spec] * 9 + [st_spec],
        out_specs=[seq_spec, st_spec],
        out_shape=[jax.ShapeDtypeStruct((b, t, hw), F32), jax.ShapeDtypeStruct((n_tiles, HEAD_DIM, LANES), F32)],
        scratch_shapes=[pltpu.VMEM((n_tiles, HEAD_DIM, LANES), F32),
                        pltpu.VMEM((n_tiles, HEAD_DIM, LANES), F32),
                        pltpu.VMEM((n_tiles, LANES, RWKV_CHUNK), F32),
                        pltpu.VMEM((n_tiles, LANES, RWKV_CHUNK), F32)],
        compiler_params=_cparams("arbitrary"),
        name="rwkv_prompt",
    )(r, w, k, kk, kka, v, w * kk_next, head_dot(kka), head_dot(k), pack(s0))
    return o, unpack(s)


STEP_SEQS = 8
NT_DIMS = (((1,), (1,)), ((), ()))


def _row_dot(row, mat):
    r8 = jnp.broadcast_to(row, (SUBLANES, row.shape[1]))
    return lax.dot_general(r8, mat, NT_DIMS, preferred_element_type=F32,
                           precision=lax.Precision.HIGHEST)[0:1, :]


def _ssd_step_kernel(h0_ref, xdtb_ref, dab_ref, b_ref, c_ref, h_ref, y_ref):
    for s in range(h0_ref.shape[0]):
        for h in range(N_HEADS):
            hn = h0_ref[s, h] * jnp.exp(dab_ref[s, h]) + xdtb_ref[s, h] * b_ref[s, h]
            h_ref[s, h] = hn
            y_ref[s, h] = _row_dot(c_ref[s, h], hn)


def ssd_step(xs, dt, a_neg, bm, cm, h0):
    s = xs.shape[0]
    assert s % STEP_SEQS == 0
    xdtb = jnp.broadcast_to((xs.reshape(s, N_HEADS, HEAD_DIM) * dt[:, :, None])[..., None],
                            (s, N_HEADS, HEAD_DIM, SSM_DSTATE))
    dab = jnp.broadcast_to((dt * a_neg)[:, :, None, None], (s, N_HEADS, 1, SSM_DSTATE))
    per_head = lambda z: jnp.repeat(z.reshape(s, SSM_GROUPS, 1, SSM_DSTATE), HEADS_PER_GROUP, axis=1)
    big = pl.BlockSpec((STEP_SEQS, N_HEADS, HEAD_DIM, SSM_DSTATE), lambda i: (i, 0, 0, 0))
    row = pl.BlockSpec((STEP_SEQS, N_HEADS, 1, SSM_DSTATE), lambda i: (i, 0, 0, 0))
    h, y = pl.pallas_call(
        _ssd_step_kernel,
        grid=(s // STEP_SEQS,),
        in_specs=[big, big, row, row, row],
        out_specs=[big, pl.BlockSpec((STEP_SEQS, N_HEADS, 1, HEAD_DIM), lambda i: (i, 0, 0, 0))],
        out_shape=[jax.ShapeDtypeStruct(h0.shape, F32), jax.ShapeDtypeStruct((s, N_HEADS, 1, HEAD_DIM), F32)],
        compiler_params=_cparams("parallel"),
        name="ssd_step",
    )(h0, xdtb, dab, per_head(bm), per_head(cm))
    return y.reshape(s, N_HEADS * HEAD_DIM), h


def _rwkv_step_kernel(s0_ref, vb_ref, w_ref, kk_ref, kka_ref, k_ref, r_ref, s_ref, o_ref):
    for s in range(s0_ref.shape[0]):
        for h in range(N_HEADS):
            st = s0_ref[s, h]
            sa = -jnp.sum(st * kk_ref[s, h], axis=1, keepdims=True)
            st = st * w_ref[s, h] + sa * kka_ref[s, h] + vb_ref[s, h] * k_ref[s, h]
            s_ref[s, h] = st
            o_ref[s, h] = _row_dot(r_ref[s, h], st)


def rwkv_step(r, w, k, v, kk, kka, s0):
    s = r.shape[0]
    assert s % STEP_SEQS == 0
    rows = lambda z: z.reshape(s, N_HEADS, 1, HEAD_DIM)
    vb = jnp.broadcast_to(v.reshape(s, N_HEADS, HEAD_DIM, 1), (s, N_HEADS, HEAD_DIM, HEAD_DIM))
    big = pl.BlockSpec((STEP_SEQS, N_HEADS, HEAD_DIM, HEAD_DIM), lambda i: (i, 0, 0, 0))
    row = pl.BlockSpec((STEP_SEQS, N_HEADS, 1, HEAD_DIM), lambda i: (i, 0, 0, 0))
    st, o = pl.pallas_call(
        _rwkv_step_kernel,
        grid=(s // STEP_SEQS,),
        in_specs=[big, big, row, row, row, row, row],
        out_specs=[big, row],
        out_shape=[jax.ShapeDtypeStruct(s0.shape, F32), jax.ShapeDtypeStruct((s, N_HEADS, 1, HEAD_DIM), F32)],
        compiler_params=_cparams("parallel"),
        name="rwkv_step",
    )(s0, vb, rows(w), rows(kk), rows(kka), rows(k), rows(r))
    return o.reshape(s, N_HEADS * HEAD_DIM), st


def _head_rows_to_row(o):
    r = lax.broadcasted_iota(jnp.int32, o.shape, 0)
    c = lax.broadcasted_iota(jnp.int32, o.shape, 1)
    return jnp.sum(jnp.where(c // HEAD_DIM == r, o, 0.0), axis=0, keepdims=True)


def _sample_softmax_pv(logit_pages, s_new, vt_pages, vnew):
    m = s_new
    for s in logit_pages:
        m = jnp.maximum(m, jnp.max(s, axis=1, keepdims=True))
    p_new = jnp.exp(s_new - m)
    l = p_new
    acc = p_new * vnew
    for s, vt_ref in zip(logit_pages, vt_pages):
        p = jnp.exp(s - m)
        l = l + jnp.sum(p, axis=1, keepdims=True)
        acc = acc + lax.dot_general(p.astype(BF16), vt_ref[...].astype(BF16), NT_DIMS, preferred_element_type=F32)
    return _head_rows_to_row(acc / l)


def _moba_sample_kernel(pt_ref, qbd_ref, knew_ref, vnew_ref, bias_ref, biasn_ref, *rest, n_pages, page):
    del pt_ref
    kt_pages, vt_pages, o_ref = rest[:n_pages], rest[n_pages:2 * n_pages], rest[2 * n_pages]
    qbd = qbd_ref[...]
    q_hi, q_lo = _split_bf16(qbd)
    q_both = jnp.concatenate([q_hi, q_lo], axis=0)
    ppb = MOBA_BLOCK // page
    n_blocks = n_pages // ppb
    lane = lax.broadcasted_iota(jnp.int32, (SUBLANES, LANES), 1)
    raw = []
    score = jnp.zeros((SUBLANES, LANES), F32)
    for p in range(n_pages):
        k_hi, k_lo = _split_bf16(kt_pages[p][...])
        r = jnp.dot(q_both, k_hi, preferred_element_type=F32)
        r = r[:SUBLANES] + r[SUBLANES:] + jnp.dot(q_hi, k_lo, preferred_element_type=F32)
        raw.append(r)
        score = score + jnp.where(lane == p // ppb, jnp.sum(r, axis=1, keepdims=True) * (1.0 / MOBA_BLOCK), 0.0)
    sel = _top_blocks_mask(score, n_blocks, min(MOBA_TOPK, n_blocks + 1), 1)
    logits = [raw[p] + bias_ref[:, p * page:(p + 1) * page] + sel[:, p // ppb:p // ppb + 1]
              for p in range(n_pages)]
    s_new = jnp.sum(qbd * knew_ref[...], axis=1, keepdims=True) + biasn_ref[:, 0:1]
    o_ref[...] = _sample_softmax_pv(logits, s_new, vt_pages, vnew_ref[...])


def _block_diag_q(q):
    s = q.shape[0]
    eye = jnp.eye(N_HEADS, dtype=q.dtype)
    bd = (q.reshape(s, 1, N_HEADS, HEAD_DIM) * eye[None, :, :, None]).reshape(s, N_HEADS, N_HEADS * HEAD_DIM)
    return jnp.pad(bd, ((0, 0), (0, SUBLANES - N_HEADS), (0, 0)))


def _sample_bias(tab, past_len):
    dist = past_len - jnp.arange(past_len, dtype=jnp.int32)
    pad = ((0, SUBLANES - N_HEADS), (0, 0))
    past = jnp.pad(_bucket_lookup(tab, _rel_bucket(dist)), pad)
    new = jnp.pad(_bucket_lookup(tab, _rel_bucket(jnp.zeros((LANES,), jnp.int32))), pad)
    return past, new


def _page_specs(layer, n_pages, rows, page):
    return [pl.BlockSpec((None, None, rows, page), functools.partial(
        lambda s, pt, p: (layer, pt[s, p], 0, 0), p=p)) for p in range(n_pages)]


def _transposed_pages(cache):
    c = cache.reshape(cache.shape[:3] + (-1,))
    return jnp.transpose(c, (0, 1, 3, 2))


def moba_sample(q, knew, vnew, cache_kt, cache_vt, layer, page_table, tab):
    s, w = q.shape
    n_pages, page = page_table.shape[1], cache_kt.shape[3]
    past_len = n_pages * page
    assert past_len % MOBA_BLOCK == 0 and MOBA_BLOCK % page == 0
    bias_p, bias_n = _sample_bias(tab, past_len)
    row = pl.BlockSpec((None, 1, w), lambda i, pt: (i, 0, 0))
    const = lambda shape: pl.BlockSpec(shape, lambda i, pt: (0, 0))
    out = pl.pallas_call(
        functools.partial(_moba_sample_kernel, n_pages=n_pages, page=page),
        grid_spec=pltpu.PrefetchScalarGridSpec(
            num_scalar_prefetch=1, grid=(s,),
            in_specs=[pl.BlockSpec((None, SUBLANES, w), lambda i, pt: (i, 0, 0)), row, row,
                      const(bias_p.shape), const(bias_n.shape)]
            + _page_specs(layer, n_pages, w, page) * 2,
            out_specs=row),
        out_shape=jax.ShapeDtypeStruct((s, 1, w), F32),
        compiler_params=_cparams("arbitrary"),
        name="moba_sample",
    )(page_table, _block_diag_q(q * HEAD_DIM ** -0.5), knew[:, None], vnew[:, None], bias_p, bias_n,
      *([cache_kt] * n_pages), *([cache_vt] * n_pages))
    return out.reshape(s, w)


def _dsa_sample_kernel(pt_ref, qbd_ref, knew_ref, vnew_ref, bias_ref, biasn_ref, qi_ref, wi_ref, kin_ref,
                       *rest, n_pages, page, n_keep):
    del pt_ref
    kt_pages, vt_pages = rest[:n_pages], rest[n_pages:2 * n_pages]
    kit_pages, o_ref = rest[2 * n_pages:3 * n_pages], rest[3 * n_pages]
    qi = qi_ref[...]
    wi = wi_ref[:, 0:1]
    hi = lax.Precision.HIGHEST
    keys = []
    for p in range(n_pages):
        d = jnp.dot(qi, kit_pages[p][...], preferred_element_type=F32, precision=hi)
        keys.append(_ordered_key(jnp.sum(wi * jnp.maximum(d, 0.0), axis=0, keepdims=True)))
    d_new = jnp.sum(qi * kin_ref[...], axis=1, keepdims=True)
    key_new = _ordered_key(jnp.sum(wi * jnp.maximum(d_new, 0.0), axis=0, keepdims=True))

    key_all = jnp.concatenate(keys, axis=0)

    def total(x):
        return jnp.sum(jnp.sum(x, axis=0, keepdims=True), axis=1, keepdims=True)

    def count(pred):
        return jnp.where(pred(key_new), 1.0, 0.0) + total(jnp.where(pred(key_all), 1.0, 0.0))

    kcount = jnp.full((1, 1), float(n_keep), F32)
    thr = _kth_largest_key_radix(lambda c: count(lambda t: t >= c), kcount)
    need = kcount - count(lambda t: t > thr)
    eq = jnp.where(key_all == thr, 1.0, 0.0)
    upper = jnp.where(lax.broadcasted_iota(jnp.int32, (page, page), 0)
                      <= lax.broadcasted_iota(jnp.int32, (page, page), 1), 1.0, 0.0).astype(BF16)
    before = jnp.where(lax.broadcasted_iota(jnp.int32, (n_pages, n_pages), 1)
                       < lax.broadcasted_iota(jnp.int32, (n_pages, n_pages), 0), 1.0, 0.0).astype(BF16)
    per_page = jnp.broadcast_to(jnp.sum(eq, axis=1, keepdims=True), eq.shape).astype(BF16)
    rank = (jnp.dot(eq.astype(BF16), upper, preferred_element_type=F32)
            + jnp.dot(before, per_page, preferred_element_type=F32))
    keep = jnp.where(key_all > thr, 1.0, jnp.where(rank <= need, eq, 0.0))
    mask_all = jnp.where(keep > 0.5, 0.0, NEG_BIG)
    masks = [mask_all[p:p + 1, :] for p in range(n_pages)]
    keep_new = jnp.where(key_new > thr, 1.0,
                         jnp.where(key_new == thr, jnp.where(total(eq) + 1.0 <= need, 1.0, 0.0), 0.0))
    mask_new = jnp.where(keep_new > 0.5, 0.0, NEG_BIG)
    qbd = qbd_ref[...]
    qb = qbd.astype(BF16)
    logits = [jnp.dot(qb, kt_pages[p][...].astype(BF16), preferred_element_type=F32)
              + bias_ref[:, p * page:(p + 1) * page] + masks[p] for p in range(n_pages)]
    s_new = jnp.sum(qbd * knew_ref[...], axis=1, keepdims=True) + biasn_ref[:, 0:1] + mask_new
    o_ref[...] = _sample_softmax_pv(logits, s_new, vt_pages, vnew_ref[...])


def dsa_sample(q, knew, vnew, qi, wi, kinew, cache_kt, cache_vt, cache_kit, layer, page_table, tab):
    s, w = q.shape
    n_pages, page = page_table.shape[1], cache_kt.shape[3]
    past_len = n_pages * page
    n_keep = min(DSA_TOPK, (past_len + 1) // 4)
    bias_p, bias_n = _sample_bias(tab, past_len)
    row = pl.BlockSpec((None, 1, w), lambda i, pt: (i, 0, 0))
    const = lambda shape: pl.BlockSpec(shape, lambda i, pt: (0, 0))
    qis = (qi * IDX_DIM ** -0.5).reshape(s, IDX_HEADS, IDX_DIM)
    wib = jnp.broadcast_to((wi * IDX_HEADS ** -0.5)[:, :, None], (s, IDX_HEADS, LANES))
    out = pl.pallas_call(
        functools.partial(_dsa_sample_kernel, n_pages=n_pages, page=page, n_keep=n_keep),
        grid_spec=pltpu.PrefetchScalarGridSpec(
            num_scalar_prefetch=1, grid=(s,),
            in_specs=[pl.BlockSpec((None, SUBLANES, w), lambda i, pt: (i, 0, 0)), row, row,
                      const(bias_p.shape), const(bias_n.shape),
                      pl.BlockSpec((None, IDX_HEADS, IDX_DIM), lambda i, pt: (i, 0, 0)),
                      pl.BlockSpec((None, IDX_HEADS, LANES), lambda i, pt: (i, 0, 0)),
                      pl.BlockSpec((None, 1, IDX_DIM), lambda i, pt: (i, 0, 0))]
            + _page_specs(layer, n_pages, w, page) * 2 + _page_specs(layer, n_pages, IDX_DIM, page),
            out_specs=row),
        out_shape=jax.ShapeDtypeStruct((s, 1, w), F32),
        compiler_params=_cparams("arbitrary"),
        name="dsa_sample",
    )(page_table, _block_diag_q(q * HEAD_DIM ** -0.5), knew[:, None], vnew[:, None], bias_p, bias_n,
      qis, wib, kinew[:, None], *([cache_kt] * n_pages), *([cache_vt] * n_pages), *([cache_kit] * n_pages))
    return out.reshape(s, w)


N_IN = sum(IN_SPLITS)
N_IN_PAD = -(-N_IN // LANES) * LANES


def _rms_norm(x, g):
    return x * lax.rsqrt(jnp.mean(x * x, axis=-1, keepdims=True) + NORM_EPS) * g


def _split_cols(a, sizes):
    return jnp.split(a, [int(i) for i in np.cumsum(sizes)[:-1]], axis=-1)


def _causal_dwconv(x, buf, w, b):
    width = w.shape[0]
    length = x.shape[1]
    xp = jnp.concatenate([buf, x], axis=1)
    y = b + sum(xp[:, j:j + length] * w[j] for j in range(width))
    return y, xp[:, length:]


def _tokens_matmul(x, w):
    b, t, k = x.shape
    return matmul(x.reshape(b * t, k), w).reshape(b, t, w.shape[1])


def _mamba(pz, pxbc, pdt, conv_buf, h0, conv_w, conv_b, dt_bias, a_log, d_skip, norm_g):
    b, t, _ = pz.shape
    xbc, new_buf = _causal_dwconv(pxbc, conv_buf, conv_w, conv_b)
    xbc = jax.nn.silu(xbc)
    xs, bm, cm = jnp.split(xbc, [BRANCH_W, BRANCH_W + SSM_GROUPS * SSM_DSTATE], axis=-1)
    dt = jax.nn.softplus(pdt + dt_bias)
    a_neg = -jnp.exp(a_log)
    if t == 1:
        y, h_new = ssd_step(xs[:, 0], dt[:, 0], a_neg, bm[:, 0], cm[:, 0], h0)
        y = y[:, None]
    else:
        y, h_new = ssd_prompt(xs, dt, a_neg, bm, cm, h0)
    y = y + jnp.repeat(d_skip, HEAD_DIM) * xs
    return _rms_norm(y * jax.nn.silu(pz), norm_g), new_buf, h_new


def _rwkv(p, shift_prev, s0, mu, w0, w2, a0, a2, g2, k_k, k_a, r_k, ln_w, ln_b):
    b, t, _ = p.shape
    p_prev = jnp.concatenate([shift_prev[:, None], p[:, :-1]], axis=1)
    pm = p + (p_prev - p) * mu
    r, k, v, xw, xa, xg = _split_cols(pm, (BRANCH_W, BRANCH_W, BRANCH_W, RWKV_W_LORA, RWKV_A_LORA, RWKV_G_LORA))
    logw = -jax.nn.softplus(-(w0 + _tokens_matmul(jnp.tanh(xw), w2))) - 0.5
    decay = jnp.exp(-jnp.exp(logw))
    a = jax.nn.sigmoid(a0 + _tokens_matmul(xa, a2))
    g = _tokens_matmul(jax.nn.sigmoid(xg), g2)
    heads = lambda z: z.reshape(b, t, N_HEADS, HEAD_DIM)
    kk = heads(k * k_k)
    kk = (kk * lax.rsqrt(jnp.maximum(jnp.sum(kk * kk, axis=-1, keepdims=True), 1e-24))).reshape(b, t, BRANCH_W)
    k = k * (1.0 + (a - 1.0) * k_a)
    if t == 1:
        o, s_new = rwkv_step(r[:, 0], decay[:, 0], k[:, 0], v[:, 0], kk[:, 0], (kk * a)[:, 0], s0)
        o = o[:, None]
    else:
        o, s_new = rwkv_prompt(r, decay, k, v, kk, kk * a, s0)
    o = heads(o)
    mean = jnp.mean(o, axis=-1, keepdims=True)
    var = jnp.mean(jnp.square(o - mean), axis=-1, keepdims=True)
    o = ((o - mean) * lax.rsqrt(var + RWKV_GN_EPS)).reshape(b, t, BRANCH_W) * ln_w + ln_b
    bonus = jnp.sum(heads(r) * heads(k) * r_k, axis=-1, keepdims=True) * heads(v)
    return (o + bonus.reshape(b, t, BRANCH_W)) * g, p[:, -1], s_new


PROJ_ORDER = (13, 12, 0, 1, 3, 4, 5, 6, 7, 8, 9, 2, 10, 11)
MERGE_ROWS = 256


def _merge_kernel(oa_ref, ob_ref, oc_ref, od_ref, g_ref, wb_ref, wo_ref, x_ref, out_ref):
    mix = None
    for n, o_ref in enumerate((oa_ref, ob_ref, oc_ref, od_ref)):
        pr = jnp.dot(o_ref[...].astype(BF16), wb_ref[n], preferred_element_type=F32)
        term = jax.nn.sigmoid(g_ref[:, n * D_MODEL:(n + 1) * D_MODEL]) * pr
        mix = term if mix is None else mix + term
    out_ref[...] = x_ref[...] + jnp.dot(mix.astype(BF16), wo_ref[...], preferred_element_type=F32)


def merge_branches(x, branches, proj, w_branch, w_out):
    m = x.shape[0]
    tm = _pick_tile(m, MERGE_ROWS, SUBLANES)
    tok = lambda width: pl.BlockSpec((tm, width), lambda i: (i, 0))
    return pl.pallas_call(
        _merge_kernel,
        grid=(m // tm,),
        in_specs=[tok(BRANCH_W)] * N_BRANCH + [tok(N_BRANCH * D_MODEL),
                  pl.BlockSpec((N_BRANCH, BRANCH_W, D_MODEL), lambda i: (0, 0, 0)),
                  pl.BlockSpec((D_MODEL, D_MODEL), lambda i: (0, 0)), tok(D_MODEL)],
        out_specs=tok(D_MODEL),
        out_shape=jax.ShapeDtypeStruct((m, D_MODEL), F32),
        compiler_params=_cparams("parallel"),
        name="merge_branches",
    )(*branches, proj, w_branch.astype(BF16), w_out.astype(BF16), x)


FFN_ROWS = 256


def _ffn_down_kernel(a_ref, bv_ref, halo_ref, cw_ref, cb_ref, wd_ref, x_ref, out_ref):
    a = a_ref[...]
    tm = a.shape[0]
    ext = jnp.concatenate([halo_ref[...], a], axis=0)
    y = cb_ref[...] + cw_ref[FFN_CONV - 1:FFN_CONV, :] * a
    for back in range(1, FFN_CONV):
        y = y + cw_ref[FFN_CONV - 1 - back:FFN_CONV - back, :] * ext[SUBLANES - back:SUBLANES - back + tm, :]
    g = (y * jax.nn.sigmoid(y)) * bv_ref[...]
    out_ref[...] = x_ref[...] + jnp.dot(g.astype(BF16), wd_ref[...], preferred_element_type=F32)


def ffn_down(x, up, buf, conv_w, conv_b, w_down):
    b, t, d = x.shape
    tm = _pick_tile(t, FFN_ROWS, SUBLANES)
    nt = t // tm
    keep = FFN_CONV - 1
    a_tail = up[..., :D_FF].reshape(b, nt, tm, D_FF)[:, :, tm - keep:, :]
    before = jnp.concatenate([buf[:, None], a_tail[:, :-1]], axis=1)
    halo = jnp.pad(before, ((0, 0), (0, 0), (SUBLANES - keep, 0), (0, 0))).reshape(b * nt, SUBLANES, D_FF)
    m = b * t
    out = pl.pallas_call(
        _ffn_down_kernel,
        grid=(m // tm,),
        in_specs=[pl.BlockSpec((tm, D_FF), lambda i: (i, 0)), pl.BlockSpec((tm, D_FF), lambda i: (i, 1)),
                  pl.BlockSpec((None, SUBLANES, D_FF), lambda i: (i, 0, 0)),
                  pl.BlockSpec((FFN_CONV, D_FF), lambda i: (0, 0)), pl.BlockSpec((1, D_FF), lambda i: (0, 0)),
                  pl.BlockSpec((D_FF, d), lambda i: (0, 0)), pl.BlockSpec((tm, d), lambda i: (i, 0))],
        out_specs=pl.BlockSpec((tm, d), lambda i: (i, 0)),
        out_shape=jax.ShapeDtypeStruct((m, d), F32),
        compiler_params=_cparams("parallel"),
        name="ffn_down",
    )(up.reshape(m, 2 * D_FF), up.reshape(m, 2 * D_FF), halo, conv_w, conv_b[None], w_down.astype(BF16),
      x.reshape(m, d))
    return out.reshape(b, t, d), up[:, t - keep:, :D_FF]


def _layer(x, l, past, prm, attn):
    ssm_h, ssm_buf, rw_s, rw_shift, ffn_buf = past
    b, t, _ = x.shape
    h = _rms_norm(x, prm["norm1_g"][l])
    w_cols = _split_cols(prm["w_in"][l], IN_SPLITS)
    w_in = jnp.concatenate([w_cols[i] for i in PROJ_ORDER] + [jnp.zeros((D_MODEL, N_IN_PAD - N_IN), F32)], axis=1)
    proj = _tokens_matmul(h, w_in)
    parts = dict(zip(PROJ_ORDER, _split_cols(proj[..., :N_IN], [IN_SPLITS[i] for i in PROJ_ORDER])))
    (s_z, s_xbc, s_dt, m_q, m_k, m_v, d_q, d_k, d_v, d_qi, d_ki, d_wi, p_rwkv, _) = (parts[i] for i in range(14))
    o_a, ssm_buf_new, ssm_h_new = _mamba(s_z, s_xbc, s_dt, ssm_buf, ssm_h, prm["ssm_conv_w"][l], prm["ssm_conv_b"][l],
                                         prm["ssm_dt_bias"][l], prm["ssm_A_log"][l], prm["ssm_D"][l],
                                         prm["ssm_norm_g"][l])
    o_b = attn("moba", l, m_q, m_k, m_v)
    o_c = attn("dsa", l, d_q, d_k, d_v, d_qi, d_ki, d_wi)
    o_d, rw_shift_new, rw_s_new = _rwkv(p_rwkv, rw_shift, rw_s, prm["rwkv_mu"][l], prm["rwkv_w0"][l], prm["rwkv_w2"][l],
                                        prm["rwkv_a0"][l], prm["rwkv_a2"][l], prm["rwkv_g2"][l], prm["rwkv_k_k"][l],
                                        prm["rwkv_k_a"][l], prm["rwkv_r_k"][l], prm["rwkv_ln_w"][l], prm["rwkv_ln_b"][l])
    flat = lambda z: z.reshape(b * t, z.shape[-1])
    x = merge_branches(flat(x), [flat(o) for o in (o_a, o_b, o_c, o_d)], flat(proj), prm["w_branch"][l],
                       prm["w_out"][l]).reshape(b, t, D_MODEL)
    up = _tokens_matmul(_rms_norm(x, prm["norm2_g"][l]), prm["ffn_w_up"][l])
    if t >= FFN_CONV:
        x, ffn_buf_new = ffn_down(x, up, ffn_buf, prm["ffn_conv_w"][l], prm["ffn_conv_b"][l], prm["ffn_w_down"][l])
    else:
        a, bv = jnp.split(up, 2, axis=-1)
        a, ffn_buf_new = _causal_dwconv(a, ffn_buf, prm["ffn_conv_w"][l], prm["ffn_conv_b"][l])
        x = x + _tokens_matmul(jax.nn.silu(a) * bv, prm["ffn_w_down"][l])
    heads = lambda z: z.reshape(b, t, N_HEADS, HEAD_DIM)
    new = (heads(m_k), heads(m_v), heads(d_k), heads(d_v), d_ki, ssm_h_new, ssm_buf_new, rw_s_new, rw_shift_new,
           ffn_buf_new)
    return x, new


def kernel(x_prompt, x_sample, cache_moba_k, cache_moba_v, cache_dsa_k, cache_dsa_v, cache_dsa_kidx, page_table,
           state_ssm, state_ssm_conv, state_rwkv, state_rwkv_shift, state_ffn_conv,
           norm1_g, w_in, ssm_conv_w, ssm_conv_b, ssm_dt_bias, ssm_A_log, ssm_D, ssm_norm_g,
           rwkv_mu, rwkv_w0, rwkv_w2, rwkv_a0, rwkv_a2, rwkv_g2, rwkv_k_k, rwkv_k_a, rwkv_r_k, rwkv_ln_w, rwkv_ln_b,
           w_branch, w_out, norm2_g, ffn_w_up, ffn_conv_w, ffn_conv_b, ffn_w_down, rel_bias, final_g):
    prm = dict(norm1_g=norm1_g, w_in=w_in, ssm_conv_w=ssm_conv_w, ssm_conv_b=ssm_conv_b, ssm_dt_bias=ssm_dt_bias,
               ssm_A_log=ssm_A_log, ssm_D=ssm_D, ssm_norm_g=ssm_norm_g, rwkv_mu=rwkv_mu, rwkv_w0=rwkv_w0,
               rwkv_w2=rwkv_w2, rwkv_a0=rwkv_a0, rwkv_a2=rwkv_a2, rwkv_g2=rwkv_g2, rwkv_k_k=rwkv_k_k,
               rwkv_k_a=rwkv_k_a, rwkv_r_k=rwkv_r_k, rwkv_ln_w=rwkv_ln_w, rwkv_ln_b=rwkv_ln_b, w_branch=w_branch,
               w_out=w_out, norm2_g=norm2_g, ffn_w_up=ffn_w_up, ffn_conv_w=ffn_conv_w, ffn_conv_b=ffn_conv_b,
               ffn_w_down=ffn_w_down)
    depth = w_in.shape[0]
    bp = x_prompt.shape[0]
    ns = x_sample.shape[0]
    assert x_sample.shape[1] == 1
    tab_moba, tab_dsa = rel_bias[:, :N_HEADS], rel_bias[:, N_HEADS:]
    tiles_moba, tiles_dsa = _bias_tiles(tab_moba * LOG2E), _bias_tiles(tab_dsa * LOG2E)
    ck_m, cv_m, ck_d, cv_d, cki_d = (_transposed_pages(c) for c in (cache_moba_k, cache_moba_v, cache_dsa_k,
                                                                    cache_dsa_v, cache_dsa_kidx))

    def attn_prompt(kind, l, q, k, v, qi=None, ki=None, wi=None):
        if kind == "moba":
            return moba_prompt(q, k, v, tiles_moba)
        return dsa_prompt(q, k, v, qi, ki, wi, tiles_dsa)

    def attn_sample(kind, l, q, k, v, qi=None, ki=None, wi=None):
        if kind == "moba":
            return moba_sample(q[:, 0], k[:, 0], v[:, 0], ck_m, cv_m, l, page_table, tab_moba)[:, None]
        return dsa_sample(q[:, 0], k[:, 0], v[:, 0], qi[:, 0], wi[:, 0], ki[:, 0], ck_d, cv_d, cki_d,
                          l, page_table, tab_dsa)[:, None]

    zp = lambda *shape: jnp.zeros((bp,) + shape, F32)
    prompt_past = (zp(N_HEADS, HEAD_DIM, SSM_DSTATE), zp(SSM_CONV - 1, SSM_CONV_DIM),
                   zp(N_HEADS, HEAD_DIM, HEAD_DIM), zp(RWKV_IN), zp(FFN_CONV - 1, D_FF))
    xp, xs = x_prompt, x_sample
    new_p, new_s = [], []
    for l in range(depth):
        sample_past = (state_ssm[l], state_ssm_conv[l], state_rwkv[l], state_rwkv_shift[l], state_ffn_conv[l])
        xp, np_l = _layer(xp, l, prompt_past, prm, attn_prompt)
        xs, ns_l = _layer(xs, l, sample_past, prm, attn_sample)
        new_p.append(np_l)
        new_s.append(ns_l)
    outs = [_rms_norm(xp, final_g), _rms_norm(xs, final_g)]
    for i in range(10):
        outs.append(jnp.stack([o[i] for o in new_p], axis=0))
        outs.append(jnp.stack([o[i] for o in new_s], axis=0))
    return tuple(outs)
```

```python
import functools
import math

import jax
import jax.numpy as jnp
import numpy as np
from jax import lax
from jax.experimental import pallas as pl
from jax.experimental.pallas import tpu as pltpu

D_MODEL = 1024
N_BRANCH = 4
BRANCH_W = D_MODEL // N_BRANCH
HEAD_DIM = 64
N_HEADS = BRANCH_W // HEAD_DIM
SSM_DSTATE = 128
SSM_GROUPS = 2
SSM_CONV = 4
SSM_CHUNK = 128
SSM_CONV_DIM = BRANCH_W + 2 * SSM_GROUPS * SSM_DSTATE
MOBA_BLOCK = 256
MOBA_TOPK = 3
DSA_TOPK = 256
IDX_HEADS = 8
IDX_DIM = 32
RWKV_W_LORA = 64
RWKV_A_LORA = 64
RWKV_G_LORA = 128
RWKV_IN = 3 * BRANCH_W + RWKV_W_LORA + RWKV_A_LORA + RWKV_G_LORA
RWKV_GN_EPS = 64e-5
D_FF = 11 * D_MODEL // 4
FFN_CONV = 3
NUM_BUCKETS = 32
MAX_DISTANCE = 1024
NORM_EPS = 1e-6
IN_SPLITS = (BRANCH_W, SSM_CONV_DIM, N_HEADS,
             BRANCH_W, BRANCH_W, BRANCH_W,
             BRANCH_W, BRANCH_W, BRANCH_W, IDX_HEADS * IDX_DIM, IDX_DIM, IDX_HEADS,
             RWKV_IN, N_BRANCH * D_MODEL)
F32 = jnp.float32
BF16 = jnp.bfloat16

LANES = 128
SUBLANES = 8
VMEM_LIMIT = 56 * 1024 * 1024
NEG_BIG = -1e30
LOG2E = math.log2(math.e)
PROMPT_Q_SCALE = HEAD_DIM ** -0.5 * LOG2E
Q_TILE = 128
K_TILE = MOBA_BLOCK


def _cparams(*sem):
    return pltpu.CompilerParams(dimension_semantics=sem, vmem_limit_bytes=VMEM_LIMIT)


def _mm_kernel(x_ref, w_ref, o_ref):
    o_ref[...] = jnp.dot(x_ref[...], w_ref[...], preferred_element_type=F32)


def _pick_tile(n, cap, quantum):
    if n <= cap:
        return n
    best = None
    for t in range(quantum, cap + 1, quantum):
        if n % t == 0:
            best = t
    assert best is not None, (n, cap, quantum)
    return best


def matmul(x, w, *, tm_cap=1024, tn_cap=1408):
    m, k = x.shape
    k2, n = w.shape
    assert k == k2
    x = x.astype(BF16)
    w = w.astype(BF16)
    tm = _pick_tile(m, tm_cap, SUBLANES * 2)
    tn = _pick_tile(n, tn_cap, LANES)
    return pl.pallas_call(
        _mm_kernel,
        grid=(n // tn, m // tm),
        in_specs=[pl.BlockSpec((tm, k), lambda j, i: (i, 0)),
                  pl.BlockSpec((k, tn), lambda j, i: (0, j))],
        out_specs=pl.BlockSpec((tm, tn), lambda j, i: (i, j)),
        out_shape=jax.ShapeDtypeStruct((m, n), F32),
        compiler_params=_cparams("parallel", "parallel"),
        name="matmul",
    )(x, w)


def _rel_bucket(dist):
    n = jnp.maximum(dist, 0)
    max_exact = NUM_BUCKETS // 2
    nf = jnp.maximum(n, 1).astype(F32)
    large = max_exact + (jnp.log(nf / max_exact) / math.log(MAX_DISTANCE / max_exact)
                         * (NUM_BUCKETS - max_exact)).astype(jnp.int32)
    large = jnp.minimum(large, NUM_BUCKETS - 1)
    return jnp.where(n < max_exact, n, large)


N_NEAR = -(-(MAX_DISTANCE + K_TILE - 1) // Q_TILE)


def _bias_tiles(tab):
    o = jnp.arange(N_NEAR + 1, dtype=jnp.int32)[:, None, None] * Q_TILE
    j = jnp.arange(K_TILE, dtype=jnp.int32)[None, :, None]
    i = jnp.arange(Q_TILE, dtype=jnp.int32)[None, None, :]
    dist = jnp.where(o >= N_NEAR * Q_TILE, MAX_DISTANCE, o + i - j)
    return _bucket_lookup(tab, _rel_bucket(dist))


def _bucket_lookup(tab, bucket):
    hit = bucket[..., None] == jnp.arange(NUM_BUCKETS, dtype=jnp.int32)
    cols = tab.T.reshape((tab.shape[1],) + (1,) * bucket.ndim + (NUM_BUCKETS,))
    return jnp.sum(jnp.where(hit, cols, 0.0), axis=-1)


FLASH_GROUP = 4


class _FlashScratch:
    shapes = staticmethod(lambda: [
        pltpu.VMEM((FLASH_GROUP, N_HEADS, K_TILE, Q_TILE), F32),
        pltpu.VMEM((FLASH_GROUP, N_HEADS, K_TILE, Q_TILE), BF16),
        pltpu.VMEM((SUBLANES, Q_TILE), F32),
        pltpu.VMEM((SUBLANES, Q_TILE), F32),
        pltpu.VMEM((N_HEADS, HEAD_DIM, Q_TILE), F32)])

    def __init__(self, s, p, m, l, acc):
        self.s, self.p, self.m, self.l, self.acc = s, p, m, l, acc

    def init(self):
        self.m[...] = jnp.full(self.m.shape, NEG_BIG, F32)
        self.l[...] = jnp.zeros(self.l.shape, F32)
        self.acc[...] = jnp.zeros(self.acc.shape, F32)

    def write(self, o_ref):
        for h in range(N_HEADS):
            o_ref[h * HEAD_DIM:(h + 1) * HEAD_DIM, :] = self.acc[h] / self.l[h:h + 1, :]


REDUCE_CHAINS = 8


def _fold_rows(x, op):
    r, c = x.shape
    x = x.reshape(REDUCE_CHAINS, r // (REDUCE_CHAINS * SUBLANES), SUBLANES, c)
    return op(op(x, axis=1), axis=0)


def _flash_group(k_ref, vt_ref, bias_ref, qpair, tiles, fs):
    for u, (k0, _, _, _) in enumerate(tiles):
        for p in range(N_HEADS // 2):
            kt = k_ref[pl.ds(k0, K_TILE), p * LANES:(p + 1) * LANES]
            s2 = jnp.dot(kt, qpair[p], preferred_element_type=F32)
            fs.s[u, 2 * p] = s2[:, :Q_TILE]
            fs.s[u, 2 * p + 1] = s2[:, Q_TILE:]
    for u, (k0, off_idx, mask_tile, mask_rows) in enumerate(tiles):
        for h in range(N_HEADS):
            s = fs.s[u, h] + bias_ref[h, off_idx]
            if mask_tile is not None:
                s = s + mask_tile
            m_old = fs.m[h:h + 1, :]
            m_tile = jnp.max(_fold_rows(s, jnp.max), axis=0, keepdims=True)
            if mask_rows is not None:
                m_tile = m_tile + mask_rows[h]
            m_new = jnp.maximum(m_old, m_tile)
            shift = m_new if mask_rows is None else m_new - mask_rows[h]
            p = jnp.exp2(s - shift)
            alpha = jnp.exp2(m_old - m_new)
            fs.l[h:h + 1, :] = alpha * fs.l[h:h + 1, :] + jnp.sum(_fold_rows(p, jnp.sum), axis=0, keepdims=True)
            fs.m[h:h + 1, :] = m_new
            fs.p[u, h] = p.astype(BF16)
            vt = vt_ref[h * HEAD_DIM:(h + 1) * HEAD_DIM, pl.ds(k0, K_TILE)]
            fs.acc[h] = alpha * fs.acc[h] + jnp.dot(vt, fs.p[u, h], preferred_element_type=F32)


def _query_pairs(q_ref):
    return [jnp.concatenate([q_ref[2 * p].astype(BF16), q_ref[2 * p + 1].astype(BF16)], axis=1)
            for p in range(N_HEADS // 2)]


def _causal_mask(q0, k0):
    row = lax.broadcasted_iota(jnp.int32, (K_TILE, Q_TILE), 0)
    lane = lax.broadcasted_iota(jnp.int32, (K_TILE, Q_TILE), 1)
    return jnp.where(k0 + row <= q0 + lane, 0.0, NEG_BIG).astype(F32)


def _top_blocks_mask(score, n_valid, n_sel, axis):
    n = score.shape[axis]
    idx = lax.broadcasted_iota(jnp.int32, score.shape, axis)
    low = jnp.float32(-3.0e38)
    s = jnp.where(idx < n_valid, score, low)
    mask = jnp.full(score.shape, NEG_BIG, F32)
    for _ in range(n_sel):
        mx = jnp.max(s, axis=axis, keepdims=True)
        first = jnp.min(jnp.where(s == mx, idx, n), axis=axis, keepdims=True)
        pick = idx == jnp.where(mx > low, first, -1)
        mask = jnp.where(pick, 0.0, mask)
        s = jnp.where(pick, low, s)
    return mask


def _moba_prompt_kernel(q_ref, k_ref, vt_ref, kmean_ref, bias_ref, o_ref, sel_scr, *flash_scr):
    qi = pl.program_id(1)
    q0 = qi * Q_TILE
    own = q0 // K_TILE
    nb = kmean_ref.shape[1]
    fs = _FlashScratch(*flash_scr)
    fs.init()
    for h in range(N_HEADS):
        score = jnp.dot(kmean_ref[h], q_ref[h], preferred_element_type=F32,
                        precision=lax.Precision.HIGHEST)
        sel_scr[h] = _top_blocks_mask(score, own, min(MOBA_TOPK, nb), 0)
    qpair = _query_pairs(q_ref)
    k_own = pl.multiple_of(own * K_TILE, K_TILE)
    _flash_group(k_ref, vt_ref, bias_ref, qpair,
                 [(k_own, (q0 - k_own) // Q_TILE, _causal_mask(q0, k_own), None)], fs)

    def body(g, _):
        tiles = []
        for u in range(FLASH_GROUP):
            kj = g * FLASH_GROUP + u
            k0 = pl.multiple_of(kj * K_TILE, K_TILE)
            off = jnp.clip((q0 - k0) // Q_TILE, 0, N_NEAR)
            tiles.append((k0, off, None, [sel_scr[h, pl.ds(kj, 1), :] for h in range(N_HEADS)]))
        _flash_group(k_ref, vt_ref, bias_ref, qpair, tiles, fs)
        return 0

    lax.fori_loop(0, (own + FLASH_GROUP - 1) // FLASH_GROUP, body, 0)
    fs.write(o_ref)


def _pad_heads_t(q):
    b, t, _ = q.shape
    qt = jnp.transpose(q.reshape(b, t, N_HEADS, HEAD_DIM), (0, 2, 3, 1))
    z = jnp.zeros_like(qt)
    even = jnp.concatenate([qt, z], axis=2)
    odd = jnp.concatenate([z, qt], axis=2)
    is_even = (jnp.arange(N_HEADS) % 2 == 0)[None, :, None, None]
    return jnp.where(is_even, even, odd)


def _block_mean_kernel(k_ref, o_ref):
    o_ref[...] = jnp.mean(k_ref[...], axis=0, keepdims=True)


def block_mean(k):
    b, t, c = k.shape
    nb = t // MOBA_BLOCK
    out = pl.pallas_call(
        _block_mean_kernel,
        grid=(b, nb),
        in_specs=[pl.BlockSpec((None, MOBA_BLOCK, c), lambda i, j: (i, j, 0))],
        out_specs=pl.BlockSpec((None, None, 1, c), lambda i, j: (i, j, 0, 0)),
        out_shape=jax.ShapeDtypeStruct((b, nb, 1, c), F32),
        compiler_params=_cparams("parallel", "parallel"),
        name="block_mean",
    )(k)
    return out.reshape(b, nb, c)


def moba_prompt(q, k, v, bias_tiles):
    b, t, w = q.shape
    assert t % (FLASH_GROUP * K_TILE) == 0 and K_TILE % Q_TILE == 0
    nb = t // MOBA_BLOCK
    qp = _pad_heads_t(q * PROMPT_Q_SCALE)
    kmean = block_mean(k)
    kmp = jnp.transpose(_pad_heads_t(kmean), (0, 1, 3, 2))
    kb = k.astype(BF16)
    vtb = jnp.transpose(v, (0, 2, 1)).astype(BF16)
    ot = pl.pallas_call(
        _moba_prompt_kernel,
        grid=(b, t // Q_TILE),
        in_specs=[pl.BlockSpec((None, N_HEADS, LANES, Q_TILE), lambda i, j: (i, 0, 0, j)),
                  pl.BlockSpec((None, t, w), lambda i, j: (i, 0, 0)),
                  pl.BlockSpec((None, w, t), lambda i, j: (i, 0, 0)),
                  pl.BlockSpec((None, N_HEADS, nb, LANES), lambda i, j: (i, 0, 0, 0)),
                  pl.BlockSpec(bias_tiles.shape, lambda i, j: (0, 0, 0, 0))],
        out_specs=pl.BlockSpec((None, w, Q_TILE), lambda i, j: (i, 0, j)),
        out_shape=jax.ShapeDtypeStruct((b, w, t), F32),
        scratch_shapes=[pltpu.VMEM((N_HEADS, nb, Q_TILE), F32)] + _FlashScratch.shapes(),
        compiler_params=_cparams("parallel", "arbitrary"),
        name="moba_prompt",
    )(qp, kb, vtb, kmp, bias_tiles)
    return jnp.transpose(ot, (0, 2, 1))


def _split_bf16(x):
    hi = x.astype(BF16)
    lo = (x - hi.astype(F32)).astype(BF16)
    return hi, lo


def _ordered_key(x):
    x = jnp.where(x == 0.0, 0.0, x)
    bits = lax.bitcast_convert_type(x, jnp.int32)
    return bits ^ ((bits >> 31) & jnp.int32(0x7FFFFFFF))


INT_MIN = -2 ** 31


def _kth_largest_key(count_ge, kcount):
    int_min = jnp.int32(INT_MIN)
    thr = jnp.where(count_ge(jnp.zeros_like(kcount, jnp.int32)) >= kcount, jnp.int32(0), int_min)

    def bit_body(i, thr):
        cand = thr | lax.shift_left(jnp.int32(1), 30 - i)
        return jnp.where(count_ge(cand) >= kcount, cand, thr)

    return lax.fori_loop(0, 31, bit_body, thr)


RADIX_BITS = 4


def _kth_largest_key_radix(count_ge, kcount):
    thr = jnp.full(kcount.shape, INT_MIN, jnp.int32)
    for shift in range(32 - RADIX_BITS, -1, -RADIX_BITS):
        digit = jnp.zeros(kcount.shape, jnp.int32)
        for j in range(1, 2 ** RADIX_BITS):
            cand = thr + jnp.int32(np.array(j << shift, np.uint32).astype(np.int32))
            digit = digit + jnp.where(count_ge(cand) >= kcount, 1, 0)
        thr = thr + lax.shift_left(digit, shift)
    return thr


def _dsa_prompt_kernel(q_ref, k_ref, vt_ref, kic_ref, qic_ref, wi_ref, bias_ref, o_ref,
                       key_scr, mask_scr, *flash_scr, n_keep):
    qi = pl.program_id(1)
    q0 = qi * Q_TILE
    own = q0 // K_TILE
    n_tiles = own + 1
    qpos = q0 + lax.broadcasted_iota(jnp.int32, (1, Q_TILE), 1)
    row = lax.broadcasted_iota(jnp.int32, (K_TILE, Q_TILE), 0)
    fs = _FlashScratch(*flash_scr)
    fs.init()

    def tile_start(kj):
        return pl.multiple_of(kj * K_TILE, K_TILE)

    n_pairs = (n_tiles + 1) // 2

    def score_body(kp, _):
        for u in range(2):
            k0 = tile_start(2 * kp + u)
            kic = kic_ref[pl.ds(k0, K_TILE), :]
            sc = jnp.zeros((K_TILE, Q_TILE), F32)
            for j in range(IDX_HEADS // 2):
                q2 = jnp.concatenate([qic_ref[2 * j], qic_ref[2 * j + 1]], axis=1)
                d = jnp.maximum(jnp.dot(kic, q2, preferred_element_type=F32), 0.0)
                sc = sc + wi_ref[2 * j:2 * j + 1, :] * d[:, :Q_TILE]
                sc = sc + wi_ref[2 * j + 1:2 * j + 2, :] * d[:, Q_TILE:]
            key = jnp.where(k0 + row <= qpos, _ordered_key(sc), jnp.int32(INT_MIN))
            key_scr[pl.ds(k0, K_TILE), :] = key
        return 0

    lax.fori_loop(0, n_pairs, score_body, 0)

    def count(pred):
        def body(kp, c):
            t = key_scr[pl.ds(pl.multiple_of(kp * 2 * K_TILE, 2 * K_TILE), 2 * K_TILE), :]
            return c + _fold_rows(jnp.where(pred(t), 1.0, 0.0), jnp.sum)
        c = lax.fori_loop(0, n_pairs, body, jnp.zeros((SUBLANES, Q_TILE), F32))
        return jnp.sum(c, axis=0, keepdims=True)

    kcount = jnp.minimum(n_keep, qpos + 1).astype(F32)
    thr = _kth_largest_key(lambda c: count(lambda t: t >= c), kcount)
    need = kcount - count(lambda t: t > thr)
    tri = jnp.where(lax.broadcasted_iota(jnp.int32, (K_TILE, K_TILE), 0)
                    >= lax.broadcasted_iota(jnp.int32, (K_TILE, K_TILE), 1), 1.0, 0.0).astype(BF16)

    def mask_body(kp, seen):
        for u in range(2):
            k0 = tile_start(2 * kp + u)
            t = key_scr[pl.ds(k0, K_TILE), :]
            eq = jnp.where(t == thr, 1.0, 0.0)
            rank = jnp.dot(tri, eq.astype(BF16), preferred_element_type=F32) + seen
            keep = jnp.where(t > thr, 1.0, jnp.where(rank <= need, eq, 0.0))
            mask_scr[pl.ds(k0, K_TILE), :] = jnp.where(keep > 0.5, 0.0, NEG_BIG).astype(BF16)
            seen = seen + jnp.sum(_fold_rows(eq, jnp.sum), axis=0, keepdims=True)
        return seen

    lax.fori_loop(0, n_pairs, mask_body, jnp.zeros((1, Q_TILE), F32))
    qpair = _query_pairs(q_ref)

    @pl.when(n_pairs % 2 == 1)
    def _():
        mask_scr[pl.ds(tile_start(2 * n_pairs), 2 * K_TILE), :] = jnp.full((2 * K_TILE, Q_TILE), NEG_BIG, BF16)

    def body(g, _):
        tiles = []
        for u in range(FLASH_GROUP):
            k0 = tile_start(g * FLASH_GROUP + u)
            off = jnp.clip((q0 - k0) // Q_TILE, 0, N_NEAR)
            tiles.append((k0, off, mask_scr[pl.ds(k0, K_TILE), :].astype(F32), None))
        _flash_group(k_ref, vt_ref, bias_ref, qpair, tiles, fs)
        return 0

    lax.fori_loop(0, (n_pairs + 1) // 2, body, 0)
    fs.write(o_ref)


def dsa_prompt(q, k, v, qi, ki, wi, bias_tiles):
    b, t, w = q.shape
    assert t % (FLASH_GROUP * K_TILE) == 0 and K_TILE % Q_TILE == 0 and FLASH_GROUP == 4
    n_keep = min(DSA_TOPK, t // 4)
    qp = _pad_heads_t(q * PROMPT_Q_SCALE)
    kb = k.astype(BF16)
    vtb = jnp.transpose(v, (0, 2, 1)).astype(BF16)
    ki_hi, ki_lo = _split_bf16(ki)
    kic = jnp.concatenate([ki_hi, ki_hi, ki_lo], axis=-1)
    qis = jnp.transpose((qi * IDX_DIM ** -0.5).reshape(b, t, IDX_HEADS, IDX_DIM), (0, 2, 3, 1))
    qi_hi, qi_lo = _split_bf16(qis)
    qic = jnp.concatenate([qi_hi, qi_lo, qi_hi], axis=2)
    wit = jnp.transpose(wi * IDX_HEADS ** -0.5, (0, 2, 1))
    ot = pl.pallas_call(
        functools.partial(_dsa_prompt_kernel, n_keep=n_keep),
        grid=(b, t // Q_TILE),
        in_specs=[pl.BlockSpec((None, N_HEADS, LANES, Q_TILE), lambda i, j: (i, 0, 0, j)),
                  pl.BlockSpec((None, t, w), lambda i, j: (i, 0, 0)),
                  pl.BlockSpec((None, w, t), lambda i, j: (i, 0, 0)),
                  pl.BlockSpec((None, t, 3 * IDX_DIM), lambda i, j: (i, 0, 0)),
                  pl.BlockSpec((None, IDX_HEADS, 3 * IDX_DIM, Q_TILE), lambda i, j: (i, 0, 0, j)),
                  pl.BlockSpec((None, IDX_HEADS, Q_TILE), lambda i, j: (i, 0, j)),
                  pl.BlockSpec(bias_tiles.shape, lambda i, j: (0, 0, 0, 0))],
        out_specs=pl.BlockSpec((None, w, Q_TILE), lambda i, j: (i, 0, j)),
        out_shape=jax.ShapeDtypeStruct((b, w, t), F32),
        scratch_shapes=[pltpu.VMEM((t, Q_TILE), jnp.int32), pltpu.VMEM((t, Q_TILE), BF16)]
        + _FlashScratch.shapes(),
        compiler_params=_cparams("parallel", "arbitrary"),
        name="dsa_prompt",
    )(qp, kb, vtb, kic, qic, wit, bias_tiles)
    return jnp.transpose(ot, (0, 2, 1))


HEADS_PER_GROUP = N_HEADS // SSM_GROUPS
PAIR_W = HEADS_PER_GROUP * HEAD_DIM
assert PAIR_W == LANES and SSM_DSTATE == LANES and SSM_CHUNK == LANES


def _ssd_prompt_kernel(xdt_ref, b_ref, c_ref, dax_ref, dat_ref, h0_ref, y_ref, hout_ref, st_scr):
    ci = pl.program_id(1)

    @pl.when(ci == 0)
    def _():
        st_scr[...] = h0_ref[...]

    q = SSM_CHUNK
    row = lax.broadcasted_iota(jnp.int32, (q, q), 0)
    col = lax.broadcasted_iota(jnp.int32, (q, q), 1)
    lower = row >= col
    hi = lax.Precision.HIGHEST
    cs_x = jnp.dot(jnp.where(lower, 1.0, 0.0), dax_ref[...], preferred_element_type=F32, precision=hi)
    cs_t = jnp.dot(dat_ref[...], jnp.where(row <= col, 1.0, 0.0), preferred_element_type=F32, precision=hi)
    lane_lo = col < HEAD_DIM
    for g in range(SSM_GROUPS):
        sl = slice(g * PAIR_W, (g + 1) * PAIR_W)
        cg = c_ref[:, g * SSM_DSTATE:(g + 1) * SSM_DSTATE]
        bg = b_ref[:, g * SSM_DSTATE:(g + 1) * SSM_DSTATE]
        xdt = xdt_ref[:, sl]
        cs_pair = cs_x[:, sl]
        cs_swap = pltpu.roll(cs_pair, HEAD_DIM, 1)
        cb = lax.dot_general(cg, bg, (((1,), (1,)), ((), ())), preferred_element_type=F32)
        y_pair = []
        for a in range(HEADS_PER_GROUP):
            h = g * HEADS_PER_GROUP + a
            own_half = lane_lo if a == 0 else jnp.logical_not(lane_lo)
            cs_col = jnp.where(own_half, cs_pair, cs_swap)
            seg = jnp.exp(jnp.minimum(cs_col - cs_t[h:h + 1, :], 0.0))
            scores = cb * jnp.where(lower, seg, 0.0)
            y_pair.append(jnp.dot(scores, xdt, preferred_element_type=F32))
        y_diag = jnp.where(lane_lo, y_pair[0], y_pair[1])
        st = st_scr[sl, :]
        y_off = lax.dot_general(cg, st, (((1,), (1,)), ((), ())), preferred_element_type=F32)
        y_ref[:, sl] = y_diag + y_off * jnp.exp(cs_pair)
        cs_last = cs_pair[q - 1:q, :]
        xdtd = xdt * jnp.exp(cs_last - cs_pair)
        upd = jnp.dot(xdtd.T, bg, preferred_element_type=F32)
        last_a = cs_last[:, 0:1]
        last_b = cs_last[:, HEAD_DIM:HEAD_DIM + 1]
        st_scr[sl, :] = st * jnp.exp(jnp.where(row < HEAD_DIM, last_a, last_b)) + upd

    @pl.when(ci == pl.num_programs(1) - 1)
    def _():
        hout_ref[...] = st_scr[...]


def ssd_prompt(xs, dt, a_neg, bm, cm, h0):
    b, t, w = xs.shape
    assert t % SSM_CHUNK == 0
    rep = lambda z: jnp.repeat(z, HEAD_DIM, axis=-1)
    xdt = xs * rep(dt)
    da = dt * a_neg
    dat = jnp.pad(jnp.transpose(da, (0, 2, 1)), ((0, 0), (0, SUBLANES - N_HEADS), (0, 0)))
    hp = N_HEADS * HEAD_DIM
    chunk = lambda width: pl.BlockSpec((None, SSM_CHUNK, width), lambda i, j: (i, j, 0))
    state = pl.BlockSpec((None, hp, SSM_DSTATE), lambda i, j: (i, 0, 0))
    y, h = pl.pallas_call(
        _ssd_prompt_kernel,
        grid=(b, t // SSM_CHUNK),
        in_specs=[chunk(w), chunk(bm.shape[-1]), chunk(cm.shape[-1]), chunk(w),
                  pl.BlockSpec((None, SUBLANES, SSM_CHUNK), lambda i, j: (i, 0, j)), state],
        out_specs=[chunk(w), state],
        out_shape=[jax.ShapeDtypeStruct((b, t, w), F32), jax.ShapeDtypeStruct((b, hp, SSM_DSTATE), F32)],
        scratch_shapes=[pltpu.VMEM((hp, SSM_DSTATE), F32)],
        compiler_params=_cparams("parallel", "arbitrary"),
        name="ssd_prompt",
    )(xdt, bm, cm, rep(da), dat, h0.reshape(b, hp, SSM_DSTATE))
    return y, h.reshape(b, N_HEADS, HEAD_DIM, SSM_DSTATE)


RWKV_CHUNK = 128
HEAD_PAIRS = N_HEADS // 2


def _split_cat(x):
    hi, lo = _split_bf16(x)
    return jnp.concatenate([hi, lo], axis=1)


def _rwkv_prompt_kernel(r_ref, w_ref, k_ref, kk_ref, kka_ref, v_ref, g_ref, c1_ref, c2_ref, s0_ref,
                        o_ref, sout_ref, s_scr, z_scr, vt_scr, ot_scr):
    ci = pl.program_id(0)
    n_b = r_ref.shape[0]
    tiles = [(b, p) for b in range(n_b) for p in range(HEAD_PAIRS)]
    lanes = lambda p: slice(p * LANES, (p + 1) * LANES)

    r2 = lax.broadcasted_iota(jnp.int32, (2 * LANES, LANES), 0)
    c2 = lax.broadcasted_iota(jnp.int32, (2 * LANES, LANES), 1)
    ones_same = jnp.where((r2 % LANES) // HEAD_DIM == c2 // HEAD_DIM, 1.0, 0.0).astype(BF16)
    r3 = lax.broadcasted_iota(jnp.int32, (LANES, 2 * LANES), 0)
    c3 = lax.broadcasted_iota(jnp.int32, (LANES, 2 * LANES), 1)
    ones_out = jnp.where(r3 // HEAD_DIM == c3 // LANES, 1.0, 0.0).astype(BF16)

    def key_sums(xs):
        out = jnp.dot(jnp.concatenate([_split_cat(x) for x in xs], axis=0), ones_same, preferred_element_type=F32)
        return [out[i * HEAD_DIM:(i + 1) * HEAD_DIM] for i in range(len(xs))]

    @pl.when(ci == 0)
    def _():
        s_scr[...] = s0_ref[...]
        z0 = key_sums([s0_ref[i] * kk_ref[b, 0:1, lanes(p)] for i, (b, p) in enumerate(tiles)])
        for i in range(len(tiles)):
            z_scr[i] = z0[i]

    for i, (b, p) in enumerate(tiles):
        vt_scr[i] = v_ref[b, :, lanes(p)].T
    lane_t = lax.broadcasted_iota(jnp.int32, (1, RWKV_CHUNK), 1)
    head_a = lax.broadcasted_iota(jnp.int32, (HEAD_DIM, LANES), 1) < HEAD_DIM

    def step(t, row):
        at_t = lane_t == t
        states = [s_scr[i] for i in range(len(tiles))]
        ahead = key_sums([s * row(g_ref, b, p) for s, (b, p) in zip(states, tiles)])
        new_states = []
        for i, (b, p) in enumerate(tiles):
            z = z_scr[i]
            v_a = jnp.sum(jnp.where(at_t, vt_scr[i, :HEAD_DIM, :], 0.0), axis=1, keepdims=True)
            v_b = jnp.sum(jnp.where(at_t, vt_scr[i, HEAD_DIM:, :], 0.0), axis=1, keepdims=True)
            v_bc = jnp.where(head_a, v_a, v_b)
            s = states[i] * row(w_ref, b, p) - z * row(kka_ref, b, p) + v_bc * row(k_ref, b, p)
            s_scr[i] = s
            new_states.append(s)
            z_scr[i] = ahead[i] - z * row(c1_ref, b, p) + v_bc * row(c2_ref, b, p)
        y = jnp.concatenate([(s * row(r_ref, b, p)).astype(BF16) for s, (b, p) in zip(new_states, tiles)], axis=0)
        o_all = jnp.dot(y, ones_out, preferred_element_type=F32)
        col_t = jnp.broadcast_to(at_t, (HEAD_DIM, RWKV_CHUNK))
        for i in range(len(tiles)):
            o = o_all[i * HEAD_DIM:(i + 1) * HEAD_DIM]
            pltpu.store(ot_scr.at[i, pl.ds(0, HEAD_DIM), :], o[:, :LANES], mask=col_t)
            pltpu.store(ot_scr.at[i, pl.ds(HEAD_DIM, HEAD_DIM), :], o[:, LANES:], mask=col_t)

    def step_group(g, _):
        base = pl.multiple_of(g * SUBLANES, SUBLANES)
        refs = (r_ref, w_ref, k_ref, kka_ref, g_ref, c1_ref, c2_ref)
        blocks = {(id(ref), b, p): ref[b, pl.ds(base, SUBLANES), lanes(p)] for ref in refs for b, p in tiles}
        for j in range(SUBLANES):
            step(base + j, lambda ref, b, p, j=j: blocks[(id(ref), b, p)][j:j + 1, :])
        return 0

    lax.fori_loop(0, RWKV_CHUNK // SUBLANES, step_group, 0)
    for i, (b, p) in enumerate(tiles):
        o_ref[b, :, lanes(p)] = ot_scr[i].T

    @pl.when(ci == pl.num_programs(0) - 1)
    def _():
        sout_ref[...] = s_scr[...]


def rwkv_prompt(r, w, k, v, kk, kka, s0):
    b, t, hw = r.shape
    assert t % RWKV_CHUNK == 0 and RWKV_CHUNK == LANES and 2 * HEAD_DIM == LANES
    n_tiles = b * HEAD_PAIRS
    pack = lambda s: jnp.transpose(s.reshape(b, HEAD_PAIRS, 2, HEAD_DIM, HEAD_DIM), (0, 1, 3, 2, 4)).reshape(
        n_tiles, HEAD_DIM, LANES)
    unpack = lambda s: jnp.transpose(s.reshape(b, HEAD_PAIRS, HEAD_DIM, 2, HEAD_DIM), (0, 1, 3, 2, 4)).reshape(
        b, N_HEADS, HEAD_DIM, HEAD_DIM)
    kk_next = jnp.concatenate([kk[:, 1:], jnp.zeros_like(kk[:, :1])], axis=1)
    head_dot = lambda x: jnp.repeat(jnp.sum((x * kk_next).reshape(b, t, N_HEADS, HEAD_DIM), axis=-1), HEAD_DIM, axis=-1)
    seq_spec = pl.BlockSpec((b, RWKV_CHUNK, hw), lambda c: (0, c, 0))
    st_spec = pl.BlockSpec((n_tiles, HEAD_DIM, LANES), lambda c: (0, 0, 0))
    o, s = pl.pallas_call(
        _rwkv_prompt_kernel,
        grid=(t // RWKV_CHUNK,),
        in_specs=[seq_spec] * 9 + [st_spec],
        out_specs=[seq_spec, st_spec],
        out_shape=[jax.ShapeDtypeStruct((b, t, hw), F32), jax.ShapeDtypeStruct((n_tiles, HEAD_DIM, LANES), F32)],
        scratch_shapes=[pltpu.VMEM((n_tiles, HEAD_DIM, LANES), F32),
                        pltpu.VMEM((n_tiles, HEAD_DIM, LANES), F32),
                        pltpu.VMEM((n_tiles, LANES, RWKV_CHUNK), F32),
                        pltpu.VMEM((n_tiles, LANES, RWKV_CHUNK), F32)],
        compiler_params=_cparams("arbitrary"),
        name="rwkv_prompt",
    )(r, w, k, kk, kka, v, w * kk_next, head_dot(kka), head_dot(k), pack(s0))
    return o, unpack(s)


STEP_SEQS = 8
NT_DIMS = (((1,), (1,)), ((), ()))


def _row_dot(row, mat):
    r8 = jnp.broadcast_to(row, (SUBLANES, row.shape[1]))
    return lax.dot_general(r8, mat, NT_DIMS, preferred_element_type=F32,
                           precision=lax.Precision.HIGHEST)[0:1, :]


def _ssd_step_kernel(h0_ref, xdtb_ref, dab_ref, b_ref, c_ref, h_ref, y_ref):
    for s in range(h0_ref.shape[0]):
        for h in range(N_HEADS):
            hn = h0_ref[s, h] * jnp.exp(dab_ref[s, h]) + xdtb_ref[s, h] * b_ref[s, h]
            h_ref[s, h] = hn
            y_ref[s, h] = _row_dot(c_ref[s, h], hn)


def ssd_step(xs, dt, a_neg, bm, cm, h0):
    s = xs.shape[0]
    assert s % STEP_SEQS == 0
    xdtb = jnp.broadcast_to((xs.reshape(s, N_HEADS, HEAD_DIM) * dt[:, :, None])[..., None],
                            (s, N_HEADS, HEAD_DIM, SSM_DSTATE))
    dab = jnp.broadcast_to((dt * a_neg)[:, :, None, None], (s, N_HEADS, 1, SSM_DSTATE))
    per_head = lambda z: jnp.repeat(z.reshape(s, SSM_GROUPS, 1, SSM_DSTATE), HEADS_PER_GROUP, axis=1)
    big = pl.BlockSpec((STEP_SEQS, N_HEADS, HEAD_DIM, SSM_DSTATE), lambda i: (i, 0, 0, 0))
    row = pl.BlockSpec((STEP_SEQS, N_HEADS, 1, SSM_DSTATE), lambda i: (i, 0, 0, 0))
    h, y = pl.pallas_call(
        _ssd_step_kernel,
        grid=(s // STEP_SEQS,),
        in_specs=[big, big, row, row, row],
        out_specs=[big, pl.BlockSpec((STEP_SEQS, N_HEADS, 1, HEAD_DIM), lambda i: (i, 0, 0, 0))],
        out_shape=[jax.ShapeDtypeStruct(h0.shape, F32), jax.ShapeDtypeStruct((s, N_HEADS, 1, HEAD_DIM), F32)],
        compiler_params=_cparams("parallel"),
        name="ssd_step",
    )(h0, xdtb, dab, per_head(bm), per_head(cm))
    return y.reshape(s, N_HEADS * HEAD_DIM), h


def _rwkv_step_kernel(s0_ref, vb_ref, w_ref, kk_ref, kka_ref, k_ref, r_ref, s_ref, o_ref):
    for s in range(s0_ref.shape[0]):
        for h in range(N_HEADS):
            st = s0_ref[s, h]
            sa = -jnp.sum(st * kk_ref[s, h], axis=1, keepdims=True)
            st = st * w_ref[s, h] + sa * kka_ref[s, h] + vb_ref[s, h] * k_ref[s, h]
            s_ref[s, h] = st
            o_ref[s, h] = _row_dot(r_ref[s, h], st)


def rwkv_step(r, w, k, v, kk, kka, s0):
    s = r.shape[0]
    assert s % STEP_SEQS == 0
    rows = lambda z: z.reshape(s, N_HEADS, 1, HEAD_DIM)
    vb = jnp.broadcast_to(v.reshape(s, N_HEADS, HEAD_DIM, 1), (s, N_HEADS, HEAD_DIM, HEAD_DIM))
    big = pl.BlockSpec((STEP_SEQS, N_HEADS, HEAD_DIM, HEAD_DIM), lambda i: (i, 0, 0, 0))
    row = pl.BlockSpec((STEP_SEQS, N_HEADS, 1, HEAD_DIM), lambda i: (i, 0, 0, 0))
    st, o = pl.pallas_call(
        _rwkv_step_kernel,
        grid=(s // STEP_SEQS,),
        in_specs=[big, big, row, row, row, row, row],
        out_specs=[big, row],
        out_shape=[jax.ShapeDtypeStruct(s0.shape, F32), jax.ShapeDtypeStruct((s, N_HEADS, 1, HEAD_DIM), F32)],
        compiler_params=_cparams("parallel"),
        name="rwkv_step",
    )(s0, vb, rows(w), rows(kk), rows(kka), rows(k), rows(r))
    return o.reshape(s, N_HEADS * HEAD_DIM), st


def _head_rows_to_row(o):
    r = lax.broadcasted_iota(jnp.int32, o.shape, 0)
    c = lax.broadcasted_iota(jnp.int32, o.shape, 1)
    return jnp.sum(jnp.where(c // HEAD_DIM == r, o, 0.0), axis=0, keepdims=True)


def _sample_softmax_pv(logit_pages, s_new, vt_pages, vnew):
    m = s_new
    for s in logit_pages:
        m = jnp.maximum(m, jnp.max(s, axis=1, keepdims=True))
    p_new = jnp.exp(s_new - m)
    l = p_new
    acc = p_new * vnew
    for s, vt_ref in zip(logit_pages, vt_pages):
        p = jnp.exp(s - m)
        l = l + jnp.sum(p, axis=1, keepdims=True)
        acc = acc + lax.dot_general(p.astype(BF16), vt_ref[...].astype(BF16), NT_DIMS, preferred_element_type=F32)
    return _head_rows_to_row(acc / l)


def _moba_sample_kernel(pt_ref, qbd_ref, knew_ref, vnew_ref, bias_ref, biasn_ref, *rest, n_pages, page):
    del pt_ref
    kt_pages, vt_pages, o_ref = rest[:n_pages], rest[n_pages:2 * n_pages], rest[2 * n_pages]
    qbd = qbd_ref[...]
    q_hi, q_lo = _split_bf16(qbd)
    q_both = jnp.concatenate([q_hi, q_lo], axis=0)
    ppb = MOBA_BLOCK // page
    n_blocks = n_pages // ppb
    lane = lax.broadcasted_iota(jnp.int32, (SUBLANES, LANES), 1)
    raw = []
    score = jnp.zeros((SUBLANES, LANES), F32)
    for p in range(n_pages):
        k_hi, k_lo = _split_bf16(kt_pages[p][...])
        r = jnp.dot(q_both, k_hi, preferred_element_type=F32)
        r = r[:SUBLANES] + r[SUBLANES:] + jnp.dot(q_hi, k_lo, preferred_element_type=F32)
        raw.append(r)
        score = score + jnp.where(lane == p // ppb, jnp.sum(r, axis=1, keepdims=True) * (1.0 / MOBA_BLOCK), 0.0)
    sel = _top_blocks_mask(score, n_blocks, min(MOBA_TOPK, n_blocks + 1), 1)
    logits = [raw[p] + bias_ref[:, p * page:(p + 1) * page] + sel[:, p // ppb:p // ppb + 1]
              for p in range(n_pages)]
    s_new = jnp.sum(qbd * knew_ref[...], axis=1, keepdims=True) + biasn_ref[:, 0:1]
    o_ref[...] = _sample_softmax_pv(logits, s_new, vt_pages, vnew_ref[...])


def _block_diag_q(q):
    s = q.shape[0]
    eye = jnp.eye(N_HEADS, dtype=q.dtype)
    bd = (q.reshape(s, 1, N_HEADS, HEAD_DIM) * eye[None, :, :, None]).reshape(s, N_HEADS, N_HEADS * HEAD_DIM)
    return jnp.pad(bd, ((0, 0), (0, SUBLANES - N_HEADS), (0, 0)))


def _sample_bias(tab, past_len):
    dist = past_len - jnp.arange(past_len, dtype=jnp.int32)
    pad = ((0, SUBLANES - N_HEADS), (0, 0))
    past = jnp.pad(_bucket_lookup(tab, _rel_bucket(dist)), pad)
    new = jnp.pad(_bucket_lookup(tab, _rel_bucket(jnp.zeros((LANES,), jnp.int32))), pad)
    return past, new


def _page_specs(layer, n_pages, rows, page):
    return [pl.BlockSpec((None, None, rows, page), functools.partial(
        lambda s, pt, p: (layer, pt[s, p], 0, 0), p=p)) for p in range(n_pages)]


def _transposed_pages(cache):
    c = cache.reshape(cache.shape[:3] + (-1,))
    return jnp.transpose(c, (0, 1, 3, 2))


def moba_sample(q, knew, vnew, cache_kt, cache_vt, layer, page_table, tab):
    s, w = q.shape
    n_pages, page = page_table.shape[1], cache_kt.shape[3]
    past_len = n_pages * page
    assert past_len % MOBA_BLOCK == 0 and MOBA_BLOCK % page == 0
    bias_p, bias_n = _sample_bias(tab, past_len)
    row = pl.BlockSpec((None, 1, w), lambda i, pt: (i, 0, 0))
    const = lambda shape: pl.BlockSpec(shape, lambda i, pt: (0, 0))
    out = pl.pallas_call(
        functools.partial(_moba_sample_kernel, n_pages=n_pages, page=page),
        grid_spec=pltpu.PrefetchScalarGridSpec(
            num_scalar_prefetch=1, grid=(s,),
            in_specs=[pl.BlockSpec((None, SUBLANES, w), lambda i, pt: (i, 0, 0)), row, row,
                      const(bias_p.shape), const(bias_n.shape)]
            + _page_specs(layer, n_pages, w, page) * 2,
            out_specs=row),
        out_shape=jax.ShapeDtypeStruct((s, 1, w), F32),
        compiler_params=_cparams("arbitrary"),
        name="moba_sample",
    )(page_table, _block_diag_q(q * HEAD_DIM ** -0.5), knew[:, None], vnew[:, None], bias_p, bias_n,
      *([cache_kt] * n_pages), *([cache_vt] * n_pages))
    return out.reshape(s, w)


def _dsa_sample_kernel(pt_ref, qbd_ref, knew_ref, vnew_ref, bias_ref, biasn_ref, qi_ref, wi_ref, kin_ref,
                       *rest, n_pages, page, n_keep):
    del pt_ref
    kt_pages, vt_pages = rest[:n_pages], rest[n_pages:2 * n_pages]
    kit_pages, o_ref = rest[2 * n_pages:3 * n_pages], rest[3 * n_pages]
    qi = qi_ref[...]
    wi = wi_ref[:, 0:1]
    hi = lax.Precision.HIGHEST
    keys = []
    for p in range(n_pages):
        d = jnp.dot(qi, kit_pages[p][...], preferred_element_type=F32, precision=hi)
        keys.append(_ordered_key(jnp.sum(wi * jnp.maximum(d, 0.0), axis=0, keepdims=True)))
    d_new = jnp.sum(qi * kin_ref[...], axis=1, keepdims=True)
    key_new = _ordered_key(jnp.sum(wi * jnp.maximum(d_new, 0.0), axis=0, keepdims=True))

    key_all = jnp.concatenate(keys, axis=0)

    def total(x):
        return jnp.sum(jnp.sum(x, axis=0, keepdims=True), axis=1, keepdims=True)

    def count(pred):
        return jnp.where(pred(key_new), 1.0, 0.0) + total(jnp.where(pred(key_all), 1.0, 0.0))

    kcount = jnp.full((1, 1), float(n_keep), F32)
    thr = _kth_largest_key_radix(lambda c: count(lambda t: t >= c), kcount)
    need = kcount - count(lambda t: t > thr)
    eq = jnp.where(key_all == thr, 1.0, 0.0)
    upper = jnp.where(lax.broadcasted_iota(jnp.int32, (page, page), 0)
                      <= lax.broadcasted_iota(jnp.int32, (page, page), 1), 1.0, 0.0).astype(BF16)
    before = jnp.where(lax.broadcasted_iota(jnp.int32, (n_pages, n_pages), 1)
                       < lax.broadcasted_iota(jnp.int32, (n_pages, n_pages), 0), 1.0, 0.0).astype(BF16)
    per_page = jnp.broadcast_to(jnp.sum(eq, axis=1, keepdims=True), eq.shape).astype(BF16)
    rank = (jnp.dot(eq.astype(BF16), upper, preferred_element_type=F32)
            + jnp.dot(before, per_page, preferred_element_type=F32))
    keep = jnp.where(key_all > thr, 1.0, jnp.where(rank <= need, eq, 0.0))
    mask_all = jnp.where(keep > 0.5, 0.0, NEG_BIG)
    masks = [mask_all[p:p + 1, :] for p in range(n_pages)]
    keep_new = jnp.where(key_new > thr, 1.0,
                         jnp.where(key_new == thr, jnp.where(total(eq) + 1.0 <= need, 1.0, 0.0), 0.0))
    mask_new = jnp.where(keep_new > 0.5, 0.0, NEG_BIG)
    qbd = qbd_ref[...]
    qb = qbd.astype(BF16)
    logits = [jnp.dot(qb, kt_pages[p][...].astype(BF16), preferred_element_type=F32)
              + bias_ref[:, p * page:(p + 1) * page] + masks[p] for p in range(n_pages)]
    s_new = jnp.sum(qbd * knew_ref[...], axis=1, keepdims=True) + biasn_ref[:, 0:1] + mask_new
    o_ref[...] = _sample_softmax_pv(logits, s_new, vt_pages, vnew_ref[...])


def dsa_sample(q, knew, vnew, qi, wi, kinew, cache_kt, cache_vt, cache_kit, layer, page_table, tab):
    s, w = q.shape
    n_pages, page = page_table.shape[1], cache_kt.shape[3]
    past_len = n_pages * page
    n_keep = min(DSA_TOPK, (past_len + 1) // 4)
    bias_p, bias_n = _sample_bias(tab, past_len)
    row = pl.BlockSpec((None, 1, w), lambda i, pt: (i, 0, 0))
    const = lambda shape: pl.BlockSpec(shape, lambda i, pt: (0, 0))
    qis = (qi * IDX_DIM ** -0.5).reshape(s, IDX_HEADS, IDX_DIM)
    wib = jnp.broadcast_to((wi * IDX_HEADS ** -0.5)[:, :, None], (s, IDX_HEADS, LANES))
    out = pl.pallas_call(
        functools.partial(_dsa_sample_kernel, n_pages=n_pages, page=page, n_keep=n_keep),
        grid_spec=pltpu.PrefetchScalarGridSpec(
            num_scalar_prefetch=1, grid=(s,),
            in_specs=[pl.BlockSpec((None, SUBLANES, w), lambda i, pt: (i, 0, 0)), row, row,
                      const(bias_p.shape), const(bias_n.shape),
                      pl.BlockSpec((None, IDX_HEADS, IDX_DIM), lambda i, pt: (i, 0, 0)),
                      pl.BlockSpec((None, IDX_HEADS, LANES), lambda i, pt: (i, 0, 0)),
                      pl.BlockSpec((None, 1, IDX_DIM), lambda i, pt: (i, 0, 0))]
            + _page_specs(layer, n_pages, w, page) * 2 + _page_specs(layer, n_pages, IDX_DIM, page),
            out_specs=row),
        out_shape=jax.ShapeDtypeStruct((s, 1, w), F32),
        compiler_params=_cparams("arbitrary"),
        name="dsa_sample",
    )(page_table, _block_diag_q(q * HEAD_DIM ** -0.5), knew[:, None], vnew[:, None], bias_p, bias_n,
      qis, wib, kinew[:, None], *([cache_kt] * n_pages), *([cache_vt] * n_pages), *([cache_kit] * n_pages))
    return out.reshape(s, w)


N_IN = sum(IN_SPLITS)
N_IN_PAD = -(-N_IN // LANES) * LANES


def _rms_norm(x, g):
    return x * lax.rsqrt(jnp.mean(x * x, axis=-1, keepdims=True) + NORM_EPS) * g


def _split_cols(a, sizes):
    return jnp.split(a, [int(i) for i in np.cumsum(sizes)[:-1]], axis=-1)


def _causal_dwconv(x, buf, w, b):
    width = w.shape[0]
    length = x.shape[1]
    xp = jnp.concatenate([buf, x], axis=1)
    y = b + sum(xp[:, j:j + length] * w[j] for j in range(width))
    return y, xp[:, length:]


def _tokens_matmul(x, w):
    b, t, k = x.shape
    return matmul(x.reshape(b * t, k), w).reshape(b, t, w.shape[1])


def _mamba(pz, pxbc, pdt, conv_buf, h0, conv_w, conv_b, dt_bias, a_log, d_skip, norm_g):
    b, t, _ = pz.shape
    xbc, new_buf = _causal_dwconv(pxbc, conv_buf, conv_w, conv_b)
    xbc = jax.nn.silu(xbc)
    xs, bm, cm = jnp.split(xbc, [BRANCH_W, BRANCH_W + SSM_GROUPS * SSM_DSTATE], axis=-1)
    dt = jax.nn.softplus(pdt + dt_bias)
    a_neg = -jnp.exp(a_log)
    if t == 1:
        y, h_new = ssd_step(xs[:, 0], dt[:, 0], a_neg, bm[:, 0], cm[:, 0], h0)
        y = y[:, None]
    else:
        y, h_new = ssd_prompt(xs, dt, a_neg, bm, cm, h0)
    y = y + jnp.repeat(d_skip, HEAD_DIM) * xs
    return _rms_norm(y * jax.nn.silu(pz), norm_g), new_buf, h_new


def _rwkv(p, shift_prev, s0, mu, w0, w2, a0, a2, g2, k_k, k_a, r_k, ln_w, ln_b):
    b, t, _ = p.shape
    p_prev = jnp.concatenate([shift_prev[:, None], p[:, :-1]], axis=1)
    pm = p + (p_prev - p) * mu
    r, k, v, xw, xa, xg = _split_cols(pm, (BRANCH_W, BRANCH_W, BRANCH_W, RWKV_W_LORA, RWKV_A_LORA, RWKV_G_LORA))
    logw = -jax.nn.softplus(-(w0 + _tokens_matmul(jnp.tanh(xw), w2))) - 0.5
    decay = jnp.exp(-jnp.exp(logw))
    a = jax.nn.sigmoid(a0 + _tokens_matmul(xa, a2))
    g = _tokens_matmul(jax.nn.sigmoid(xg), g2)
    heads = lambda z: z.reshape(b, t, N_HEADS, HEAD_DIM)
    kk = heads(k * k_k)
    kk = (kk * lax.rsqrt(jnp.maximum(jnp.sum(kk * kk, axis=-1, keepdims=True), 1e-24))).reshape(b, t, BRANCH_W)
    k = k * (1.0 + (a - 1.0) * k_a)
    if t == 1:
        o, s_new = rwkv_step(r[:, 0], decay[:, 0], k[:, 0], v[:, 0], kk[:, 0], (kk * a)[:, 0], s0)
        o = o[:, None]
    else:
        o, s_new = rwkv_prompt(r, decay, k, v, kk, kk * a, s0)
    o = heads(o)
    mean = jnp.mean(o, axis=-1, keepdims=True)
    var = jnp.mean(jnp.square(o - mean), axis=-1, keepdims=True)
    o = ((o - mean) * lax.rsqrt(var + RWKV_GN_EPS)).reshape(b, t, BRANCH_W) * ln_w + ln_b
    bonus = jnp.sum(heads(r) * heads(k) * r_k, axis=-1, keepdims=True) * heads(v)
    return (o + bonus.reshape(b, t, BRANCH_W)) * g, p[:, -1], s_new


PROJ_ORDER = (13, 12, 0, 1, 3, 4, 5, 6, 7, 8, 9, 2, 10, 11)
MERGE_ROWS = 256


def _merge_kernel(oa_ref, ob_ref, oc_ref, od_ref, g_ref, wb_ref, wo_ref, x_ref, out_ref):
    mix = None
    for n, o_ref in enumerate((oa_ref, ob_ref, oc_ref, od_ref)):
        pr = jnp.dot(o_ref[...].astype(BF16), wb_ref[n], preferred_element_type=F32)
        term = jax.nn.sigmoid(g_ref[:, n * D_MODEL:(n + 1) * D_MODEL]) * pr
        mix = term if mix is None else mix + term
    out_ref[...] = x_ref[...] + jnp.dot(mix.astype(BF16), wo_ref[...], preferred_element_type=F32)


def merge_branches(x, branches, proj, w_branch, w_out):
    m = x.shape[0]
    tm = _pick_tile(m, MERGE_ROWS, SUBLANES)
    tok = lambda width: pl.BlockSpec((tm, width), lambda i: (i, 0))
    return pl.pallas_call(
        _merge_kernel,
        grid=(m // tm,),
        in_specs=[tok(BRANCH_W)] * N_BRANCH + [tok(N_BRANCH * D_MODEL),
                  pl.BlockSpec((N_BRANCH, BRANCH_W, D_MODEL), lambda i: (0, 0, 0)),
                  pl.BlockSpec((D_MODEL, D_MODEL), lambda i: (0, 0)), tok(D_MODEL)],
        out_specs=tok(D_MODEL),
        out_shape=jax.ShapeDtypeStruct((m, D_MODEL), F32),
        compiler_params=_cparams("parallel"),
        name="merge_branches",
    )(*branches, proj, w_branch.astype(BF16), w_out.astype(BF16), x)


FFN_ROWS = 256


def _ffn_down_kernel(a_ref, bv_ref, halo_ref, cw_ref, cb_ref, wd_ref, x_ref, out_ref):
    a = a_ref[...]
    tm = a.shape[0]
    ext = jnp.concatenate([halo_ref[...], a], axis=0)
    y = cb_ref[...] + cw_ref[FFN_CONV - 1:FFN_CONV, :] * a
    for back in range(1, FFN_CONV):
        y = y + cw_ref[FFN_CONV - 1 - back:FFN_CONV - back, :] * ext[SUBLANES - back:SUBLANES - back + tm, :]
    g = (y * jax.nn.sigmoid(y)) * bv_ref[...]
    out_ref[...] = x_ref[...] + jnp.dot(g.astype(BF16), wd_ref[...], preferred_element_type=F32)


def ffn_down(x, up, buf, conv_w, conv_b, w_down):
    b, t, d = x.shape
    tm = _pick_tile(t, FFN_ROWS, SUBLANES)
    nt = t // tm
    keep = FFN_CONV - 1
    a_tail = up.reshape(b, nt, tm, 2 * D_FF)[:, :, tm - keep:, :D_FF]
    before = jnp.concatenate([buf[:, None], a_tail[:, :-1]], axis=1)
    halo = jnp.pad(before, ((0, 0), (0, 0), (SUBLANES - keep, 0), (0, 0))).reshape(b * nt, SUBLANES, D_FF)
    m = b * t
    out = pl.pallas_call(
        _ffn_down_kernel,
        grid=(m // tm,),
        in_specs=[pl.BlockSpec((tm, D_FF), lambda i: (i, 0)), pl.BlockSpec((tm, D_FF), lambda i: (i, 1)),
                  pl.BlockSpec((None, SUBLANES, D_FF), lambda i: (i, 0, 0)),
                  pl.BlockSpec((FFN_CONV, D_FF), lambda i: (0, 0)), pl.BlockSpec((1, D_FF), lambda i: (0, 0)),
                  pl.BlockSpec((D_FF, d), lambda i: (0, 0)), pl.BlockSpec((tm, d), lambda i: (i, 0))],
        out_specs=pl.BlockSpec((tm, d), lambda i: (i, 0)),
        out_shape=jax.ShapeDtypeStruct((m, d), F32),
        compiler_params=_cparams("parallel"),
        name="ffn_down",
    )(up.reshape(m, 2 * D_FF), up.reshape(m, 2 * D_FF), halo, conv_w, conv_b[None], w_down.astype(BF16),
      x.reshape(m, d))
    return out.reshape(b, t, d), up[:, t - keep:, :D_FF]


def _layer(x, l, past, prm, attn):
    ssm_h, ssm_buf, rw_s, rw_shift, ffn_buf = past
    b, t, _ = x.shape
    h = _rms_norm(x, prm["norm1_g"][l])
    w_cols = _split_cols(prm["w_in"][l], IN_SPLITS)
    w_in = jnp.concatenate([w_cols[i] for i in PROJ_ORDER] + [jnp.zeros((D_MODEL, N_IN_PAD - N_IN), F32)], axis=1)
    proj = _tokens_matmul(h, w_in)
    parts = dict(zip(PROJ_ORDER, _split_cols(proj[..., :N_IN], [IN_SPLITS[i] for i in PROJ_ORDER])))
    (s_z, s_xbc, s_dt, m_q, m_k, m_v, d_q, d_k, d_v, d_qi, d_ki, d_wi, p_rwkv, _) = (parts[i] for i in range(14))
    o_a, ssm_buf_new, ssm_h_new = _mamba(s_z, s_xbc, s_dt, ssm_buf, ssm_h, prm["ssm_conv_w"][l], prm["ssm_conv_b"][l],
                                         prm["ssm_dt_bias"][l], prm["ssm_A_log"][l], prm["ssm_D"][l],
                                         prm["ssm_norm_g"][l])
    o_b = attn("moba", l, m_q, m_k, m_v)
    o_c = attn("dsa", l, d_q, d_k, d_v, d_qi, d_ki, d_wi)
    o_d, rw_shift_new, rw_s_new = _rwkv(p_rwkv, rw_shift, rw_s, prm["rwkv_mu"][l], prm["rwkv_w0"][l], prm["rwkv_w2"][l],
                                        prm["rwkv_a0"][l], prm["rwkv_a2"][l], prm["rwkv_g2"][l], prm["rwkv_k_k"][l],
                                        prm["rwkv_k_a"][l], prm["rwkv_r_k"][l], prm["rwkv_ln_w"][l], prm["rwkv_ln_b"][l])
    flat = lambda z: z.reshape(b * t, z.shape[-1])
    x = merge_branches(flat(x), [flat(o) for o in (o_a, o_b, o_c, o_d)], flat(proj), prm["w_branch"][l],
                       prm["w_out"][l]).reshape(b, t, D_MODEL)
    up = _tokens_matmul(_rms_norm(x, prm["norm2_g"][l]), prm["ffn_w_up"][l])
    if t >= FFN_CONV:
        x, ffn_buf_new = ffn_down(x, up, ffn_buf, prm["ffn_conv_w"][l], prm["ffn_conv_b"][l], prm["ffn_w_down"][l])
    else:
        a, bv = jnp.split(up, 2, axis=-1)
        a, ffn_buf_new = _causal_dwconv(a, ffn_buf, prm["ffn_conv_w"][l], prm["ffn_conv_b"][l])
        x = x + _tokens_matmul(jax.nn.silu(a) * bv, prm["ffn_w_down"][l])
    heads = lambda z: z.reshape(b, t, N_HEADS, HEAD_DIM)
    new = (heads(m_k), heads(m_v), heads(d_k), heads(d_v), d_ki, ssm_h_new, ssm_buf_new, rw_s_new, rw_shift_new,
           ffn_buf_new)
    return x, new


def kernel(x_prompt, x_sample, cache_moba_k, cache_moba_v, cache_dsa_k, cache_dsa_v, cache_dsa_kidx, page_table,
           state_ssm, state_ssm_conv, state_rwkv, state_rwkv_shift, state_ffn_conv,
           norm1_g, w_in, ssm_conv_w, ssm_conv_b, ssm_dt_bias, ssm_A_log, ssm_D, ssm_norm_g,
           rwkv_mu, rwkv_w0, rwkv_w2, rwkv_a0, rwkv_a2, rwkv_g2, rwkv_k_k, rwkv_k_a, rwkv_r_k, rwkv_ln_w, rwkv_ln_b,
           w_branch, w_out, norm2_g, ffn_w_up, ffn_conv_w, ffn_conv_b, ffn_w_down, rel_bias, final_g):
    prm = dict(norm1_g=norm1_g, w_in=w_in, ssm_conv_w=ssm_conv_w, ssm_conv_b=ssm_conv_b, ssm_dt_bias=ssm_dt_bias,
               ssm_A_log=ssm_A_log, ssm_D=ssm_D, ssm_norm_g=ssm_norm_g, rwkv_mu=rwkv_mu, rwkv_w0=rwkv_w0,
               rwkv_w2=rwkv_w2, rwkv_a0=rwkv_a0, rwkv_a2=rwkv_a2, rwkv_g2=rwkv_g2, rwkv_k_k=rwkv_k_k,
               rwkv_k_a=rwkv_k_a, rwkv_r_k=rwkv_r_k, rwkv_ln_w=rwkv_ln_w, rwkv_ln_b=rwkv_ln_b, w_branch=w_branch,
               w_out=w_out, norm2_g=norm2_g, ffn_w_up=ffn_w_up, ffn_conv_w=ffn_conv_w, ffn_conv_b=ffn_conv_b,
               ffn_w_down=ffn_w_down)
    depth = w_in.shape[0]
    bp = x_prompt.shape[0]
    ns = x_sample.shape[0]
    assert x_sample.shape[1] == 1
    tab_moba, tab_dsa = rel_bias[:, :N_HEADS], rel_bias[:, N_HEADS:]
    tiles_moba, tiles_dsa = _bias_tiles(tab_moba * LOG2E), _bias_tiles(tab_dsa * LOG2E)
    ck_m, cv_m, ck_d, cv_d, cki_d = (_transposed_pages(c) for c in (cache_moba_k, cache_moba_v, cache_dsa_k,
                                                                    cache_dsa_v, cache_dsa_kidx))

    def attn_prompt(kind, l, q, k, v, qi=None, ki=None, wi=None):
        if kind == "moba":
            return moba_prompt(q, k, v, tiles_moba)
        return dsa_prompt(q, k, v, qi, ki, wi, tiles_dsa)

    def attn_sample(kind, l, q, k, v, qi=None, ki=None, wi=None):
        if kind == "moba":
            return moba_sample(q[:, 0], k[:, 0], v[:, 0], ck_m, cv_m, l, page_table, tab_moba)[:, None]
        return dsa_sample(q[:, 0], k[:, 0], v[:, 0], qi[:, 0], wi[:, 0], ki[:, 0], ck_d, cv_d, cki_d,
                          l, page_table, tab_dsa)[:, None]

    zp = lambda *shape: jnp.zeros((bp,) + shape, F32)
    prompt_past = (zp(N_HEADS, HEAD_DIM, SSM_DSTATE), zp(SSM_CONV - 1, SSM_CONV_DIM),
                   zp(N_HEADS, HEAD_DIM, HEAD_DIM), zp(RWKV_IN), zp(FFN_CONV - 1, D_FF))
    xp, xs = x_prompt, x_sample
    new_p, new_s = [], []
    for l in range(depth):
        sample_past = (state_ssm[l], state_ssm_conv[l], state_rwkv[l], state_rwkv_shift[l], state_ffn_conv[l])
        xp, np_l = _layer(xp, l, prompt_past, prm, attn_prompt)
        xs, ns_l = _layer(xs, l, sample_past, prm, attn_sample)
        new_p.append(np_l)
        new_s.append(ns_l)
    outs = [_rms_norm(xp, final_g), _rms_norm(xs, final_g)]
    for i in range(10):
        outs.append(jnp.stack([o[i] for o in new_p], axis=0))
        outs.append(jnp.stack([o[i] for o in new_s], axis=0))
    return tuple(outs)
```

```python
import functools
import math

import jax
import jax.numpy as jnp
import numpy as np
from jax import lax
from jax.experimental import pallas as pl
from jax.experimental.pallas import tpu as pltpu

D_MODEL = 1024
N_BRANCH = 4
BRANCH_W = D_MODEL // N_BRANCH
HEAD_DIM = 64
N_HEADS = BRANCH_W // HEAD_DIM
SSM_DSTATE = 128
SSM_GROUPS = 2
SSM_CONV = 4
SSM_CHUNK = 128
SSM_CONV_DIM = BRANCH_W + 2 * SSM_GROUPS * SSM_DSTATE
MOBA_BLOCK = 256
MOBA_TOPK = 3
DSA_TOPK = 256
IDX_HEADS = 8
IDX_DIM = 32
RWKV_W_LORA = 64
RWKV_A_LORA = 64
RWKV_G_LORA = 128
RWKV_IN = 3 * BRANCH_W + RWKV_W_LORA + RWKV_A_LORA + RWKV_G_LORA
RWKV_GN_EPS = 64e-5
D_FF = 11 * D_MODEL // 4
FFN_CONV = 3
NUM_BUCKETS = 32
MAX_DISTANCE = 1024
NORM_EPS = 1e-6
IN_SPLITS = (BRANCH_W, SSM_CONV_DIM, N_HEADS,
             BRANCH_W, BRANCH_W, BRANCH_W,
             BRANCH_W, BRANCH_W, BRANCH_W, IDX_HEADS * IDX_DIM, IDX_DIM, IDX_HEADS,
             RWKV_IN, N_BRANCH * D_MODEL)
F32 = jnp.float32
BF16 = jnp.bfloat16

LANES = 128
SUBLANES = 8
VMEM_LIMIT = 56 * 1024 * 1024
NEG_BIG = -1e30
LOG2E = math.log2(math.e)
PROMPT_Q_SCALE = HEAD_DIM ** -0.5 * LOG2E
Q_TILE = 128
K_TILE = MOBA_BLOCK


def _cparams(*sem):
    return pltpu.CompilerParams(dimension_semantics=sem, vmem_limit_bytes=VMEM_LIMIT)


def _mm_kernel(x_ref, w_ref, o_ref):
    o_ref[...] = jnp.dot(x_ref[...], w_ref[...], preferred_element_type=F32)


def _pick_tile(n, cap, quantum):
    if n <= cap:
        return n
    best = None
    for t in range(quantum, cap + 1, quantum):
        if n % t == 0:
            best = t
    assert best is not None, (n, cap, quantum)
    return best


def matmul(x, w, *, tm_cap=1024, tn_cap=1408):
    m, k = x.shape
    k2, n = w.shape
    assert k == k2
    x = x.astype(BF16)
    w = w.astype(BF16)
    tm = _pick_tile(m, tm_cap, SUBLANES * 2)
    tn = _pick_tile(n, tn_cap, LANES)
    return pl.pallas_call(
        _mm_kernel,
        grid=(n // tn, m // tm),
        in_specs=[pl.BlockSpec((tm, k), lambda j, i: (i, 0)),
                  pl.BlockSpec((k, tn), lambda j, i: (0, j))],
        out_specs=pl.BlockSpec((tm, tn), lambda j, i: (i, j)),
        out_shape=jax.ShapeDtypeStruct((m, n), F32),
        compiler_params=_cparams("parallel", "parallel"),
        name="matmul",
    )(x, w)


def _norm_mm_kernel(x_ref, g_ref, w_ref, o_ref, xn_scr):
    @pl.when(pl.program_id(1) == 0)
    def _():
        x = x_ref[...]
        y = x * lax.rsqrt(jnp.mean(x * x, axis=-1, keepdims=True) + NORM_EPS) * g_ref[...]
        xn_scr[...] = y.astype(BF16)

    o_ref[...] = jnp.dot(xn_scr[...], w_ref[...], preferred_element_type=F32)


def norm_matmul(x, g, w, *, tm_cap=1024, tn_cap=1408):
    m, k = x.shape
    n = w.shape[1]
    tm = _pick_tile(m, tm_cap, SUBLANES * 2)
    tn = _pick_tile(n, tn_cap, LANES)
    return pl.pallas_call(
        _norm_mm_kernel,
        grid=(m // tm, n // tn),
        in_specs=[pl.BlockSpec((tm, k), lambda i, j: (i, 0)),
                  pl.BlockSpec((1, k), lambda i, j: (0, 0)),
                  pl.BlockSpec((k, tn), lambda i, j: (0, j))],
        out_specs=pl.BlockSpec((tm, tn), lambda i, j: (i, j)),
        out_shape=jax.ShapeDtypeStruct((m, n), F32),
        scratch_shapes=[pltpu.VMEM((tm, k), BF16)],
        compiler_params=_cparams("parallel", "arbitrary"),
        name="norm_matmul",
    )(x, g[None], w.astype(BF16))


def _rel_bucket(dist):
    n = jnp.maximum(dist, 0)
    max_exact = NUM_BUCKETS // 2
    nf = jnp.maximum(n, 1).astype(F32)
    large = max_exact + (jnp.log(nf / max_exact) / math.log(MAX_DISTANCE / max_exact)
                         * (NUM_BUCKETS - max_exact)).astype(jnp.int32)
    large = jnp.minimum(large, NUM_BUCKETS - 1)
    return jnp.where(n < max_exact, n, large)


N_NEAR = -(-(MAX_DISTANCE + K_TILE - 1) // Q_TILE)


def _bias_tiles(tab):
    o = jnp.arange(N_NEAR + 1, dtype=jnp.int32)[:, None, None] * Q_TILE
    j = jnp.arange(K_TILE, dtype=jnp.int32)[None, :, None]
    i = jnp.arange(Q_TILE, dtype=jnp.int32)[None, None, :]
    dist = jnp.where(o >= N_NEAR * Q_TILE, MAX_DISTANCE, o + i - j)
    return _bucket_lookup(tab, _rel_bucket(dist))


def _bucket_lookup(tab, bucket):
    hit = bucket[..., None] == jnp.arange(NUM_BUCKETS, dtype=jnp.int32)
    cols = tab.T.reshape((tab.shape[1],) + (1,) * bucket.ndim + (NUM_BUCKETS,))
    return jnp.sum(jnp.where(hit, cols, 0.0), axis=-1)


FLASH_GROUP = 4


class _FlashScratch:
    shapes = staticmethod(lambda: [
        pltpu.VMEM((FLASH_GROUP, N_HEADS, K_TILE, Q_TILE), F32),
        pltpu.VMEM((FLASH_GROUP, N_HEADS, K_TILE, Q_TILE), BF16),
        pltpu.VMEM((SUBLANES, Q_TILE), F32),
        pltpu.VMEM((SUBLANES, Q_TILE), F32),
        pltpu.VMEM((N_HEADS, HEAD_DIM, Q_TILE), F32)])

    def __init__(self, s, p, m, l, acc):
        self.s, self.p, self.m, self.l, self.acc = s, p, m, l, acc

    def init(self):
        self.m[...] = jnp.full(self.m.shape, NEG_BIG, F32)
        self.l[...] = jnp.zeros(self.l.shape, F32)
        self.acc[...] = jnp.zeros(self.acc.shape, F32)

    def write(self, o_ref):
        for h in range(N_HEADS):
            o_ref[h * HEAD_DIM:(h + 1) * HEAD_DIM, :] = self.acc[h] / self.l[h:h + 1, :]


REDUCE_CHAINS = 8


def _fold_rows(x, op):
    r, c = x.shape
    x = x.reshape(REDUCE_CHAINS, r // (REDUCE_CHAINS * SUBLANES), SUBLANES, c)
    return op(op(x, axis=1), axis=0)


def _flash_group(k_ref, vt_ref, bias_ref, qpair, tiles, fs):
    for u, (k0, _, _, _) in enumerate(tiles):
        for p in range(N_HEADS // 2):
            kt = k_ref[pl.ds(k0, K_TILE), p * LANES:(p + 1) * LANES]
            s2 = jnp.dot(kt, qpair[p], preferred_element_type=F32)
            fs.s[u, 2 * p] = s2[:, :Q_TILE]
            fs.s[u, 2 * p + 1] = s2[:, Q_TILE:]
    for u, (k0, off_idx, mask_tile, mask_rows) in enumerate(tiles):
        for h in range(N_HEADS):
            s = fs.s[u, h] + bias_ref[h, off_idx]
            if mask_tile is not None:
                s = s + mask_tile
            m_old = fs.m[h:h + 1, :]
            m_tile = jnp.max(_fold_rows(s, jnp.max), axis=0, keepdims=True)
            if mask_rows is not None:
                m_tile = m_tile + mask_rows[h]
            m_new = jnp.maximum(m_old, m_tile)
            shift = m_new if mask_rows is None else m_new - mask_rows[h]
            p = jnp.exp2(s - shift)
            alpha = jnp.exp2(m_old - m_new)
            fs.l[h:h + 1, :] = alpha * fs.l[h:h + 1, :] + jnp.sum(_fold_rows(p, jnp.sum), axis=0, keepdims=True)
            fs.m[h:h + 1, :] = m_new
            fs.p[u, h] = p.astype(BF16)
            vt = vt_ref[h * HEAD_DIM:(h + 1) * HEAD_DIM, pl.ds(k0, K_TILE)]
            fs.acc[h] = alpha * fs.acc[h] + jnp.dot(vt, fs.p[u, h], preferred_element_type=F32)


def _query_pairs(q_ref):
    return [jnp.concatenate([q_ref[2 * p].astype(BF16), q_ref[2 * p + 1].astype(BF16)], axis=1)
            for p in range(N_HEADS // 2)]


def _causal_mask(q0, k0):
    row = lax.broadcasted_iota(jnp.int32, (K_TILE, Q_TILE), 0)
    lane = lax.broadcasted_iota(jnp.int32, (K_TILE, Q_TILE), 1)
    return jnp.where(k0 + row <= q0 + lane, 0.0, NEG_BIG).astype(F32)


def _top_blocks_mask(score, n_valid, n_sel, axis):
    n = score.shape[axis]
    idx = lax.broadcasted_iota(jnp.int32, score.shape, axis)
    low = jnp.float32(-3.0e38)
    s = jnp.where(idx < n_valid, score, low)
    mask = jnp.full(score.shape, NEG_BIG, F32)
    for _ in range(n_sel):
        mx = jnp.max(s, axis=axis, keepdims=True)
        first = jnp.min(jnp.where(s == mx, idx, n), axis=axis, keepdims=True)
        pick = idx == jnp.where(mx > low, first, -1)
        mask = jnp.where(pick, 0.0, mask)
        s = jnp.where(pick, low, s)
    return mask


def _moba_prompt_kernel(q_ref, k_ref, vt_ref, kmean_ref, bias_ref, o_ref, sel_scr, *flash_scr):
    qi = pl.program_id(1)
    q0 = qi * Q_TILE
    own = q0 // K_TILE
    nb = kmean_ref.shape[1]
    fs = _FlashScratch(*flash_scr)
    fs.init()
    for h in range(N_HEADS):
        score = jnp.dot(kmean_ref[h], q_ref[h], preferred_element_type=F32,
                        precision=lax.Precision.HIGHEST)
        sel_scr[h] = _top_blocks_mask(score, own, min(MOBA_TOPK, nb), 0)
    qpair = _query_pairs(q_ref)
    k_own = pl.multiple_of(own * K_TILE, K_TILE)
    _flash_group(k_ref, vt_ref, bias_ref, qpair,
                 [(k_own, (q0 - k_own) // Q_TILE, _causal_mask(q0, k_own), None)], fs)

    def body(g, _):
        tiles = []
        for u in range(FLASH_GROUP):
            kj = g * FLASH_GROUP + u
            k0 = pl.multiple_of(kj * K_TILE, K_TILE)
            off = jnp.clip((q0 - k0) // Q_TILE, 0, N_NEAR)
            tiles.append((k0, off, None, [sel_scr[h, pl.ds(kj, 1), :] for h in range(N_HEADS)]))
        _flash_group(k_ref, vt_ref, bias_ref, qpair, tiles, fs)
        return 0

    lax.fori_loop(0, (own + FLASH_GROUP - 1) // FLASH_GROUP, body, 0)
    fs.write(o_ref)


def _pad_heads_t(q):
    b, t, _ = q.shape
    qt = jnp.transpose(q.reshape(b, t, N_HEADS, HEAD_DIM), (0, 2, 3, 1))
    z = jnp.zeros_like(qt)
    even = jnp.concatenate([qt, z], axis=2)
    odd = jnp.concatenate([z, qt], axis=2)
    is_even = (jnp.arange(N_HEADS) % 2 == 0)[None, :, None, None]
    return jnp.where(is_even, even, odd)


def _block_mean_kernel(k_ref, o_ref):
    o_ref[...] = jnp.mean(k_ref[...], axis=0, keepdims=True)


def block_mean(k):
    b, t, c = k.shape
    nb = t // MOBA_BLOCK
    out = pl.pallas_call(
        _block_mean_kernel,
        grid=(b, nb),
        in_specs=[pl.BlockSpec((None, MOBA_BLOCK, c), lambda i, j: (i, j, 0))],
        out_specs=pl.BlockSpec((None, None, 1, c), lambda i, j: (i, j, 0, 0)),
        out_shape=jax.ShapeDtypeStruct((b, nb, 1, c), F32),
        compiler_params=_cparams("parallel", "parallel"),
        name="block_mean",
    )(k)
    return out.reshape(b, nb, c)


def moba_prompt(q, k, v, bias_tiles):
    b, t, w = q.shape
    assert t % (FLASH_GROUP * K_TILE) == 0 and K_TILE % Q_TILE == 0
    nb = t // MOBA_BLOCK
    qp = _pad_heads_t(q * PROMPT_Q_SCALE)
    kmean = block_mean(k)
    kmp = jnp.transpose(_pad_heads_t(kmean), (0, 1, 3, 2))
    kb = k.astype(BF16)
    vtb = jnp.transpose(v, (0, 2, 1)).astype(BF16)
    ot = pl.pallas_call(
        _moba_prompt_kernel,
        grid=(b, t // Q_TILE),
        in_specs=[pl.BlockSpec((None, N_HEADS, LANES, Q_TILE), lambda i, j: (i, 0, 0, j)),
                  pl.BlockSpec((None, t, w), lambda i, j: (i, 0, 0)),
                  pl.BlockSpec((None, w, t), lambda i, j: (i, 0, 0)),
                  pl.BlockSpec((None, N_HEADS, nb, LANES), lambda i, j: (i, 0, 0, 0)),
                  pl.BlockSpec(bias_tiles.shape, lambda i, j: (0, 0, 0, 0))],
        out_specs=pl.BlockSpec((None, w, Q_TILE), lambda i, j: (i, 0, j)),
        out_shape=jax.ShapeDtypeStruct((b, w, t), F32),
        scratch_shapes=[pltpu.VMEM((N_HEADS, nb, Q_TILE), F32)] + _FlashScratch.shapes(),
        compiler_params=_cparams("parallel", "arbitrary"),
        name="moba_prompt",
    )(qp, kb, vtb, kmp, bias_tiles)
    return jnp.transpose(ot, (0, 2, 1))


def _split_bf16(x):
    hi = x.astype(BF16)
    lo = (x - hi.astype(F32)).astype(BF16)
    return hi, lo


def _ordered_key(x):
    x = jnp.where(x == 0.0, 0.0, x)
    bits = lax.bitcast_convert_type(x, jnp.int32)
    return bits ^ ((bits >> 31) & jnp.int32(0x7FFFFFFF))


INT_MIN = -2 ** 31


def _kth_largest_key(count_ge, kcount):
    int_min = jnp.int32(INT_MIN)
    thr = jnp.where(count_ge(jnp.zeros_like(kcount, jnp.int32)) >= kcount, jnp.int32(0), int_min)

    def bit_body(i, thr):
        cand = thr | lax.shift_left(jnp.int32(1), 30 - i)
        return jnp.where(count_ge(cand) >= kcount, cand, thr)

    return lax.fori_loop(0, 31, bit_body, thr)


RADIX_BITS = 4


def _kth_largest_key_radix(count_ge, kcount):
    thr = jnp.full(kcount.shape, INT_MIN, jnp.int32)
    for shift in range(32 - RADIX_BITS, -1, -RADIX_BITS):
        digit = jnp.zeros(kcount.shape, jnp.int32)
        for j in range(1, 2 ** RADIX_BITS):
            cand = thr + jnp.int32(np.array(j << shift, np.uint32).astype(np.int32))
            digit = digit + jnp.where(count_ge(cand) >= kcount, 1, 0)
        thr = thr + lax.shift_left(digit, shift)
    return thr


def _dsa_prompt_kernel(q_ref, k_ref, vt_ref, kic_ref, qic_ref, wi_ref, bias_ref, o_ref,
                       key_scr, mask_scr, *flash_scr, n_keep):
    qi = pl.program_id(1)
    q0 = qi * Q_TILE
    own = q0 // K_TILE
    n_tiles = own + 1
    qpos = q0 + lax.broadcasted_iota(jnp.int32, (1, Q_TILE), 1)
    row = lax.broadcasted_iota(jnp.int32, (K_TILE, Q_TILE), 0)
    fs = _FlashScratch(*flash_scr)
    fs.init()

    def tile_start(kj):
        return pl.multiple_of(kj * K_TILE, K_TILE)

    n_pairs = (n_tiles + 1) // 2

    def score_body(kp, _):
        for u in range(2):
            k0 = tile_start(2 * kp + u)
            kic = kic_ref[pl.ds(k0, K_TILE), :]
            sc = jnp.zeros((K_TILE, Q_TILE), F32)
            for j in range(IDX_HEADS // 2):
                q2 = jnp.concatenate([qic_ref[2 * j], qic_ref[2 * j + 1]], axis=1)
                d = jnp.maximum(jnp.dot(kic, q2, preferred_element_type=F32), 0.0)
                sc = sc + wi_ref[2 * j:2 * j + 1, :] * d[:, :Q_TILE]
                sc = sc + wi_ref[2 * j + 1:2 * j + 2, :] * d[:, Q_TILE:]
            key = jnp.where(k0 + row <= qpos, _ordered_key(sc), jnp.int32(INT_MIN))
            key_scr[pl.ds(k0, K_TILE), :] = key
        return 0

    lax.fori_loop(0, n_pairs, score_body, 0)

    def count(pred):
        def body(kp, c):
            t = key_scr[pl.ds(pl.multiple_of(kp * 2 * K_TILE, 2 * K_TILE), 2 * K_TILE), :]
            return c + _fold_rows(jnp.where(pred(t), 1.0, 0.0), jnp.sum)
        c = lax.fori_loop(0, n_pairs, body, jnp.zeros((SUBLANES, Q_TILE), F32))
        return jnp.sum(c, axis=0, keepdims=True)

    kcount = jnp.minimum(n_keep, qpos + 1).astype(F32)
    thr = _kth_largest_key(lambda c: count(lambda t: t >= c), kcount)
    need = kcount - count(lambda t: t > thr)
    tri = jnp.where(lax.broadcasted_iota(jnp.int32, (K_TILE, K_TILE), 0)
                    >= lax.broadcasted_iota(jnp.int32, (K_TILE, K_TILE), 1), 1.0, 0.0).astype(BF16)

    def mask_body(kp, seen):
        for u in range(2):
            k0 = tile_start(2 * kp + u)
            t = key_scr[pl.ds(k0, K_TILE), :]
            eq = jnp.where(t == thr, 1.0, 0.0)
            rank = jnp.dot(tri, eq.astype(BF16), preferred_element_type=F32) + seen
            keep = jnp.where(t > thr, 1.0, jnp.where(rank <= need, eq, 0.0))
            mask_scr[pl.ds(k0, K_TILE), :] = jnp.where(keep > 0.5, 0.0, NEG_BIG).astype(BF16)
            seen = seen + jnp.sum(_fold_rows(eq, jnp.sum), axis=0, keepdims=True)
        return seen

    lax.fori_loop(0, n_pairs, mask_body, jnp.zeros((1, Q_TILE), F32))
    qpair = _query_pairs(q_ref)

    @pl.when(n_pairs % 2 == 1)
    def _():
        mask_scr[pl.ds(tile_start(2 * n_pairs), 2 * K_TILE), :] = jnp.full((2 * K_TILE, Q_TILE), NEG_BIG, BF16)

    def body(g, _):
        tiles = []
        for u in range(FLASH_GROUP):
            k0 = tile_start(g * FLASH_GROUP + u)
            off = jnp.clip((q0 - k0) // Q_TILE, 0, N_NEAR)
            tiles.append((k0, off, mask_scr[pl.ds(k0, K_TILE), :].astype(F32), None))
        _flash_group(k_ref, vt_ref, bias_ref, qpair, tiles, fs)
        return 0

    lax.fori_loop(0, (n_pairs + 1) // 2, body, 0)
    fs.write(o_ref)


def dsa_prompt(q, k, v, qi, ki, wi, bias_tiles):
    b, t, w = q.shape
    assert t % (FLASH_GROUP * K_TILE) == 0 and K_TILE % Q_TILE == 0 and FLASH_GROUP == 4
    n_keep = min(DSA_TOPK, t // 4)
    qp = _pad_heads_t(q * PROMPT_Q_SCALE)
    kb = k.astype(BF16)
    vtb = jnp.transpose(v, (0, 2, 1)).astype(BF16)
    ki_hi, ki_lo = _split_bf16(ki)
    kic = jnp.concatenate([ki_hi, ki_hi, ki_lo], axis=-1)
    qis = jnp.transpose((qi * IDX_DIM ** -0.5).reshape(b, t, IDX_HEADS, IDX_DIM), (0, 2, 3, 1))
    qi_hi, qi_lo = _split_bf16(qis)
    qic = jnp.concatenate([qi_hi, qi_lo, qi_hi], axis=2)
    wit = jnp.transpose(wi * IDX_HEADS ** -0.5, (0, 2, 1))
    ot = pl.pallas_call(
        functools.partial(_dsa_prompt_kernel, n_keep=n_keep),
        grid=(b, t // Q_TILE),
        in_specs=[pl.BlockSpec((None, N_HEADS, LANES, Q_TILE), lambda i, j: (i, 0, 0, j)),
                  pl.BlockSpec((None, t, w), lambda i, j: (i, 0, 0)),
                  pl.BlockSpec((None, w, t), lambda i, j: (i, 0, 0)),
                  pl.BlockSpec((None, t, 3 * IDX_DIM), lambda i, j: (i, 0, 0)),
                  pl.BlockSpec((None, IDX_HEADS, 3 * IDX_DIM, Q_TILE), lambda i, j: (i, 0, 0, j)),
                  pl.BlockSpec((None, IDX_HEADS, Q_TILE), lambda i, j: (i, 0, j)),
                  pl.BlockSpec(bias_tiles.shape, lambda i, j: (0, 0, 0, 0))],
        out_specs=pl.BlockSpec((None, w, Q_TILE), lambda i, j: (i, 0, j)),
        out_shape=jax.ShapeDtypeStruct((b, w, t), F32),
        scratch_shapes=[pltpu.VMEM((t, Q_TILE), jnp.int32), pltpu.VMEM((t, Q_TILE), BF16)]
        + _FlashScratch.shapes(),
        compiler_params=_cparams("parallel", "arbitrary"),
        name="dsa_prompt",
    )(qp, kb, vtb, kic, qic, wit, bias_tiles)
    return jnp.transpose(ot, (0, 2, 1))


HEADS_PER_GROUP = N_HEADS // SSM_GROUPS
PAIR_W = HEADS_PER_GROUP * HEAD_DIM
assert PAIR_W == LANES and SSM_DSTATE == LANES and SSM_CHUNK == LANES


def _ssd_prompt_kernel(xdt_ref, b_ref, c_ref, dax_ref, dat_ref, h0_ref, y_ref, hout_ref, st_scr):
    ci = pl.program_id(1)

    @pl.when(ci == 0)
    def _():
        st_scr[...] = h0_ref[...]

    q = SSM_CHUNK
    row = lax.broadcasted_iota(jnp.int32, (q, q), 0)
    col = lax.broadcasted_iota(jnp.int32, (q, q), 1)
    lower = row >= col
    hi = lax.Precision.HIGHEST
    cs_x = jnp.dot(jnp.where(lower, 1.0, 0.0), dax_ref[...], preferred_element_type=F32, precision=hi)
    cs_t = jnp.dot(dat_ref[...], jnp.where(row <= col, 1.0, 0.0), preferred_element_type=F32, precision=hi)
    lane_lo = col < HEAD_DIM
    for g in range(SSM_GROUPS):
        sl = slice(g * PAIR_W, (g + 1) * PAIR_W)
        cg = c_ref[:, g * SSM_DSTATE:(g + 1) * SSM_DSTATE]
        bg = b_ref[:, g * SSM_DSTATE:(g + 1) * SSM_DSTATE]
        xdt = xdt_ref[:, sl]
        cs_pair = cs_x[:, sl]
        cs_swap = pltpu.roll(cs_pair, HEAD_DIM, 1)
        cb = lax.dot_general(cg, bg, (((1,), (1,)), ((), ())), preferred_element_type=F32)
        y_pair = []
        for a in range(HEADS_PER_GROUP):
            h = g * HEADS_PER_GROUP + a
            own_half = lane_lo if a == 0 else jnp.logical_not(lane_lo)
            cs_col = jnp.where(own_half, cs_pair, cs_swap)
            seg = jnp.exp(jnp.minimum(cs_col - cs_t[h:h + 1, :], 0.0))
            scores = cb * jnp.where(lower, seg, 0.0)
            y_pair.append(jnp.dot(scores, xdt, preferred_element_type=F32))
        y_diag = jnp.where(lane_lo, y_pair[0], y_pair[1])
        st = st_scr[sl, :]
        y_off = lax.dot_general(cg, st, (((1,), (1,)), ((), ())), preferred_element_type=F32)
        y_ref[:, sl] = y_diag + y_off * jnp.exp(cs_pair)
        cs_last = cs_pair[q - 1:q, :]
        xdtd = xdt * jnp.exp(cs_last - cs_pair)
        upd = jnp.dot(xdtd.T, bg, preferred_element_type=F32)
        last_a = cs_last[:, 0:1]
        last_b = cs_last[:, HEAD_DIM:HEAD_DIM + 1]
        st_scr[sl, :] = st * jnp.exp(jnp.where(row < HEAD_DIM, last_a, last_b)) + upd

    @pl.when(ci == pl.num_programs(1) - 1)
    def _():
        hout_ref[...] = st_scr[...]


def ssd_prompt(xs, dt, a_neg, bm, cm, h0):
    b, t, w = xs.shape
    assert t % SSM_CHUNK == 0
    rep = lambda z: jnp.repeat(z, HEAD_DIM, axis=-1)
    xdt = xs * rep(dt)
    da = dt * a_neg
    dat = jnp.pad(jnp.transpose(da, (0, 2, 1)), ((0, 0), (0, SUBLANES - N_HEADS), (0, 0)))
    hp = N_HEADS * HEAD_DIM
    chunk = lambda width: pl.BlockSpec((None, SSM_CHUNK, width), lambda i, j: (i, j, 0))
    state = pl.BlockSpec((None, hp, SSM_DSTATE), lambda i, j: (i, 0, 0))
    y, h = pl.pallas_call(
        _ssd_prompt_kernel,
        grid=(b, t // SSM_CHUNK),
        in_specs=[chunk(w), chunk(bm.shape[-1]), chunk(cm.shape[-1]), chunk(w),
                  pl.BlockSpec((None, SUBLANES, SSM_CHUNK), lambda i, j: (i, 0, j)), state],
        out_specs=[chunk(w), state],
        out_shape=[jax.ShapeDtypeStruct((b, t, w), F32), jax.ShapeDtypeStruct((b, hp, SSM_DSTATE), F32)],
        scratch_shapes=[pltpu.VMEM((hp, SSM_DSTATE), F32)],
        compiler_params=_cparams("parallel", "arbitrary"),
        name="ssd_prompt",
    )(xdt, bm, cm, rep(da), dat, h0.reshape(b, hp, SSM_DSTATE))
    return y, h.reshape(b, N_HEADS, HEAD_DIM, SSM_DSTATE)


RWKV_CHUNK = 128
HEAD_PAIRS = N_HEADS // 2


def _split_cat(x):
    hi, lo = _split_bf16(x)
    return jnp.concatenate([hi, lo], axis=1)


def _rwkv_prompt_kernel(r_ref, w_ref, k_ref, kk_ref, kka_ref, v_ref, g_ref, c1_ref, c2_ref, s0_ref,
                        o_ref, sout_ref, s_scr, z_scr, vt_scr, ot_scr):
    ci = pl.program_id(0)
    n_b = r_ref.shape[0]
    tiles = [(b, p) for b in range(n_b) for p in range(HEAD_PAIRS)]
    lanes = lambda p: slice(p * LANES, (p + 1) * LANES)

    r2 = lax.broadcasted_iota(jnp.int32, (2 * LANES, LANES), 0)
    c2 = lax.broadcasted_iota(jnp.int32, (2 * LANES, LANES), 1)
    ones_same = jnp.where((r2 % LANES) // HEAD_DIM == c2 // HEAD_DIM, 1.0, 0.0).astype(BF16)
    r3 = lax.broadcasted_iota(jnp.int32, (LANES, 2 * LANES), 0)
    c3 = lax.broadcasted_iota(jnp.int32, (LANES, 2 * LANES), 1)
    ones_out = jnp.where(r3 // HEAD_DIM == c3 // LANES, 1.0, 0.0).astype(BF16)

    def key_sums(xs):
        out = jnp.dot(jnp.concatenate([_split_cat(x) for x in xs], axis=0), ones_same, preferred_element_type=F32)
        return [out[i * HEAD_DIM:(i + 1) * HEAD_DIM] for i in range(len(xs))]

    @pl.when(ci == 0)
    def _():
        s_scr[...] = s0_ref[...]
        z0 = key_sums([s0_ref[i] * kk_ref[b, 0:1, lanes(p)] for i, (b, p) in enumerate(tiles)])
        for i in range(len(tiles)):
            z_scr[i] = z0[i]

    for i, (b, p) in enumerate(tiles):
        vt_scr[i] = v_ref[b, :, lanes(p)].T
    lane_t = lax.broadcasted_iota(jnp.int32, (1, RWKV_CHUNK), 1)
    head_a = lax.broadcasted_iota(jnp.int32, (HEAD_DIM, LANES), 1) < HEAD_DIM

    def step(t, row):
        at_t = lane_t == t
        states = [s_scr[i] for i in range(len(tiles))]
        ahead = key_sums([s * row(g_ref, b, p) for s, (b, p) in zip(states, tiles)])
        new_states = []
        for i, (b, p) in enumerate(tiles):
            z = z_scr[i]
            v_a = jnp.sum(jnp.where(at_t, vt_scr[i, :HEAD_DIM, :], 0.0), axis=1, keepdims=True)
            v_b = jnp.sum(jnp.where(at_t, vt_scr[i, HEAD_DIM:, :], 0.0), axis=1, keepdims=True)
            v_bc = jnp.where(head_a, v_a, v_b)
            s = states[i] * row(w_ref, b, p) - z * row(kka_ref, b, p) + v_bc * row(k_ref, b, p)
            s_scr[i] = s
            new_states.append(s)
            z_scr[i] = ahead[i] - z * row(c1_ref, b, p) + v_bc * row(c2_ref, b, p)
        y = jnp.concatenate([(s * row(r_ref, b, p)).astype(BF16) for s, (b, p) in zip(new_states, tiles)], axis=0)
        o_all = jnp.dot(y, ones_out, preferred_element_type=F32)
        col_t = jnp.broadcast_to(at_t, (HEAD_DIM, RWKV_CHUNK))
        for i in range(len(tiles)):
            o = o_all[i * HEAD_DIM:(i + 1) * HEAD_DIM]
            pltpu.store(ot_scr.at[i, pl.ds(0, HEAD_DIM), :], o[:, :LANES], mask=col_t)
            pltpu.store(ot_scr.at[i, pl.ds(HEAD_DIM, HEAD_DIM), :], o[:, LANES:], mask=col_t)

    def step_group(g, _):
        base = pl.multiple_of(g * SUBLANES, SUBLANES)
        refs = (r_ref, w_ref, k_ref, kka_ref, g_ref, c1_ref, c2_ref)
        blocks = {(id(ref), b, p): ref[b, pl.ds(base, SUBLANES), lanes(p)] for ref in refs for b, p in tiles}
        for j in range(SUBLANES):
            step(base + j, lambda ref, b, p, j=j: blocks[(id(ref), b, p)][j:j + 1, :])
        return 0

    lax.fori_loop(0, RWKV_CHUNK // SUBLANES, step_group, 0)
    for i, (b, p) in enumerate(tiles):
        o_ref[b, :, lanes(p)] = ot_scr[i].T

    @pl.when(ci == pl.num_programs(0) - 1)
    def _():
        sout_ref[...] = s_scr[...]


def rwkv_prompt(r, w, k, v, kk, kka, s0):
    b, t, hw = r.shape
    assert t % RWKV_CHUNK == 0 and RWKV_CHUNK == LANES and 2 * HEAD_DIM == LANES
    n_tiles = b * HEAD_PAIRS
    pack = lambda s: jnp.transpose(s.reshape(b, HEAD_PAIRS, 2, HEAD_DIM, HEAD_DIM), (0, 1, 3, 2, 4)).reshape(
        n_tiles, HEAD_DIM, LANES)
    unpack = lambda s: jnp.transpose(s.reshape(b, HEAD_PAIRS, HEAD_DIM, 2, HEAD_DIM), (0, 1, 3, 2, 4)).reshape(
        b, N_HEADS, HEAD_DIM, HEAD_DIM)
    kk_next = jnp.concatenate([kk[:, 1:], jnp.zeros_like(kk[:, :1])], axis=1)
    head_dot = lambda x: jnp.repeat(jnp.sum((x * kk_next).reshape(b, t, N_HEADS, HEAD_DIM), axis=-1), HEAD_DIM, axis=-1)
    seq_spec = pl.BlockSpec((b, RWKV_CHUNK, hw), lambda c: (0, c, 0))
    st_spec = pl.BlockSpec((n_tiles, HEAD_DIM, LANES), lambda c: (0, 0, 0))
    o, s = pl.pallas_call(
        _rwkv_prompt_kernel,
        grid=(t // RWKV_CHUNK,),
        in_specs=[seq_spec] * 9 + [st_spec],
        out_specs=[seq_spec, st_spec],
        out_shape=[jax.ShapeDtypeStruct((b, t, hw), F32), jax.ShapeDtypeStruct((n_tiles, HEAD_DIM, LANES), F32)],
        scratch_shapes=[pltpu.VMEM((n_tiles, HEAD_DIM, LANES), F32),
                        pltpu.VMEM((n_tiles, HEAD_DIM, LANES), F32),
                        pltpu.VMEM((n_tiles, LANES, RWKV_CHUNK), F32),
                        pltpu.VMEM((n_tiles, LANES, RWKV_CHUNK), F32)],
        compiler_params=_cparams("arbitrary"),
        name="rwkv_prompt",
    )(r, w, k, kk, kka, v, w * kk_next, head_dot(kka), head_dot(k), pack(s0))
    return o, unpack(s)


STEP_SEQS = 8
NT_DIMS = (((1,), (1,)), ((), ()))


def _row_dot(row, mat):
    r8 = jnp.broadcast_to(row, (SUBLANES, row.shape[1]))
    return lax.dot_general(r8, mat, NT_DIMS, preferred_element_type=F32,
                           precision=lax.Precision.HIGHEST)[0:1, :]


def _ssd_step_kernel(h0_ref, xdtb_ref, dab_ref, b_ref, c_ref, h_ref, y_ref):
    for s in range(h0_ref.shape[0]):
        for h in range(N_HEADS):
            hn = h0_ref[s, h] * jnp.exp(dab_ref[s, h]) + xdtb_ref[s, h] * b_ref[s, h]
            h_ref[s, h] = hn
            y_ref[s, h] = _row_dot(c_ref[s, h], hn)


def ssd_step(xs, dt, a_neg, bm, cm, h0):
    s = xs.shape[0]
    assert s % STEP_SEQS == 0
    xdtb = jnp.broadcast_to((xs.reshape(s, N_HEADS, HEAD_DIM) * dt[:, :, None])[..., None],
                            (s, N_HEADS, HEAD_DIM, SSM_DSTATE))
    dab = jnp.broadcast_to((dt * a_neg)[:, :, None, None], (s, N_HEADS, 1, SSM_DSTATE))
    per_head = lambda z: jnp.repeat(z.reshape(s, SSM_GROUPS, 1, SSM_DSTATE), HEADS_PER_GROUP, axis=1)
    big = pl.BlockSpec((STEP_SEQS, N_HEADS, HEAD_DIM, SSM_DSTATE), lambda i: (i, 0, 0, 0))
    row = pl.BlockSpec((STEP_SEQS, N_HEADS, 1, SSM_DSTATE), lambda i: (i, 0, 0, 0))
    h, y = pl.pallas_call(
        _ssd_step_kernel,
        grid=(s // STEP_SEQS,),
        in_specs=[big, big, row, row, row],
        out_specs=[big, pl.BlockSpec((STEP_SEQS, N_HEADS, 1, HEAD_DIM), lambda i: (i, 0, 0, 0))],
        out_shape=[jax.ShapeDtypeStruct(h0.shape, F32), jax.ShapeDtypeStruct((s, N_HEADS, 1, HEAD_DIM), F32)],
        compiler_params=_cparams("parallel"),
        name="ssd_step",
    )(h0, xdtb, dab, per_head(bm), per_head(cm))
    return y.reshape(s, N_HEADS * HEAD_DIM), h


def _rwkv_step_kernel(s0_ref, vb_ref, w_ref, kk_ref, kka_ref, k_ref, r_ref, s_ref, o_ref):
    for s in range(s0_ref.shape[0]):
        for h in range(N_HEADS):
            st = s0_ref[s, h]
            sa = -jnp.sum(st * kk_ref[s, h], axis=1, keepdims=True)
            st = st * w_ref[s, h] + sa * kka_ref[s, h] + vb_ref[s, h] * k_ref[s, h]
            s_ref[s, h] = st
            o_ref[s, h] = _row_dot(r_ref[s, h], st)


def rwkv_step(r, w, k, v, kk, kka, s0):
    s = r.shape[0]
    assert s % STEP_SEQS == 0
    rows = lambda z: z.reshape(s, N_HEADS, 1, HEAD_DIM)
    vb = jnp.broadcast_to(v.reshape(s, N_HEADS, HEAD_DIM, 1), (s, N_HEADS, HEAD_DIM, HEAD_DIM))
    big = pl.BlockSpec((STEP_SEQS, N_HEADS, HEAD_DIM, HEAD_DIM), lambda i: (i, 0, 0, 0))
    row = pl.BlockSpec((STEP_SEQS, N_HEADS, 1, HEAD_DIM), lambda i: (i, 0, 0, 0))
    st, o = pl.pallas_call(
        _rwkv_step_kernel,
        grid=(s // STEP_SEQS,),
        in_specs=[big, big, row, row, row, row, row],
        out_specs=[big, row],
        out_shape=[jax.ShapeDtypeStruct(s0.shape, F32), jax.ShapeDtypeStruct((s, N_HEADS, 1, HEAD_DIM), F32)],
        compiler_params=_cparams("parallel"),
        name="rwkv_step",
    )(s0, vb, rows(w), rows(kk), rows(kka), rows(k), rows(r))
    return o.reshape(s, N_HEADS * HEAD_DIM), st


def _head_rows_to_row(o):
    r = lax.broadcasted_iota(jnp.int32, o.shape, 0)
    c = lax.broadcasted_iota(jnp.int32, o.shape, 1)
    return jnp.sum(jnp.where(c // HEAD_DIM == r, o, 0.0), axis=0, keepdims=True)


def _sample_softmax_pv(logit_pages, s_new, vt_pages, vnew):
    m = s_new
    for s in logit_pages:
        m = jnp.maximum(m, jnp.max(s, axis=1, keepdims=True))
    p_new = jnp.exp(s_new - m)
    l = p_new
    acc = p_new * vnew
    for s, vt_ref in zip(logit_pages, vt_pages):
        p = jnp.exp(s - m)
        l = l + jnp.sum(p, axis=1, keepdims=True)
        acc = acc + lax.dot_general(p.astype(BF16), vt_ref[...].astype(BF16), NT_DIMS, preferred_element_type=F32)
    return _head_rows_to_row(acc / l)


def _moba_sample_kernel(pt_ref, qbd_ref, knew_ref, vnew_ref, bias_ref, biasn_ref, *rest, n_pages, page):
    del pt_ref
    kt_pages, vt_pages, o_ref = rest[:n_pages], rest[n_pages:2 * n_pages], rest[2 * n_pages]
    qbd = qbd_ref[...]
    q_hi, q_lo = _split_bf16(qbd)
    q_both = jnp.concatenate([q_hi, q_lo], axis=0)
    ppb = MOBA_BLOCK // page
    n_blocks = n_pages // ppb
    lane = lax.broadcasted_iota(jnp.int32, (SUBLANES, LANES), 1)
    raw = []
    score = jnp.zeros((SUBLANES, LANES), F32)
    for p in range(n_pages):
        k_hi, k_lo = _split_bf16(kt_pages[p][...])
        r = jnp.dot(q_both, k_hi, preferred_element_type=F32)
        r = r[:SUBLANES] + r[SUBLANES:] + jnp.dot(q_hi, k_lo, preferred_element_type=F32)
        raw.append(r)
        score = score + jnp.where(lane == p // ppb, jnp.sum(r, axis=1, keepdims=True) * (1.0 / MOBA_BLOCK), 0.0)
    sel = _top_blocks_mask(score, n_blocks, min(MOBA_TOPK, n_blocks + 1), 1)
    logits = [raw[p] + bias_ref[:, p * page:(p + 1) * page] + sel[:, p // ppb:p // ppb + 1]
              for p in range(n_pages)]
    s_new = jnp.sum(qbd * knew_ref[...], axis=1, keepdims=True) + biasn_ref[:, 0:1]
    o_ref[...] = _sample_softmax_pv(logits, s_new, vt_pages, vnew_ref[...])


def _block_diag_q(q):
    s = q.shape[0]
    eye = jnp.eye(N_HEADS, dtype=q.dtype)
    bd = (q.reshape(s, 1, N_HEADS, HEAD_DIM) * eye[None, :, :, None]).reshape(s, N_HEADS, N_HEADS * HEAD_DIM)
    return jnp.pad(bd, ((0, 0), (0, SUBLANES - N_HEADS), (0, 0)))


def _sample_bias(tab, past_len):
    dist = past_len - jnp.arange(past_len, dtype=jnp.int32)
    pad = ((0, SUBLANES - N_HEADS), (0, 0))
    past = jnp.pad(_bucket_lookup(tab, _rel_bucket(dist)), pad)
    new = jnp.pad(_bucket_lookup(tab, _rel_bucket(jnp.zeros((LANES,), jnp.int32))), pad)
    return past, new


def _page_specs(layer, n_pages, rows, page):
    return [pl.BlockSpec((None, None, rows, page), functools.partial(
        lambda s, pt, p: (layer, pt[s, p], 0, 0), p=p)) for p in range(n_pages)]


def _transposed_pages(cache):
    c = cache.reshape(cache.shape[:3] + (-1,))
    return jnp.transpose(c, (0, 1, 3, 2))


def moba_sample(q, knew, vnew, cache_kt, cache_vt, layer, page_table, tab):
    s, w = q.shape
    n_pages, page = page_table.shape[1], cache_kt.shape[3]
    past_len = n_pages * page
    assert past_len % MOBA_BLOCK == 0 and MOBA_BLOCK % page == 0
    bias_p, bias_n = _sample_bias(tab, past_len)
    row = pl.BlockSpec((None, 1, w), lambda i, pt: (i, 0, 0))
    const = lambda shape: pl.BlockSpec(shape, lambda i, pt: (0, 0))
    out = pl.pallas_call(
        functools.partial(_moba_sample_kernel, n_pages=n_pages, page=page),
        grid_spec=pltpu.PrefetchScalarGridSpec(
            num_scalar_prefetch=1, grid=(s,),
            in_specs=[pl.BlockSpec((None, SUBLANES, w), lambda i, pt: (i, 0, 0)), row, row,
                      const(bias_p.shape), const(bias_n.shape)]
            + _page_specs(layer, n_pages, w, page) * 2,
            out_specs=row),
        out_shape=jax.ShapeDtypeStruct((s, 1, w), F32),
        compiler_params=_cparams("arbitrary"),
        name="moba_sample",
    )(page_table, _block_diag_q(q * HEAD_DIM ** -0.5), knew[:, None], vnew[:, None], bias_p, bias_n,
      *([cache_kt] * n_pages), *([cache_vt] * n_pages))
    return out.reshape(s, w)


def _dsa_sample_kernel(pt_ref, qbd_ref, knew_ref, vnew_ref, bias_ref, biasn_ref, qi_ref, wi_ref, kin_ref,
                       *rest, n_pages, page, n_keep):
    del pt_ref
    kt_pages, vt_pages = rest[:n_pages], rest[n_pages:2 * n_pages]
    kit_pages, o_ref = rest[2 * n_pages:3 * n_pages], rest[3 * n_pages]
    qi = qi_ref[...]
    wi = wi_ref[:, 0:1]
    hi = lax.Precision.HIGHEST
    keys = []
    for p in range(n_pages):
        d = jnp.dot(qi, kit_pages[p][...], preferred_element_type=F32, precision=hi)
        keys.append(_ordered_key(jnp.sum(wi * jnp.maximum(d, 0.0), axis=0, keepdims=True)))
    d_new = jnp.sum(qi * kin_ref[...], axis=1, keepdims=True)
    key_new = _ordered_key(jnp.sum(wi * jnp.maximum(d_new, 0.0), axis=0, keepdims=True))

    key_all = jnp.concatenate(keys, axis=0)

    def total(x):
        return jnp.sum(jnp.sum(x, axis=0, keepdims=True), axis=1, keepdims=True)

    def count(pred):
        return jnp.where(pred(key_new), 1.0, 0.0) + total(jnp.where(pred(key_all), 1.0, 0.0))

    kcount = jnp.full((1, 1), float(n_keep), F32)
    thr = _kth_largest_key_radix(lambda c: count(lambda t: t >= c), kcount)
    need = kcount - count(lambda t: t > thr)
    eq = jnp.where(key_all == thr, 1.0, 0.0)
    upper = jnp.where(lax.broadcasted_iota(jnp.int32, (page, page), 0)
                      <= lax.broadcasted_iota(jnp.int32, (page, page), 1), 1.0, 0.0).astype(BF16)
    before = jnp.where(lax.broadcasted_iota(jnp.int32, (n_pages, n_pages), 1)
                       < lax.broadcasted_iota(jnp.int32, (n_pages, n_pages), 0), 1.0, 0.0).astype(BF16)
    per_page = jnp.broadcast_to(jnp.sum(eq, axis=1, keepdims=True), eq.shape).astype(BF16)
    rank = (jnp.dot(eq.astype(BF16), upper, preferred_element_type=F32)
            + jnp.dot(before, per_page, preferred_element_type=F32))
    keep = jnp.where(key_all > thr, 1.0, jnp.where(rank <= need, eq, 0.0))
    mask_all = jnp.where(keep > 0.5, 0.0, NEG_BIG)
    masks = [mask_all[p:p + 1, :] for p in range(n_pages)]
    keep_new = jnp.where(key_new > thr, 1.0,
                         jnp.where(key_new == thr, jnp.where(total(eq) + 1.0 <= need, 1.0, 0.0), 0.0))
    mask_new = jnp.where(keep_new > 0.5, 0.0, NEG_BIG)
    qbd = qbd_ref[...]
    qb = qbd.astype(BF16)
    logits = [jnp.dot(qb, kt_pages[p][...].astype(BF16), preferred_element_type=F32)
              + bias_ref[:, p * page:(p + 1) * page] + masks[p] for p in range(n_pages)]
    s_new = jnp.sum(qbd * knew_ref[...], axis=1, keepdims=True) + biasn_ref[:, 0:1] + mask_new
    o_ref[...] = _sample_softmax_pv(logits, s_new, vt_pages, vnew_ref[...])


def dsa_sample(q, knew, vnew, qi, wi, kinew, cache_kt, cache_vt, cache_kit, layer, page_table, tab):
    s, w = q.shape
    n_pages, page = page_table.shape[1], cache_kt.shape[3]
    past_len = n_pages * page
    n_keep = min(DSA_TOPK, (past_len + 1) // 4)
    bias_p, bias_n = _sample_bias(tab, past_len)
    row = pl.BlockSpec((None, 1, w), lambda i, pt: (i, 0, 0))
    const = lambda shape: pl.BlockSpec(shape, lambda i, pt: (0, 0))
    qis = (qi * IDX_DIM ** -0.5).reshape(s, IDX_HEADS, IDX_DIM)
    wib = jnp.broadcast_to((wi * IDX_HEADS ** -0.5)[:, :, None], (s, IDX_HEADS, LANES))
    out = pl.pallas_call(
        functools.partial(_dsa_sample_kernel, n_pages=n_pages, page=page, n_keep=n_keep),
        grid_spec=pltpu.PrefetchScalarGridSpec(
            num_scalar_prefetch=1, grid=(s,),
            in_specs=[pl.BlockSpec((None, SUBLANES, w), lambda i, pt: (i, 0, 0)), row, row,
                      const(bias_p.shape), const(bias_n.shape),
                      pl.BlockSpec((None, IDX_HEADS, IDX_DIM), lambda i, pt: (i, 0, 0)),
                      pl.BlockSpec((None, IDX_HEADS, LANES), lambda i, pt: (i, 0, 0)),
                      pl.BlockSpec((None, 1, IDX_DIM), lambda i, pt: (i, 0, 0))]
            + _page_specs(layer, n_pages, w, page) * 2 + _page_specs(layer, n_pages, IDX_DIM, page),
            out_specs=row),
        out_shape=jax.ShapeDtypeStruct((s, 1, w), F32),
        compiler_params=_cparams("arbitrary"),
        name="dsa_sample",
    )(page_table, _block_diag_q(q * HEAD_DIM ** -0.5), knew[:, None], vnew[:, None], bias_p, bias_n,
      qis, wib, kinew[:, None], *([cache_kt] * n_pages), *([cache_vt] * n_pages), *([cache_kit] * n_pages))
    return out.reshape(s, w)


N_IN = sum(IN_SPLITS)
N_IN_PAD = -(-N_IN // LANES) * LANES


def _rms_norm(x, g):
    return x * lax.rsqrt(jnp.mean(x * x, axis=-1, keepdims=True) + NORM_EPS) * g


def _split_cols(a, sizes):
    return jnp.split(a, [int(i) for i in np.cumsum(sizes)[:-1]], axis=-1)


def _causal_dwconv(x, buf, w, b):
    width = w.shape[0]
    length = x.shape[1]
    xp = jnp.concatenate([buf, x], axis=1)
    y = b + sum(xp[:, j:j + length] * w[j] for j in range(width))
    return y, xp[:, length:]


def _tokens_matmul(x, w):
    b, t, k = x.shape
    return matmul(x.reshape(b * t, k), w).reshape(b, t, w.shape[1])


def _mamba(pz, pxbc, pdt, conv_buf, h0, conv_w, conv_b, dt_bias, a_log, d_skip, norm_g):
    b, t, _ = pz.shape
    xbc, new_buf = _causal_dwconv(pxbc, conv_buf, conv_w, conv_b)
    xbc = jax.nn.silu(xbc)
    xs, bm, cm = jnp.split(xbc, [BRANCH_W, BRANCH_W + SSM_GROUPS * SSM_DSTATE], axis=-1)
    dt = jax.nn.softplus(pdt + dt_bias)
    a_neg = -jnp.exp(a_log)
    if t == 1:
        y, h_new = ssd_step(xs[:, 0], dt[:, 0], a_neg, bm[:, 0], cm[:, 0], h0)
        y = y[:, None]
    else:
        y, h_new = ssd_prompt(xs, dt, a_neg, bm, cm, h0)
    y = y + jnp.repeat(d_skip, HEAD_DIM) * xs
    return _rms_norm(y * jax.nn.silu(pz), norm_g), new_buf, h_new


def _rwkv(p, shift_prev, s0, mu, w0, w2, a0, a2, g2, k_k, k_a, r_k, ln_w, ln_b):
    b, t, _ = p.shape
    p_prev = jnp.concatenate([shift_prev[:, None], p[:, :-1]], axis=1)
    pm = p + (p_prev - p) * mu
    r, k, v, xw, xa, xg = _split_cols(pm, (BRANCH_W, BRANCH_W, BRANCH_W, RWKV_W_LORA, RWKV_A_LORA, RWKV_G_LORA))
    logw = -jax.nn.softplus(-(w0 + _tokens_matmul(jnp.tanh(xw), w2))) - 0.5
    decay = jnp.exp(-jnp.exp(logw))
    a = jax.nn.sigmoid(a0 + _tokens_matmul(xa, a2))
    g = _tokens_matmul(jax.nn.sigmoid(xg), g2)
    heads = lambda z: z.reshape(b, t, N_HEADS, HEAD_DIM)
    kk = heads(k * k_k)
    kk = (kk * lax.rsqrt(jnp.maximum(jnp.sum(kk * kk, axis=-1, keepdims=True), 1e-24))).reshape(b, t, BRANCH_W)
    k = k * (1.0 + (a - 1.0) * k_a)
    if t == 1:
        o, s_new = rwkv_step(r[:, 0], decay[:, 0], k[:, 0], v[:, 0], kk[:, 0], (kk * a)[:, 0], s0)
        o = o[:, None]
    else:
        o, s_new = rwkv_prompt(r, decay, k, v, kk, kk * a, s0)
    o = heads(o)
    mean = jnp.mean(o, axis=-1, keepdims=True)
    var = jnp.mean(jnp.square(o - mean), axis=-1, keepdims=True)
    o = ((o - mean) * lax.rsqrt(var + RWKV_GN_EPS)).reshape(b, t, BRANCH_W) * ln_w + ln_b
    bonus = jnp.sum(heads(r) * heads(k) * r_k, axis=-1, keepdims=True) * heads(v)
    return (o + bonus.reshape(b, t, BRANCH_W)) * g, p[:, -1], s_new


PROJ_ORDER = (13, 12, 0, 1, 3, 4, 5, 6, 7, 8, 9, 2, 10, 11)
MERGE_ROWS = 256


def _merge_kernel(oa_ref, ob_ref, oc_ref, od_ref, g_ref, wb_ref, wo_ref, x_ref, out_ref):
    mix = None
    for n, o_ref in enumerate((oa_ref, ob_ref, oc_ref, od_ref)):
        pr = jnp.dot(o_ref[...].astype(BF16), wb_ref[n], preferred_element_type=F32)
        term = jax.nn.sigmoid(g_ref[:, n * D_MODEL:(n + 1) * D_MODEL]) * pr
        mix = term if mix is None else mix + term
    out_ref[...] = x_ref[...] + jnp.dot(mix.astype(BF16), wo_ref[...], preferred_element_type=F32)


def merge_branches(x, branches, proj, w_branch, w_out):
    m = x.shape[0]
    tm = _pick_tile(m, MERGE_ROWS, SUBLANES)
    tok = lambda width: pl.BlockSpec((tm, width), lambda i: (i, 0))
    return pl.pallas_call(
        _merge_kernel,
        grid=(m // tm,),
        in_specs=[tok(BRANCH_W)] * N_BRANCH + [tok(N_BRANCH * D_MODEL),
                  pl.BlockSpec((N_BRANCH, BRANCH_W, D_MODEL), lambda i: (0, 0, 0)),
                  pl.BlockSpec((D_MODEL, D_MODEL), lambda i: (0, 0)), tok(D_MODEL)],
        out_specs=tok(D_MODEL),
        out_shape=jax.ShapeDtypeStruct((m, D_MODEL), F32),
        compiler_params=_cparams("parallel"),
        name="merge_branches",
    )(*branches, proj, w_branch.astype(BF16), w_out.astype(BF16), x)


FFN_ROWS = 256


def _ffn_down_kernel(a_ref, bv_ref, halo_ref, cw_ref, cb_ref, wd_ref, x_ref, out_ref):
    a = a_ref[...]
    tm = a.shape[0]
    ext = jnp.concatenate([halo_ref[...], a], axis=0)
    y = cb_ref[...] + cw_ref[FFN_CONV - 1:FFN_CONV, :] * a
    for back in range(1, FFN_CONV):
        y = y + cw_ref[FFN_CONV - 1 - back:FFN_CONV - back, :] * ext[SUBLANES - back:SUBLANES - back + tm, :]
    g = (y * jax.nn.sigmoid(y)) * bv_ref[...]
    out_ref[...] = x_ref[...] + jnp.dot(g.astype(BF16), wd_ref[...], preferred_element_type=F32)


def ffn_down(x, up, buf, conv_w, conv_b, w_down):
    b, t, d = x.shape
    tm = _pick_tile(t, FFN_ROWS, SUBLANES)
    nt = t // tm
    keep = FFN_CONV - 1
    a_tail = up.reshape(b, nt, tm, 2 * D_FF)[:, :, tm - keep:, :D_FF]
    before = jnp.concatenate([buf[:, None], a_tail[:, :-1]], axis=1)
    halo = jnp.pad(before, ((0, 0), (0, 0), (SUBLANES - keep, 0), (0, 0))).reshape(b * nt, SUBLANES, D_FF)
    m = b * t
    out = pl.pallas_call(
        _ffn_down_kernel,
        grid=(m // tm,),
        in_specs=[pl.BlockSpec((tm, D_FF), lambda i: (i, 0)), pl.BlockSpec((tm, D_FF), lambda i: (i, 1)),
                  pl.BlockSpec((None, SUBLANES, D_FF), lambda i: (i, 0, 0)),
                  pl.BlockSpec((FFN_CONV, D_FF), lambda i: (0, 0)), pl.BlockSpec((1, D_FF), lambda i: (0, 0)),
                  pl.BlockSpec((D_FF, d), lambda i: (0, 0)), pl.BlockSpec((tm, d), lambda i: (i, 0))],
        out_specs=pl.BlockSpec((tm, d), lambda i: (i, 0)),
        out_shape=jax.ShapeDtypeStruct((m, d), F32),
        compiler_params=_cparams("parallel"),
        name="ffn_down",
    )(up.reshape(m, 2 * D_FF), up.reshape(m, 2 * D_FF), halo, conv_w, conv_b[None], w_down.astype(BF16),
      x.reshape(m, d))
    return out.reshape(b, t, d), up[:, t - keep:, :D_FF]


def _layer(x, l, past, prm, attn):
    ssm_h, ssm_buf, rw_s, rw_shift, ffn_buf = past
    b, t, _ = x.shape
    w_cols = _split_cols(prm["w_in"][l], IN_SPLITS)
    w_in = jnp.concatenate([w_cols[i] for i in PROJ_ORDER] + [jnp.zeros((D_MODEL, N_IN_PAD - N_IN), F32)], axis=1)
    proj = norm_matmul(x.reshape(b * t, D_MODEL), prm["norm1_g"][l], w_in).reshape(b, t, N_IN_PAD)
    parts = dict(zip(PROJ_ORDER, _split_cols(proj[..., :N_IN], [IN_SPLITS[i] for i in PROJ_ORDER])))
    (s_z, s_xbc, s_dt, m_q, m_k, m_v, d_q, d_k, d_v, d_qi, d_ki, d_wi, p_rwkv, _) = (parts[i] for i in range(14))
    o_a, ssm_buf_new, ssm_h_new = _mamba(s_z, s_xbc, s_dt, ssm_buf, ssm_h, prm["ssm_conv_w"][l], prm["ssm_conv_b"][l],
                                         prm["ssm_dt_bias"][l], prm["ssm_A_log"][l], prm["ssm_D"][l],
                                         prm["ssm_norm_g"][l])
    o_b = attn("moba", l, m_q, m_k, m_v)
    o_c = attn("dsa", l, d_q, d_k, d_v, d_qi, d_ki, d_wi)
    o_d, rw_shift_new, rw_s_new = _rwkv(p_rwkv, rw_shift, rw_s, prm["rwkv_mu"][l], prm["rwkv_w0"][l], prm["rwkv_w2"][l],
                                        prm["rwkv_a0"][l], prm["rwkv_a2"][l], prm["rwkv_g2"][l], prm["rwkv_k_k"][l],
                                        prm["rwkv_k_a"][l], prm["rwkv_r_k"][l], prm["rwkv_ln_w"][l], prm["rwkv_ln_b"][l])
    flat = lambda z: z.reshape(b * t, z.shape[-1])
    x = merge_branches(flat(x), [flat(o) for o in (o_a, o_b, o_c, o_d)], flat(proj), prm["w_branch"][l],
                       prm["w_out"][l]).reshape(b, t, D_MODEL)
    up = norm_matmul(x.reshape(b * t, D_MODEL), prm["norm2_g"][l], prm["ffn_w_up"][l]).reshape(b, t, 2 * D_FF)
    if t >= FFN_CONV:
        x, ffn_buf_new = ffn_down(x, up, ffn_buf, prm["ffn_conv_w"][l], prm["ffn_conv_b"][l], prm["ffn_w_down"][l])
    else:
        a, bv = jnp.split(up, 2, axis=-1)
        a, ffn_buf_new = _causal_dwconv(a, ffn_buf, prm["ffn_conv_w"][l], prm["ffn_conv_b"][l])
        x = x + _tokens_matmul(jax.nn.silu(a) * bv, prm["ffn_w_down"][l])
    heads = lambda z: z.reshape(b, t, N_HEADS, HEAD_DIM)
    new = (heads(m_k), heads(m_v), heads(d_k), heads(d_v), d_ki, ssm_h_new, ssm_buf_new, rw_s_new, rw_shift_new,
           ffn_buf_new)
    return x, new


def kernel(x_prompt, x_sample, cache_moba_k, cache_moba_v, cache_dsa_k, cache_dsa_v, cache_dsa_kidx, page_table,
           state_ssm, state_ssm_conv, state_rwkv, state_rwkv_shift, state_ffn_conv,
           norm1_g, w_in, ssm_conv_w, ssm_conv_b, ssm_dt_bias, ssm_A_log, ssm_D, ssm_norm_g,
           rwkv_mu, rwkv_w0, rwkv_w2, rwkv_a0, rwkv_a2, rwkv_g2, rwkv_k_k, rwkv_k_a, rwkv_r_k, rwkv_ln_w, rwkv_ln_b,
           w_branch, w_out, norm2_g, ffn_w_up, ffn_conv_w, ffn_conv_b, ffn_w_down, rel_bias, final_g):
    prm = dict(norm1_g=norm1_g, w_in=w_in, ssm_conv_w=ssm_conv_w, ssm_conv_b=ssm_conv_b, ssm_dt_bias=ssm_dt_bias,
               ssm_A_log=ssm_A_log, ssm_D=ssm_D, ssm_norm_g=ssm_norm_g, rwkv_mu=rwkv_mu, rwkv_w0=rwkv_w0,
               rwkv_w2=rwkv_w2, rwkv_a0=rwkv_a0, rwkv_a2=rwkv_a2, rwkv_g2=rwkv_g2, rwkv_k_k=rwkv_k_k,
               rwkv_k_a=rwkv_k_a, rwkv_r_k=rwkv_r_k, rwkv_ln_w=rwkv_ln_w, rwkv_ln_b=rwkv_ln_b, w_branch=w_branch,
               w_out=w_out, norm2_g=norm2_g, ffn_w_up=ffn_w_up, ffn_conv_w=ffn_conv_w, ffn_conv_b=ffn_conv_b,
               ffn_w_down=ffn_w_down)
    depth = w_in.shape[0]
    bp = x_prompt.shape[0]
    assert x_sample.shape[1] == 1
    tab_moba, tab_dsa = rel_bias[:, :N_HEADS], rel_bias[:, N_HEADS:]
    tiles_moba, tiles_dsa = _bias_tiles(tab_moba * LOG2E), _bias_tiles(tab_dsa * LOG2E)
    ck_m, cv_m, ck_d, cv_d, cki_d = (_transposed_pages(c) for c in (cache_moba_k, cache_moba_v, cache_dsa_k,
                                                                    cache_dsa_v, cache_dsa_kidx))

    def attn_prompt(kind, l, q, k, v, qi=None, ki=None, wi=None):
        if kind == "moba":
            return moba_prompt(q, k, v, tiles_moba)
        return dsa_prompt(q, k, v, qi, ki, wi, tiles_dsa)

    def attn_sample(kind, l, q, k, v, qi=None, ki=None, wi=None):
        if kind == "moba":
            return moba_sample(q[:, 0], k[:, 0], v[:, 0], ck_m, cv_m, l, page_table, tab_moba)[:, None]
        return dsa_sample(q[:, 0], k[:, 0], v[:, 0], qi[:, 0], wi[:, 0], ki[:, 0], ck_d, cv_d, cki_d,
                          l, page_table, tab_dsa)[:, None]

    zp = lambda *shape: jnp.zeros((bp,) + shape, F32)
    prompt_past = (zp(N_HEADS, HEAD_DIM, SSM_DSTATE), zp(SSM_CONV - 1, SSM_CONV_DIM),
                   zp(N_HEADS, HEAD_DIM, HEAD_DIM), zp(RWKV_IN), zp(FFN_CONV - 1, D_FF))
    xp, xs = x_prompt, x_sample
    new_p, new_s = [], []
    for l in range(depth):
        sample_past = (state_ssm[l], state_ssm_conv[l], state_rwkv[l], state_rwkv_shift[l], state_ffn_conv[l])
        xp, np_l = _layer(xp, l, prompt_past, prm, attn_prompt)
        xs, ns_l = _layer(xs, l, sample_past, prm, attn_sample)
        new_p.append(np_l)
        new_s.append(ns_l)
    outs = [_rms_norm(xp, final_g), _rms_norm(xs, final_g)]
    for i in range(10):
        outs.append(jnp.stack([o[i] for o in new_p], axis=0))
        outs.append(jnp.stack([o[i] for o in new_s], axis=0))
    return tuple(outs)
```

```python
import functools
import math

import jax
import jax.numpy as jnp
import numpy as np
from jax import lax
from jax.experimental import pallas as pl
from jax.experimental.pallas import tpu as pltpu

D_MODEL = 1024
N_BRANCH = 4
BRANCH_W = D_MODEL // N_BRANCH
HEAD_DIM = 64
N_HEADS = BRANCH_W // HEAD_DIM
SSM_DSTATE = 128
SSM_GROUPS = 2
SSM_CONV = 4
SSM_CHUNK = 128
SSM_CONV_DIM = BRANCH_W + 2 * SSM_GROUPS * SSM_DSTATE
MOBA_BLOCK = 256
MOBA_TOPK = 3
DSA_TOPK = 256
IDX_HEADS = 8
IDX_DIM = 32
RWKV_W_LORA = 64
RWKV_A_LORA = 64
RWKV_G_LORA = 128
RWKV_IN = 3 * BRANCH_W + RWKV_W_LORA + RWKV_A_LORA + RWKV_G_LORA
RWKV_GN_EPS = 64e-5
D_FF = 11 * D_MODEL // 4
FFN_CONV = 3
NUM_BUCKETS = 32
MAX_DISTANCE = 1024
NORM_EPS = 1e-6
IN_SPLITS = (BRANCH_W, SSM_CONV_DIM, N_HEADS,
             BRANCH_W, BRANCH_W, BRANCH_W,
             BRANCH_W, BRANCH_W, BRANCH_W, IDX_HEADS * IDX_DIM, IDX_DIM, IDX_HEADS,
             RWKV_IN, N_BRANCH * D_MODEL)
F32 = jnp.float32
BF16 = jnp.bfloat16

LANES = 128
SUBLANES = 8
VMEM_LIMIT = 56 * 1024 * 1024
NEG_BIG = -1e30
LOG2E = math.log2(math.e)
PROMPT_Q_SCALE = HEAD_DIM ** -0.5 * LOG2E
Q_TILE = 128
K_TILE = MOBA_BLOCK


def _cparams(*sem):
    return pltpu.CompilerParams(dimension_semantics=sem, vmem_limit_bytes=VMEM_LIMIT)


def _mm_kernel(x_ref, w_ref, o_ref):
    o_ref[...] = jnp.dot(x_ref[...], w_ref[...], preferred_element_type=F32)


def _pick_tile(n, cap, quantum):
    if n <= cap:
        return n
    best = None
    for t in range(quantum, cap + 1, quantum):
        if n % t == 0:
            best = t
    assert best is not None, (n, cap, quantum)
    return best


def matmul(x, w, *, tm_cap=1024, tn_cap=1408):
    m, k = x.shape
    k2, n = w.shape
    assert k == k2
    x = x.astype(BF16)
    w = w.astype(BF16)
    tm = _pick_tile(m, tm_cap, SUBLANES * 2)
    tn = _pick_tile(n, tn_cap, LANES)
    return pl.pallas_call(
        _mm_kernel,
        grid=(n // tn, m // tm),
        in_specs=[pl.BlockSpec((tm, k), lambda j, i: (i, 0)),
                  pl.BlockSpec((k, tn), lambda j, i: (0, j))],
        out_specs=pl.BlockSpec((tm, tn), lambda j, i: (i, j)),
        out_shape=jax.ShapeDtypeStruct((m, n), F32),
        compiler_params=_cparams("parallel", "parallel"),
        name="matmul",
    )(x, w)


def _norm_mm_kernel(x_ref, g_ref, w_ref, o_ref, xn_scr):
    @pl.when(pl.program_id(1) == 0)
    def _():
        x = x_ref[...]
        y = x * lax.rsqrt(jnp.mean(x * x, axis=-1, keepdims=True) + NORM_EPS) * g_ref[...]
        xn_scr[...] = y.astype(BF16)

    o_ref[...] = jnp.dot(xn_scr[...], w_ref[...], preferred_element_type=F32)


def norm_matmul(x, g, w, *, tm_cap=1024, tn_cap=1408):
    m, k = x.shape
    n = w.shape[1]
    tm = _pick_tile(m, tm_cap, SUBLANES * 2)
    tn = _pick_tile(n, tn_cap, LANES)
    return pl.pallas_call(
        _norm_mm_kernel,
        grid=(m // tm, n // tn),
        in_specs=[pl.BlockSpec((tm, k), lambda i, j: (i, 0)),
                  pl.BlockSpec((1, k), lambda i, j: (0, 0)),
                  pl.BlockSpec((k, tn), lambda i, j: (0, j))],
        out_specs=pl.BlockSpec((tm, tn), lambda i, j: (i, j)),
        out_shape=jax.ShapeDtypeStruct((m, n), F32),
        scratch_shapes=[pltpu.VMEM((tm, k), BF16)],
        compiler_params=_cparams("parallel", "arbitrary"),
        name="norm_matmul",
    )(x, g[None], w.astype(BF16))


def _rel_bucket(dist):
    n = jnp.maximum(dist, 0)
    max_exact = NUM_BUCKETS // 2
    nf = jnp.maximum(n, 1).astype(F32)
    large = max_exact + (jnp.log(nf / max_exact) / math.log(MAX_DISTANCE / max_exact)
                         * (NUM_BUCKETS - max_exact)).astype(jnp.int32)
    large = jnp.minimum(large, NUM_BUCKETS - 1)
    return jnp.where(n < max_exact, n, large)


N_NEAR = -(-(MAX_DISTANCE + K_TILE - 1) // Q_TILE)


def _bias_tiles(tab):
    span = K_TILE + Q_TILE - 1
    o = jnp.arange(N_NEAR + 1, dtype=jnp.int32)[:, None] * Q_TILE
    m = jnp.arange(span, dtype=jnp.int32)[None, :]
    dist = jnp.where(o >= N_NEAR * Q_TILE, MAX_DISTANCE, o - (K_TILE - 1) + m)
    vec = _bucket_lookup(tab, _rel_bucket(dist))
    period = jnp.pad(vec, ((0, 0), (0, 0), (0, 1)))
    skew = jnp.tile(period, (1, 1, K_TILE))[..., :K_TILE * span].reshape(vec.shape[:2] + (K_TILE, span))
    return skew[..., K_TILE - 1:K_TILE - 1 + Q_TILE]


def _bucket_lookup(tab, bucket):
    hit = bucket[..., None] == jnp.arange(NUM_BUCKETS, dtype=jnp.int32)
    cols = tab.T.reshape((tab.shape[1],) + (1,) * bucket.ndim + (NUM_BUCKETS,))
    return jnp.sum(jnp.where(hit, cols, 0.0), axis=-1)


FLASH_GROUP = 4


class _FlashScratch:
    shapes = staticmethod(lambda: [
        pltpu.VMEM((FLASH_GROUP, N_HEADS, K_TILE, Q_TILE), F32),
        pltpu.VMEM((FLASH_GROUP, N_HEADS, K_TILE, Q_TILE), BF16),
        pltpu.VMEM((SUBLANES, Q_TILE), F32),
        pltpu.VMEM((SUBLANES, Q_TILE), F32),
        pltpu.VMEM((N_HEADS, HEAD_DIM, Q_TILE), F32)])

    def __init__(self, s, p, m, l, acc):
        self.s, self.p, self.m, self.l, self.acc = s, p, m, l, acc

    def init(self):
        self.m[...] = jnp.full(self.m.shape, NEG_BIG, F32)
        self.l[...] = jnp.zeros(self.l.shape, F32)
        self.acc[...] = jnp.zeros(self.acc.shape, F32)

    def write(self, o_ref):
        for p in range(N_HEADS // 2):
            pair = jnp.concatenate([self.acc[h] / self.l[h:h + 1, :] for h in (2 * p, 2 * p + 1)], axis=0)
            o_ref[:, p * LANES:(p + 1) * LANES] = pair.T


REDUCE_CHAINS = 8


def _fold_rows(x, op):
    r, c = x.shape
    x = x.reshape(REDUCE_CHAINS, r // (REDUCE_CHAINS * SUBLANES), SUBLANES, c)
    return op(op(x, axis=1), axis=0)


def _flash_group(k_ref, vt_ref, bias_ref, qpair, tiles, fs):
    for u, (k0, _, _, _) in enumerate(tiles):
        for p in range(N_HEADS // 2):
            kt = k_ref[pl.ds(k0, K_TILE), p * LANES:(p + 1) * LANES]
            s2 = jnp.dot(kt, qpair[p], preferred_element_type=F32)
            fs.s[u, 2 * p] = s2[:, :Q_TILE]
            fs.s[u, 2 * p + 1] = s2[:, Q_TILE:]
    for u, (k0, off_idx, mask_tile, mask_rows) in enumerate(tiles):
        for h in range(N_HEADS):
            s = fs.s[u, h] + bias_ref[h, off_idx]
            if mask_tile is not None:
                s = s + mask_tile
            m_old = fs.m[h:h + 1, :]
            m_tile = jnp.max(_fold_rows(s, jnp.max), axis=0, keepdims=True)
            if mask_rows is not None:
                m_tile = m_tile + mask_rows[h]
            m_new = jnp.maximum(m_old, m_tile)
            shift = m_new if mask_rows is None else m_new - mask_rows[h]
            p = jnp.exp2(s - shift)
            alpha = jnp.exp2(m_old - m_new)
            fs.l[h:h + 1, :] = alpha * fs.l[h:h + 1, :] + jnp.sum(_fold_rows(p, jnp.sum), axis=0, keepdims=True)
            fs.m[h:h + 1, :] = m_new
            fs.p[u, h] = p.astype(BF16)
            vt = vt_ref[h * HEAD_DIM:(h + 1) * HEAD_DIM, pl.ds(k0, K_TILE)]
            fs.acc[h] = alpha * fs.acc[h] + jnp.dot(vt, fs.p[u, h], preferred_element_type=F32)


def _padded_queries(q_ref):
    row = lax.broadcasted_iota(jnp.int32, (LANES, Q_TILE), 0)
    out = []
    for p in range(N_HEADS // 2):
        qt = (q_ref[:, p * LANES:(p + 1) * LANES] * PROMPT_Q_SCALE).T
        out += [jnp.where(row < HEAD_DIM, qt, 0.0), jnp.where(row >= HEAD_DIM, qt, 0.0)]
    return out


def _query_pairs(qpad):
    return [jnp.concatenate([qpad[2 * p].astype(BF16), qpad[2 * p + 1].astype(BF16)], axis=1)
            for p in range(N_HEADS // 2)]


def _causal_mask(q0, k0):
    row = lax.broadcasted_iota(jnp.int32, (K_TILE, Q_TILE), 0)
    lane = lax.broadcasted_iota(jnp.int32, (K_TILE, Q_TILE), 1)
    return jnp.where(k0 + row <= q0 + lane, 0.0, NEG_BIG).astype(F32)


def _top_blocks_mask(score, n_valid, n_sel, axis):
    n = score.shape[axis]
    idx = lax.broadcasted_iota(jnp.int32, score.shape, axis)
    low = jnp.float32(-3.0e38)
    s = jnp.where(idx < n_valid, score, low)
    mask = jnp.full(score.shape, NEG_BIG, F32)
    for _ in range(n_sel):
        mx = jnp.max(s, axis=axis, keepdims=True)
        first = jnp.min(jnp.where(s == mx, idx, n), axis=axis, keepdims=True)
        pick = idx == jnp.where(mx > low, first, -1)
        mask = jnp.where(pick, 0.0, mask)
        s = jnp.where(pick, low, s)
    return mask


def _moba_prompt_kernel(q_ref, k_ref, vt_ref, kmean_ref, bias_ref, o_ref, sel_scr, *flash_scr):
    qi = pl.program_id(1)
    q0 = qi * Q_TILE
    own = q0 // K_TILE
    nb = kmean_ref.shape[1]
    fs = _FlashScratch(*flash_scr)
    fs.init()
    qpad = _padded_queries(q_ref)
    for h in range(N_HEADS):
        score = jnp.dot(kmean_ref[h], qpad[h], preferred_element_type=F32,
                        precision=lax.Precision.HIGHEST)
        sel_scr[h] = _top_blocks_mask(score, own, min(MOBA_TOPK, nb), 0)
    qpair = _query_pairs(qpad)
    k_own = pl.multiple_of(own * K_TILE, K_TILE)
    _flash_group(k_ref, vt_ref, bias_ref, qpair,
                 [(k_own, (q0 - k_own) // Q_TILE, _causal_mask(q0, k_own), None)], fs)

    def body(g, _):
        tiles = []
        for u in range(FLASH_GROUP):
            kj = g * FLASH_GROUP + u
            k0 = pl.multiple_of(kj * K_TILE, K_TILE)
            off = jnp.clip((q0 - k0) // Q_TILE, 0, N_NEAR)
            tiles.append((k0, off, None, [sel_scr[h, pl.ds(kj, 1), :] for h in range(N_HEADS)]))
        _flash_group(k_ref, vt_ref, bias_ref, qpair, tiles, fs)
        return 0

    lax.fori_loop(0, (own + FLASH_GROUP - 1) // FLASH_GROUP, body, 0)
    fs.write(o_ref)


def _pad_heads_t(q):
    b, t, _ = q.shape
    qt = jnp.transpose(q.reshape(b, t, N_HEADS, HEAD_DIM), (0, 2, 3, 1))
    z = jnp.zeros_like(qt)
    even = jnp.concatenate([qt, z], axis=2)
    odd = jnp.concatenate([z, qt], axis=2)
    is_even = (jnp.arange(N_HEADS) % 2 == 0)[None, :, None, None]
    return jnp.where(is_even, even, odd)


def _block_mean_kernel(k_ref, o_ref):
    o_ref[...] = jnp.mean(k_ref[...], axis=0, keepdims=True)


def block_mean(k):
    b, t, c = k.shape
    nb = t // MOBA_BLOCK
    out = pl.pallas_call(
        _block_mean_kernel,
        grid=(b, nb),
        in_specs=[pl.BlockSpec((None, MOBA_BLOCK, c), lambda i, j: (i, j, 0))],
        out_specs=pl.BlockSpec((None, None, 1, c), lambda i, j: (i, j, 0, 0)),
        out_shape=jax.ShapeDtypeStruct((b, nb, 1, c), F32),
        compiler_params=_cparams("parallel", "parallel"),
        name="block_mean",
    )(k)
    return out.reshape(b, nb, c)


def moba_prompt(q, k, v, bias_tiles):
    b, t, w = q.shape
    assert t % (FLASH_GROUP * K_TILE) == 0 and K_TILE % Q_TILE == 0
    nb = t // MOBA_BLOCK
    kmean = block_mean(k)
    kmp = jnp.transpose(_pad_heads_t(kmean), (0, 1, 3, 2))
    kb = k.astype(BF16)
    vtb = jnp.transpose(v, (0, 2, 1)).astype(BF16)
    return pl.pallas_call(
        _moba_prompt_kernel,
        grid=(b, t // Q_TILE),
        in_specs=[pl.BlockSpec((None, Q_TILE, w), lambda i, j: (i, j, 0)),
                  pl.BlockSpec((None, t, w), lambda i, j: (i, 0, 0)),
                  pl.BlockSpec((None, w, t), lambda i, j: (i, 0, 0)),
                  pl.BlockSpec((None, N_HEADS, nb, LANES), lambda i, j: (i, 0, 0, 0)),
                  pl.BlockSpec(bias_tiles.shape, lambda i, j: (0, 0, 0, 0))],
        out_specs=pl.BlockSpec((None, Q_TILE, w), lambda i, j: (i, j, 0)),
        out_shape=jax.ShapeDtypeStruct((b, t, w), F32),
        scratch_shapes=[pltpu.VMEM((N_HEADS, nb, Q_TILE), F32)] + _FlashScratch.shapes(),
        compiler_params=_cparams("parallel", "arbitrary"),
        name="moba_prompt",
    )(q, kb, vtb, kmp, bias_tiles)


def _split_bf16(x):
    hi = x.astype(BF16)
    lo = (x - hi.astype(F32)).astype(BF16)
    return hi, lo


def _ordered_key(x):
    x = jnp.where(x == 0.0, 0.0, x)
    bits = lax.bitcast_convert_type(x, jnp.int32)
    return bits ^ ((bits >> 31) & jnp.int32(0x7FFFFFFF))


INT_MIN = -2 ** 31


def _kth_largest_key(count_ge, kcount):
    int_min = jnp.int32(INT_MIN)
    thr = jnp.where(count_ge(jnp.zeros_like(kcount, jnp.int32)) >= kcount, jnp.int32(0), int_min)

    def bit_body(i, thr):
        cand = thr | lax.shift_left(jnp.int32(1), 30 - i)
        return jnp.where(count_ge(cand) >= kcount, cand, thr)

    return lax.fori_loop(0, 31, bit_body, thr)


RADIX_BITS = 4


def _kth_largest_key_radix(count_ge, kcount):
    thr = jnp.full(kcount.shape, INT_MIN, jnp.int32)
    for shift in range(32 - RADIX_BITS, -1, -RADIX_BITS):
        digit = jnp.zeros(kcount.shape, jnp.int32)
        for j in range(1, 2 ** RADIX_BITS):
            cand = thr + jnp.int32(np.array(j << shift, np.uint32).astype(np.int32))
            digit = digit + jnp.where(count_ge(cand) >= kcount, 1, 0)
        thr = thr + lax.shift_left(digit, shift)
    return thr


def _dsa_prompt_kernel(q_ref, k_ref, vt_ref, kic_ref, qic_ref, wi_ref, bias_ref, o_ref,
                       key_scr, mask_scr, *flash_scr, n_keep):
    qi = pl.program_id(1)
    q0 = qi * Q_TILE
    own = q0 // K_TILE
    n_tiles = own + 1
    qpos = q0 + lax.broadcasted_iota(jnp.int32, (1, Q_TILE), 1)
    row = lax.broadcasted_iota(jnp.int32, (K_TILE, Q_TILE), 0)
    fs = _FlashScratch(*flash_scr)
    fs.init()

    def tile_start(kj):
        return pl.multiple_of(kj * K_TILE, K_TILE)

    n_pairs = (n_tiles + 1) // 2

    def score_body(kp, _):
        for u in range(2):
            k0 = tile_start(2 * kp + u)
            kic = kic_ref[pl.ds(k0, K_TILE), :]
            sc = jnp.zeros((K_TILE, Q_TILE), F32)
            for j in range(IDX_HEADS // 2):
                q2 = jnp.concatenate([qic_ref[2 * j], qic_ref[2 * j + 1]], axis=1)
                d = jnp.maximum(jnp.dot(kic, q2, preferred_element_type=F32), 0.0)
                sc = sc + wi_ref[2 * j:2 * j + 1, :] * d[:, :Q_TILE]
                sc = sc + wi_ref[2 * j + 1:2 * j + 2, :] * d[:, Q_TILE:]
            key = jnp.where(k0 + row <= qpos, _ordered_key(sc), jnp.int32(INT_MIN))
            key_scr[pl.ds(k0, K_TILE), :] = key
        return 0

    lax.fori_loop(0, n_pairs, score_body, 0)

    def count(pred):
        def body(kp, c):
            t = key_scr[pl.ds(pl.multiple_of(kp * 2 * K_TILE, 2 * K_TILE), 2 * K_TILE), :]
            return c + _fold_rows(jnp.where(pred(t), 1.0, 0.0), jnp.sum)
        c = lax.fori_loop(0, n_pairs, body, jnp.zeros((SUBLANES, Q_TILE), F32))
        return jnp.sum(c, axis=0, keepdims=True)

    kcount = jnp.minimum(n_keep, qpos + 1).astype(F32)
    thr = _kth_largest_key(lambda c: count(lambda t: t >= c), kcount)
    need = kcount - count(lambda t: t > thr)
    tri = jnp.where(lax.broadcasted_iota(jnp.int32, (K_TILE, K_TILE), 0)
                    >= lax.broadcasted_iota(jnp.int32, (K_TILE, K_TILE), 1), 1.0, 0.0).astype(BF16)

    def mask_body(kp, seen):
        for u in range(2):
            k0 = tile_start(2 * kp + u)
            t = key_scr[pl.ds(k0, K_TILE), :]
            eq = jnp.where(t == thr, 1.0, 0.0)
            rank = jnp.dot(tri, eq.astype(BF16), preferred_element_type=F32) + seen
            keep = jnp.where(t > thr, 1.0, jnp.where(rank <= need, eq, 0.0))
            mask_scr[pl.ds(k0, K_TILE), :] = jnp.where(keep > 0.5, 0.0, NEG_BIG).astype(BF16)
            seen = seen + jnp.sum(_fold_rows(eq, jnp.sum), axis=0, keepdims=True)
        return seen

    lax.fori_loop(0, n_pairs, mask_body, jnp.zeros((1, Q_TILE), F32))
    qpair = _query_pairs(_padded_queries(q_ref))

    @pl.when(n_pairs % 2 == 1)
    def _():
        mask_scr[pl.ds(tile_start(2 * n_pairs), 2 * K_TILE), :] = jnp.full((2 * K_TILE, Q_TILE), NEG_BIG, BF16)

    def body(g, _):
        tiles = []
        for u in range(FLASH_GROUP):
            k0 = tile_start(g * FLASH_GROUP + u)
            off = jnp.clip((q0 - k0) // Q_TILE, 0, N_NEAR)
            tiles.append((k0, off, mask_scr[pl.ds(k0, K_TILE), :].astype(F32), None))
        _flash_group(k_ref, vt_ref, bias_ref, qpair, tiles, fs)
        return 0

    lax.fori_loop(0, (n_pairs + 1) // 2, body, 0)
    fs.write(o_ref)


def dsa_prompt(q, k, v, qi, ki, wi, bias_tiles):
    b, t, w = q.shape
    assert t % (FLASH_GROUP * K_TILE) == 0 and K_TILE % Q_TILE == 0 and FLASH_GROUP == 4
    n_keep = min(DSA_TOPK, t // 4)
    kb = k.astype(BF16)
    vtb = jnp.transpose(v, (0, 2, 1)).astype(BF16)
    ki_hi, ki_lo = _split_bf16(ki)
    kic = jnp.concatenate([ki_hi, ki_hi, ki_lo], axis=-1)
    qis = jnp.transpose((qi * IDX_DIM ** -0.5).reshape(b, t, IDX_HEADS, IDX_DIM), (0, 2, 3, 1))
    qi_hi, qi_lo = _split_bf16(qis)
    qic = jnp.concatenate([qi_hi, qi_lo, qi_hi], axis=2)
    wit = jnp.transpose(wi * IDX_HEADS ** -0.5, (0, 2, 1))
    return pl.pallas_call(
        functools.partial(_dsa_prompt_kernel, n_keep=n_keep),
        grid=(b, t // Q_TILE),
        in_specs=[pl.BlockSpec((None, Q_TILE, w), lambda i, j: (i, j, 0)),
                  pl.BlockSpec((None, t, w), lambda i, j: (i, 0, 0)),
                  pl.BlockSpec((None, w, t), lambda i, j: (i, 0, 0)),
                  pl.BlockSpec((None, t, 3 * IDX_DIM), lambda i, j: (i, 0, 0)),
                  pl.BlockSpec((None, IDX_HEADS, 3 * IDX_DIM, Q_TILE), lambda i, j: (i, 0, 0, j)),
                  pl.BlockSpec((None, IDX_HEADS, Q_TILE), lambda i, j: (i, 0, j)),
                  pl.BlockSpec(bias_tiles.shape, lambda i, j: (0, 0, 0, 0))],
        out_specs=pl.BlockSpec((None, Q_TILE, w), lambda i, j: (i, j, 0)),
        out_shape=jax.ShapeDtypeStruct((b, t, w), F32),
        scratch_shapes=[pltpu.VMEM((t, Q_TILE), jnp.int32), pltpu.VMEM((t, Q_TILE), BF16)]
        + _FlashScratch.shapes(),
        compiler_params=_cparams("parallel", "arbitrary"),
        name="dsa_prompt",
    )(q, kb, vtb, kic, qic, wit, bias_tiles)


HEADS_PER_GROUP = N_HEADS // SSM_GROUPS
PAIR_W = HEADS_PER_GROUP * HEAD_DIM
assert PAIR_W == LANES and SSM_DSTATE == LANES and SSM_CHUNK == LANES


def _ssd_prompt_kernel(xdt_ref, b_ref, c_ref, dax_ref, dat_ref, h0_ref, y_ref, hout_ref, st_scr):
    ci = pl.program_id(1)

    @pl.when(ci == 0)
    def _():
        st_scr[...] = h0_ref[...]

    q = SSM_CHUNK
    row = lax.broadcasted_iota(jnp.int32, (q, q), 0)
    col = lax.broadcasted_iota(jnp.int32, (q, q), 1)
    lower = row >= col
    hi = lax.Precision.HIGHEST
    cs_x = jnp.dot(jnp.where(lower, 1.0, 0.0), dax_ref[...], preferred_element_type=F32, precision=hi)
    cs_t = jnp.dot(dat_ref[...], jnp.where(row <= col, 1.0, 0.0), preferred_element_type=F32, precision=hi)
    lane_lo = col < HEAD_DIM
    for g in range(SSM_GROUPS):
        sl = slice(g * PAIR_W, (g + 1) * PAIR_W)
        cg = c_ref[:, g * SSM_DSTATE:(g + 1) * SSM_DSTATE]
        bg = b_ref[:, g * SSM_DSTATE:(g + 1) * SSM_DSTATE]
        xdt = xdt_ref[:, sl]
        cs_pair = cs_x[:, sl]
        cs_swap = pltpu.roll(cs_pair, HEAD_DIM, 1)
        cb = lax.dot_general(cg, bg, (((1,), (1,)), ((), ())), preferred_element_type=F32)
        y_pair = []
        for a in range(HEADS_PER_GROUP):
            h = g * HEADS_PER_GROUP + a
            own_half = lane_lo if a == 0 else jnp.logical_not(lane_lo)
            cs_col = jnp.where(own_half, cs_pair, cs_swap)
            seg = jnp.exp(jnp.minimum(cs_col - cs_t[h:h + 1, :], 0.0))
            scores = cb * jnp.where(lower, seg, 0.0)
            y_pair.append(jnp.dot(scores, xdt, preferred_element_type=F32))
        y_diag = jnp.where(lane_lo, y_pair[0], y_pair[1])
        st = st_scr[sl, :]
        y_off = lax.dot_general(cg, st, (((1,), (1,)), ((), ())), preferred_element_type=F32)
        y_ref[:, sl] = y_diag + y_off * jnp.exp(cs_pair)
        cs_last = cs_pair[q - 1:q, :]
        xdtd = xdt * jnp.exp(cs_last - cs_pair)
        upd = jnp.dot(xdtd.T, bg, preferred_element_type=F32)
        last_a = cs_last[:, 0:1]
        last_b = cs_last[:, HEAD_DIM:HEAD_DIM + 1]
        st_scr[sl, :] = st * jnp.exp(jnp.where(row < HEAD_DIM, last_a, last_b)) + upd

    @pl.when(ci == pl.num_programs(1) - 1)
    def _():
        hout_ref[...] = st_scr[...]


def ssd_prompt(xs, dt, a_neg, bm, cm, h0):
    b, t, w = xs.shape
    assert t % SSM_CHUNK == 0
    rep = lambda z: jnp.repeat(z, HEAD_DIM, axis=-1)
    xdt = xs * rep(dt)
    da = dt * a_neg
    dat = jnp.pad(jnp.transpose(da, (0, 2, 1)), ((0, 0), (0, SUBLANES - N_HEADS), (0, 0)))
    hp = N_HEADS * HEAD_DIM
    chunk = lambda width: pl.BlockSpec((None, SSM_CHUNK, width), lambda i, j: (i, j, 0))
    state = pl.BlockSpec((None, hp, SSM_DSTATE), lambda i, j: (i, 0, 0))
    y, h = pl.pallas_call(
        _ssd_prompt_kernel,
        grid=(b, t // SSM_CHUNK),
        in_specs=[chunk(w), chunk(bm.shape[-1]), chunk(cm.shape[-1]), chunk(w),
                  pl.BlockSpec((None, SUBLANES, SSM_CHUNK), lambda i, j: (i, 0, j)), state],
        out_specs=[chunk(w), state],
        out_shape=[jax.ShapeDtypeStruct((b, t, w), F32), jax.ShapeDtypeStruct((b, hp, SSM_DSTATE), F32)],
        scratch_shapes=[pltpu.VMEM((hp, SSM_DSTATE), F32)],
        compiler_params=_cparams("parallel", "arbitrary"),
        name="ssd_prompt",
    )(xdt, bm, cm, rep(da), dat, h0.reshape(b, hp, SSM_DSTATE))
    return y, h.reshape(b, N_HEADS, HEAD_DIM, SSM_DSTATE)


RWKV_CHUNK = 128
HEAD_PAIRS = N_HEADS // 2


def _split_cat(x):
    hi, lo = _split_bf16(x)
    return jnp.concatenate([hi, lo], axis=1)


def _rwkv_prompt_kernel(r_ref, w_ref, k_ref, kk_ref, kka_ref, v_ref, g_ref, c1_ref, c2_ref, s0_ref,
                        o_ref, sout_ref, s_scr, z_scr, vt_scr, ot_scr):
    ci = pl.program_id(0)
    n_b = r_ref.shape[0]
    tiles = [(b, p) for b in range(n_b) for p in range(HEAD_PAIRS)]
    lanes = lambda p: slice(p * LANES, (p + 1) * LANES)

    r2 = lax.broadcasted_iota(jnp.int32, (2 * LANES, LANES), 0)
    c2 = lax.broadcasted_iota(jnp.int32, (2 * LANES, LANES), 1)
    ones_same = jnp.where((r2 % LANES) // HEAD_DIM == c2 // HEAD_DIM, 1.0, 0.0).astype(BF16)
    r3 = lax.broadcasted_iota(jnp.int32, (LANES, 2 * LANES), 0)
    c3 = lax.broadcasted_iota(jnp.int32, (LANES, 2 * LANES), 1)
    ones_out = jnp.where(r3 // HEAD_DIM == c3 // LANES, 1.0, 0.0).astype(BF16)

    def key_sums(xs):
        out = jnp.dot(jnp.concatenate([_split_cat(x) for x in xs], axis=0), ones_same, preferred_element_type=F32)
        return [out[i * HEAD_DIM:(i + 1) * HEAD_DIM] for i in range(len(xs))]

    @pl.when(ci == 0)
    def _():
        s_scr[...] = s0_ref[...]
        z0 = key_sums([s0_ref[i] * kk_ref[b, 0:1, lanes(p)] for i, (b, p) in enumerate(tiles)])
        for i in range(len(tiles)):
            z_scr[i] = z0[i]

    for i, (b, p) in enumerate(tiles):
        vt_scr[i] = v_ref[b, :, lanes(p)].T
    lane_t = lax.broadcasted_iota(jnp.int32, (1, RWKV_CHUNK), 1)
    head_a = lax.broadcasted_iota(jnp.int32, (HEAD_DIM, LANES), 1) < HEAD_DIM

    def step(t, row):
        at_t = lane_t == t
        states = [s_scr[i] for i in range(len(tiles))]
        ahead = key_sums([s * row(g_ref, b, p) for s, (b, p) in zip(states, tiles)])
        new_states = []
        for i, (b, p) in enumerate(tiles):
            z = z_scr[i]
            v_a = jnp.sum(jnp.where(at_t, vt_scr[i, :HEAD_DIM, :], 0.0), axis=1, keepdims=True)
            v_b = jnp.sum(jnp.where(at_t, vt_scr[i, HEAD_DIM:, :], 0.0), axis=1, keepdims=True)
            v_bc = jnp.where(head_a, v_a, v_b)
            s = states[i] * row(w_ref, b, p) - z * row(kka_ref, b, p) + v_bc * row(k_ref, b, p)
            s_scr[i] = s
            new_states.append(s)
            z_scr[i] = ahead[i] - z * row(c1_ref, b, p) + v_bc * row(c2_ref, b, p)
        y = jnp.concatenate([(s * row(r_ref, b, p)).astype(BF16) for s, (b, p) in zip(new_states, tiles)], axis=0)
        o_all = jnp.dot(y, ones_out, preferred_element_type=F32)
        col_t = jnp.broadcast_to(at_t, (HEAD_DIM, RWKV_CHUNK))
        for i in range(len(tiles)):
            o = o_all[i * HEAD_DIM:(i + 1) * HEAD_DIM]
            pltpu.store(ot_scr.at[i, pl.ds(0, HEAD_DIM), :], o[:, :LANES], mask=col_t)
            pltpu.store(ot_scr.at[i, pl.ds(HEAD_DIM, HEAD_DIM), :], o[:, LANES:], mask=col_t)

    def step_group(g, _):
        base = pl.multiple_of(g * SUBLANES, SUBLANES)
        refs = (r_ref, w_ref, k_ref, kka_ref, g_ref, c1_ref, c2_ref)
        blocks = {(id(ref), b, p): ref[b, pl.ds(base, SUBLANES), lanes(p)] for ref in refs for b, p in tiles}
        for j in range(SUBLANES):
            step(base + j, lambda ref, b, p, j=j: blocks[(id(ref), b, p)][j:j + 1, :])
        return 0

    lax.fori_loop(0, RWKV_CHUNK // SUBLANES, step_group, 0)
    for i, (b, p) in enumerate(tiles):
        o_ref[b, :, lanes(p)] = ot_scr[i].T

    @pl.when(ci == pl.num_programs(0) - 1)
    def _():
        sout_ref[...] = s_scr[...]


def rwkv_prompt(r, w, k, v, kk, kka, s0):
    b, t, hw = r.shape
    assert t % RWKV_CHUNK == 0 and RWKV_CHUNK == LANES and 2 * HEAD_DIM == LANES
    n_tiles = b * HEAD_PAIRS
    pack = lambda s: jnp.transpose(s.reshape(b, HEAD_PAIRS, 2, HEAD_DIM, HEAD_DIM), (0, 1, 3, 2, 4)).reshape(
        n_tiles, HEAD_DIM, LANES)
    unpack = lambda s: jnp.transpose(s.reshape(b, HEAD_PAIRS, HEAD_DIM, 2, HEAD_DIM), (0, 1, 3, 2, 4)).reshape(
        b, N_HEADS, HEAD_DIM, HEAD_DIM)
    kk_next = jnp.concatenate([kk[:, 1:], jnp.zeros_like(kk[:, :1])], axis=1)
    head_dot = lambda x: jnp.repeat(jnp.sum((x * kk_next).reshape(b, t, N_HEADS, HEAD_DIM), axis=-1), HEAD_DIM, axis=-1)
    seq_spec = pl.BlockSpec((b, RWKV_CHUNK, hw), lambda c: (0, c, 0))
    st_spec = pl.BlockSpec((n_tiles, HEAD_DIM, LANES), lambda c: (0, 0, 0))
    o, s = pl.pallas_call(
        _rwkv_prompt_kernel,
        grid=(t // RWKV_CHUNK,),
        in_specs=[seq_spec] * 9 + [st_spec],
        out_specs=[seq_spec, st_spec],
        out_shape=[jax.ShapeDtypeStruct((b, t, hw), F32), jax.ShapeDtypeStruct((n_tiles, HEAD_DIM, LANES), F32)],
        scratch_shapes=[pltpu.VMEM((n_tiles, HEAD_DIM, LANES), F32),
                        pltpu.VMEM((n_tiles, HEAD_DIM, LANES), F32),
                        pltpu.VMEM((n_tiles, LANES, RWKV_CHUNK), F32),
                        pltpu.VMEM((n_tiles, LANES, RWKV_CHUNK), F32)],
        compiler_params=_cparams("arbitrary"),
        name="rwkv_prompt",
    )(r, w, k, kk, kka, v, w * kk_next, head_dot(kka), head_dot(k), pack(s0))
    return o, unpack(s)


STEP_SEQS = 8
NT_DIMS = (((1,), (1,)), ((), ()))


def _row_dot(row, mat):
    r8 = jnp.broadcast_to(row, (SUBLANES, row.shape[1]))
    return lax.dot_general(r8, mat, NT_DIMS, preferred_element_type=F32,
                           precision=lax.Precision.HIGHEST)[0:1, :]


def _ssd_step_kernel(h0_ref, xdtb_ref, dab_ref, b_ref, c_ref, h_ref, y_ref):
    for s in range(h0_ref.shape[0]):
        for h in range(N_HEADS):
            hn = h0_ref[s, h] * jnp.exp(dab_ref[s, h]) + xdtb_ref[s, h] * b_ref[s, h]
            h_ref[s, h] = hn
            y_ref[s, h] = _row_dot(c_ref[s, h], hn)


def ssd_step(xs, dt, a_neg, bm, cm, h0):
    s = xs.shape[0]
    assert s % STEP_SEQS == 0
    xdtb = jnp.broadcast_to((xs.reshape(s, N_HEADS, HEAD_DIM) * dt[:, :, None])[..., None],
                            (s, N_HEADS, HEAD_DIM, SSM_DSTATE))
    dab = jnp.broadcast_to((dt * a_neg)[:, :, None, None], (s, N_HEADS, 1, SSM_DSTATE))
    per_head = lambda z: jnp.repeat(z.reshape(s, SSM_GROUPS, 1, SSM_DSTATE), HEADS_PER_GROUP, axis=1)
    big = pl.BlockSpec((STEP_SEQS, N_HEADS, HEAD_DIM, SSM_DSTATE), lambda i: (i, 0, 0, 0))
    row = pl.BlockSpec((STEP_SEQS, N_HEADS, 1, SSM_DSTATE), lambda i: (i, 0, 0, 0))
    h, y = pl.pallas_call(
        _ssd_step_kernel,
        grid=(s // STEP_SEQS,),
        in_specs=[big, big, row, row, row],
        out_specs=[big, pl.BlockSpec((STEP_SEQS, N_HEADS, 1, HEAD_DIM), lambda i: (i, 0, 0, 0))],
        out_shape=[jax.ShapeDtypeStruct(h0.shape, F32), jax.ShapeDtypeStruct((s, N_HEADS, 1, HEAD_DIM), F32)],
        compiler_params=_cparams("parallel"),
        name="ssd_step",
    )(h0, xdtb, dab, per_head(bm), per_head(cm))
    return y.reshape(s, N_HEADS * HEAD_DIM), h


def _rwkv_step_kernel(s0_ref, vb_ref, w_ref, kk_ref, kka_ref, k_ref, r_ref, s_ref, o_ref):
    for s in range(s0_ref.shape[0]):
        for h in range(N_HEADS):
            st = s0_ref[s, h]
            sa = -jnp.sum(st * kk_ref[s, h], axis=1, keepdims=True)
            st = st * w_ref[s, h] + sa * kka_ref[s, h] + vb_ref[s, h] * k_ref[s, h]
            s_ref[s, h] = st
            o_ref[s, h] = _row_dot(r_ref[s, h], st)


def rwkv_step(r, w, k, v, kk, kka, s0):
    s = r.shape[0]
    assert s % STEP_SEQS == 0
    rows = lambda z: z.reshape(s, N_HEADS, 1, HEAD_DIM)
    vb = jnp.broadcast_to(v.reshape(s, N_HEADS, HEAD_DIM, 1), (s, N_HEADS, HEAD_DIM, HEAD_DIM))
    big = pl.BlockSpec((STEP_SEQS, N_HEADS, HEAD_DIM, HEAD_DIM), lambda i: (i, 0, 0, 0))
    row = pl.BlockSpec((STEP_SEQS, N_HEADS, 1, HEAD_DIM), lambda i: (i, 0, 0, 0))
    st, o = pl.pallas_call(
        _rwkv_step_kernel,
        grid=(s // STEP_SEQS,),
        in_specs=[big, big, row, row, row, row, row],
        out_specs=[big, row],
        out_shape=[jax.ShapeDtypeStruct(s0.shape, F32), jax.ShapeDtypeStruct((s, N_HEADS, 1, HEAD_DIM), F32)],
        compiler_params=_cparams("parallel"),
        name="rwkv_step",
    )(s0, vb, rows(w), rows(kk), rows(kka), rows(k), rows(r))
    return o.reshape(s, N_HEADS * HEAD_DIM), st


def _head_rows_to_row(o):
    r = lax.broadcasted_iota(jnp.int32, o.shape, 0)
    c = lax.broadcasted_iota(jnp.int32, o.shape, 1)
    return jnp.sum(jnp.where(c // HEAD_DIM == r, o, 0.0), axis=0, keepdims=True)


def _sample_softmax_pv(logit_pages, s_new, vt_pages, vnew):
    m = s_new
    for s in logit_pages:
        m = jnp.maximum(m, jnp.max(s, axis=1, keepdims=True))
    p_new = jnp.exp(s_new - m)
    l = p_new
    acc = p_new * vnew
    for s, vt_ref in zip(logit_pages, vt_pages):
        p = jnp.exp(s - m)
        l = l + jnp.sum(p, axis=1, keepdims=True)
        acc = acc + lax.dot_general(p.astype(BF16), vt_ref[...].astype(BF16), NT_DIMS, preferred_element_type=F32)
    return _head_rows_to_row(acc / l)


def _moba_sample_kernel(pt_ref, qbd_ref, knew_ref, vnew_ref, bias_ref, biasn_ref, *rest, n_pages, page):
    del pt_ref
    kt_pages, vt_pages, o_ref = rest[:n_pages], rest[n_pages:2 * n_pages], rest[2 * n_pages]
    qbd = qbd_ref[...]
    q_hi, q_lo = _split_bf16(qbd)
    q_both = jnp.concatenate([q_hi, q_lo], axis=0)
    ppb = MOBA_BLOCK // page
    n_blocks = n_pages // ppb
    lane = lax.broadcasted_iota(jnp.int32, (SUBLANES, LANES), 1)
    raw = []
    score = jnp.zeros((SUBLANES, LANES), F32)
    for p in range(n_pages):
        k_hi, k_lo = _split_bf16(kt_pages[p][...])
        r = jnp.dot(q_both, k_hi, preferred_element_type=F32)
        r = r[:SUBLANES] + r[SUBLANES:] + jnp.dot(q_hi, k_lo, preferred_element_type=F32)
        raw.append(r)
        score = score + jnp.where(lane == p // ppb, jnp.sum(r, axis=1, keepdims=True) * (1.0 / MOBA_BLOCK), 0.0)
    sel = _top_blocks_mask(score, n_blocks, min(MOBA_TOPK, n_blocks + 1), 1)
    logits = [raw[p] + bias_ref[:, p * page:(p + 1) * page] + sel[:, p // ppb:p // ppb + 1]
              for p in range(n_pages)]
    s_new = jnp.sum(qbd * knew_ref[...], axis=1, keepdims=True) + biasn_ref[:, 0:1]
    o_ref[...] = _sample_softmax_pv(logits, s_new, vt_pages, vnew_ref[...])


def _block_diag_q(q):
    s = q.shape[0]
    eye = jnp.eye(N_HEADS, dtype=q.dtype)
    bd = (q.reshape(s, 1, N_HEADS, HEAD_DIM) * eye[None, :, :, None]).reshape(s, N_HEADS, N_HEADS * HEAD_DIM)
    return jnp.pad(bd, ((0, 0), (0, SUBLANES - N_HEADS), (0, 0)))


def _sample_bias(tab, past_len):
    dist = past_len - jnp.arange(past_len, dtype=jnp.int32)
    pad = ((0, SUBLANES - N_HEADS), (0, 0))
    past = jnp.pad(_bucket_lookup(tab, _rel_bucket(dist)), pad)
    new = jnp.pad(_bucket_lookup(tab, _rel_bucket(jnp.zeros((LANES,), jnp.int32))), pad)
    return past, new


def _page_specs(layer, n_pages, rows, page):
    return [pl.BlockSpec((None, None, rows, page), functools.partial(
        lambda s, pt, p: (layer, pt[s, p], 0, 0), p=p)) for p in range(n_pages)]


def _transposed_pages(cache):
    c = cache.reshape(cache.shape[:3] + (-1,))
    return jnp.transpose(c, (0, 1, 3, 2))


def moba_sample(q, knew, vnew, cache_kt, cache_vt, layer, page_table, tab):
    s, w = q.shape
    n_pages, page = page_table.shape[1], cache_kt.shape[3]
    past_len = n_pages * page
    assert past_len % MOBA_BLOCK == 0 and MOBA_BLOCK % page == 0
    bias_p, bias_n = _sample_bias(tab, past_len)
    row = pl.BlockSpec((None, 1, w), lambda i, pt: (i, 0, 0))
    const = lambda shape: pl.BlockSpec(shape, lambda i, pt: (0, 0))
    out = pl.pallas_call(
        functools.partial(_moba_sample_kernel, n_pages=n_pages, page=page),
        grid_spec=pltpu.PrefetchScalarGridSpec(
            num_scalar_prefetch=1, grid=(s,),
            in_specs=[pl.BlockSpec((None, SUBLANES, w), lambda i, pt: (i, 0, 0)), row, row,
                      const(bias_p.shape), const(bias_n.shape)]
            + _page_specs(layer, n_pages, w, page) * 2,
            out_specs=row),
        out_shape=jax.ShapeDtypeStruct((s, 1, w), F32),
        compiler_params=_cparams("arbitrary"),
        name="moba_sample",
    )(page_table, _block_diag_q(q * HEAD_DIM ** -0.5), knew[:, None], vnew[:, None], bias_p, bias_n,
      *([cache_kt] * n_pages), *([cache_vt] * n_pages))
    return out.reshape(s, w)


def _dsa_sample_kernel(pt_ref, qbd_ref, knew_ref, vnew_ref, bias_ref, biasn_ref, qi_ref, wi_ref, kin_ref,
                       *rest, n_pages, page, n_keep):
    del pt_ref
    kt_pages, vt_pages = rest[:n_pages], rest[n_pages:2 * n_pages]
    kit_pages, o_ref = rest[2 * n_pages:3 * n_pages], rest[3 * n_pages]
    qi = qi_ref[...]
    wi = wi_ref[:, 0:1]
    hi = lax.Precision.HIGHEST
    keys = []
    for p in range(n_pages):
        d = jnp.dot(qi, kit_pages[p][...], preferred_element_type=F32, precision=hi)
        keys.append(_ordered_key(jnp.sum(wi * jnp.maximum(d, 0.0), axis=0, keepdims=True)))
    d_new = jnp.sum(qi * kin_ref[...], axis=1, keepdims=True)
    key_new = _ordered_key(jnp.sum(wi * jnp.maximum(d_new, 0.0), axis=0, keepdims=True))

    key_all = jnp.concatenate(keys, axis=0)

    def total(x):
        return jnp.sum(jnp.sum(x, axis=0, keepdims=True), axis=1, keepdims=True)

    def count(pred):
        return jnp.where(pred(key_new), 1.0, 0.0) + total(jnp.where(pred(key_all), 1.0, 0.0))

    kcount = jnp.full((1, 1), float(n_keep), F32)
    thr = _kth_largest_key_radix(lambda c: count(lambda t: t >= c), kcount)
    need = kcount - count(lambda t: t > thr)
    eq = jnp.where(key_all == thr, 1.0, 0.0)
    upper = jnp.where(lax.broadcasted_iota(jnp.int32, (page, page), 0)
                      <= lax.broadcasted_iota(jnp.int32, (page, page), 1), 1.0, 0.0).astype(BF16)
    before = jnp.where(lax.broadcasted_iota(jnp.int32, (n_pages, n_pages), 1)
                       < lax.broadcasted_iota(jnp.int32, (n_pages, n_pages), 0), 1.0, 0.0).astype(BF16)
    per_page = jnp.broadcast_to(jnp.sum(eq, axis=1, keepdims=True), eq.shape).astype(BF16)
    rank = (jnp.dot(eq.astype(BF16), upper, preferred_element_type=F32)
            + jnp.dot(before, per_page, preferred_element_type=F32))
    keep = jnp.where(key_all > thr, 1.0, jnp.where(rank <= need, eq, 0.0))
    mask_all = jnp.where(keep > 0.5, 0.0, NEG_BIG)
    masks = [mask_all[p:p + 1, :] for p in range(n_pages)]
    keep_new = jnp.where(key_new > thr, 1.0,
                         jnp.where(key_new == thr, jnp.where(total(eq) + 1.0 <= need, 1.0, 0.0), 0.0))
    mask_new = jnp.where(keep_new > 0.5, 0.0, NEG_BIG)
    qbd = qbd_ref[...]
    qb = qbd.astype(BF16)
    logits = [jnp.dot(qb, kt_pages[p][...].astype(BF16), preferred_element_type=F32)
              + bias_ref[:, p * page:(p + 1) * page] + masks[p] for p in range(n_pages)]
    s_new = jnp.sum(qbd * knew_ref[...], axis=1, keepdims=True) + biasn_ref[:, 0:1] + mask_new
    o_ref[...] = _sample_softmax_pv(logits, s_new, vt_pages, vnew_ref[...])


def dsa_sample(q, knew, vnew, qi, wi, kinew, cache_kt, cache_vt, cache_kit, layer, page_table, tab):
    s, w = q.shape
    n_pages, page = page_table.shape[1], cache_kt.shape[3]
    past_len = n_pages * page
    n_keep = min(DSA_TOPK, (past_len + 1) // 4)
    bias_p, bias_n = _sample_bias(tab, past_len)
    row = pl.BlockSpec((None, 1, w), lambda i, pt: (i, 0, 0))
    const = lambda shape: pl.BlockSpec(shape, lambda i, pt: (0, 0))
    qis = (qi * IDX_DIM ** -0.5).reshape(s, IDX_HEADS, IDX_DIM)
    wib = jnp.broadcast_to((wi * IDX_HEADS ** -0.5)[:, :, None], (s, IDX_HEADS, LANES))
    out = pl.pallas_call(
        functools.partial(_dsa_sample_kernel, n_pages=n_pages, page=page, n_keep=n_keep),
        grid_spec=pltpu.PrefetchScalarGridSpec(
            num_scalar_prefetch=1, grid=(s,),
            in_specs=[pl.BlockSpec((None, SUBLANES, w), lambda i, pt: (i, 0, 0)), row, row,
                      const(bias_p.shape), const(bias_n.shape),
                      pl.BlockSpec((None, IDX_HEADS, IDX_DIM), lambda i, pt: (i, 0, 0)),
                      pl.BlockSpec((None, IDX_HEADS, LANES), lambda i, pt: (i, 0, 0)),
                      pl.BlockSpec((None, 1, IDX_DIM), lambda i, pt: (i, 0, 0))]
            + _page_specs(layer, n_pages, w, page) * 2 + _page_specs(layer, n_pages, IDX_DIM, page),
            out_specs=row),
        out_shape=jax.ShapeDtypeStruct((s, 1, w), F32),
        compiler_params=_cparams("arbitrary"),
        name="dsa_sample",
    )(page_table, _block_diag_q(q * HEAD_DIM ** -0.5), knew[:, None], vnew[:, None], bias_p, bias_n,
      qis, wib, kinew[:, None], *([cache_kt] * n_pages), *([cache_vt] * n_pages), *([cache_kit] * n_pages))
    return out.reshape(s, w)


N_IN = sum(IN_SPLITS)
N_IN_PAD = -(-N_IN // LANES) * LANES


def _rms_norm(x, g):
    return x * lax.rsqrt(jnp.mean(x * x, axis=-1, keepdims=True) + NORM_EPS) * g


def _split_cols(a, sizes):
    return jnp.split(a, [int(i) for i in np.cumsum(sizes)[:-1]], axis=-1)


def _causal_dwconv(x, buf, w, b):
    width = w.shape[0]
    length = x.shape[1]
    xp = jnp.concatenate([buf, x], axis=1)
    y = b + sum(xp[:, j:j + length] * w[j] for j in range(width))
    return y, xp[:, length:]


def _tokens_matmul(x, w):
    b, t, k = x.shape
    return matmul(x.reshape(b * t, k), w).reshape(b, t, w.shape[1])


def _mamba(pz, pxbc, pdt, conv_buf, h0, conv_w, conv_b, dt_bias, a_log, d_skip, norm_g):
    b, t, _ = pz.shape
    xbc, new_buf = _causal_dwconv(pxbc, conv_buf, conv_w, conv_b)
    xbc = jax.nn.silu(xbc)
    xs, bm, cm = jnp.split(xbc, [BRANCH_W, BRANCH_W + SSM_GROUPS * SSM_DSTATE], axis=-1)
    dt = jax.nn.softplus(pdt + dt_bias)
    a_neg = -jnp.exp(a_log)
    if t == 1:
        y, h_new = ssd_step(xs[:, 0], dt[:, 0], a_neg, bm[:, 0], cm[:, 0], h0)
        y = y[:, None]
    else:
        y, h_new = ssd_prompt(xs, dt, a_neg, bm, cm, h0)
    y = y + jnp.repeat(d_skip, HEAD_DIM) * xs
    return _rms_norm(y * jax.nn.silu(pz), norm_g), new_buf, h_new


def _rwkv(p, shift_prev, s0, mu, w0, w2, a0, a2, g2, k_k, k_a, r_k, ln_w, ln_b):
    b, t, _ = p.shape
    p_prev = jnp.concatenate([shift_prev[:, None], p[:, :-1]], axis=1)
    pm = p + (p_prev - p) * mu
    r, k, v, xw, xa, xg = _split_cols(pm, (BRANCH_W, BRANCH_W, BRANCH_W, RWKV_W_LORA, RWKV_A_LORA, RWKV_G_LORA))
    logw = -jax.nn.softplus(-(w0 + _tokens_matmul(jnp.tanh(xw), w2))) - 0.5
    decay = jnp.exp(-jnp.exp(logw))
    a = jax.nn.sigmoid(a0 + _tokens_matmul(xa, a2))
    g = _tokens_matmul(jax.nn.sigmoid(xg), g2)
    heads = lambda z: z.reshape(b, t, N_HEADS, HEAD_DIM)
    kk = heads(k * k_k)
    kk = (kk * lax.rsqrt(jnp.maximum(jnp.sum(kk * kk, axis=-1, keepdims=True), 1e-24))).reshape(b, t, BRANCH_W)
    k = k * (1.0 + (a - 1.0) * k_a)
    if t == 1:
        o, s_new = rwkv_step(r[:, 0], decay[:, 0], k[:, 0], v[:, 0], kk[:, 0], (kk * a)[:, 0], s0)
        o = o[:, None]
    else:
        o, s_new = rwkv_prompt(r, decay, k, v, kk, kk * a, s0)
    o = heads(o)
    mean = jnp.mean(o, axis=-1, keepdims=True)
    var = jnp.mean(jnp.square(o - mean), axis=-1, keepdims=True)
    o = ((o - mean) * lax.rsqrt(var + RWKV_GN_EPS)).reshape(b, t, BRANCH_W) * ln_w + ln_b
    bonus = jnp.sum(heads(r) * heads(k) * r_k, axis=-1, keepdims=True) * heads(v)
    return (o + bonus.reshape(b, t, BRANCH_W)) * g, p[:, -1], s_new


PROJ_ORDER = (13, 12, 0, 1, 3, 4, 5, 6, 7, 8, 9, 2, 10, 11)
MERGE_ROWS = 256


def _merge_kernel(oa_ref, ob_ref, oc_ref, od_ref, g_ref, wb_ref, wo_ref, x_ref, out_ref):
    mix = None
    for n, o_ref in enumerate((oa_ref, ob_ref, oc_ref, od_ref)):
        pr = jnp.dot(o_ref[...].astype(BF16), wb_ref[n], preferred_element_type=F32)
        term = jax.nn.sigmoid(g_ref[:, n * D_MODEL:(n + 1) * D_MODEL]) * pr
        mix = term if mix is None else mix + term
    out_ref[...] = x_ref[...] + jnp.dot(mix.astype(BF16), wo_ref[...], preferred_element_type=F32)


def merge_branches(x, branches, proj, w_branch, w_out):
    m = x.shape[0]
    tm = _pick_tile(m, MERGE_ROWS, SUBLANES)
    tok = lambda width: pl.BlockSpec((tm, width), lambda i: (i, 0))
    return pl.pallas_call(
        _merge_kernel,
        grid=(m // tm,),
        in_specs=[tok(BRANCH_W)] * N_BRANCH + [tok(N_BRANCH * D_MODEL),
                  pl.BlockSpec((N_BRANCH, BRANCH_W, D_MODEL), lambda i: (0, 0, 0)),
                  pl.BlockSpec((D_MODEL, D_MODEL), lambda i: (0, 0)), tok(D_MODEL)],
        out_specs=tok(D_MODEL),
        out_shape=jax.ShapeDtypeStruct((m, D_MODEL), F32),
        compiler_params=_cparams("parallel"),
        name="merge_branches",
    )(*branches, proj, w_branch.astype(BF16), w_out.astype(BF16), x)


FFN_ROWS = 256


def _ffn_down_kernel(a_ref, bv_ref, halo_ref, cw_ref, cb_ref, wd_ref, x_ref, out_ref):
    a = a_ref[...]
    tm = a.shape[0]
    ext = jnp.concatenate([halo_ref[...], a], axis=0)
    y = cb_ref[...] + cw_ref[FFN_CONV - 1:FFN_CONV, :] * a
    for back in range(1, FFN_CONV):
        y = y + cw_ref[FFN_CONV - 1 - back:FFN_CONV - back, :] * ext[SUBLANES - back:SUBLANES - back + tm, :]
    g = (y * jax.nn.sigmoid(y)) * bv_ref[...]
    out_ref[...] = x_ref[...] + jnp.dot(g.astype(BF16), wd_ref[...], preferred_element_type=F32)


def ffn_down(x, up, buf, conv_w, conv_b, w_down):
    b, t, d = x.shape
    tm = _pick_tile(t, FFN_ROWS, SUBLANES)
    nt = t // tm
    keep = FFN_CONV - 1
    a_tail = up.reshape(b, nt, tm, 2 * D_FF)[:, :, tm - keep:, :D_FF]
    before = jnp.concatenate([buf[:, None], a_tail[:, :-1]], axis=1)
    halo = jnp.pad(before, ((0, 0), (0, 0), (SUBLANES - keep, 0), (0, 0))).reshape(b * nt, SUBLANES, D_FF)
    m = b * t
    out = pl.pallas_call(
        _ffn_down_kernel,
        grid=(m // tm,),
        in_specs=[pl.BlockSpec((tm, D_FF), lambda i: (i, 0)), pl.BlockSpec((tm, D_FF), lambda i: (i, 1)),
                  pl.BlockSpec((None, SUBLANES, D_FF), lambda i: (i, 0, 0)),
                  pl.BlockSpec((FFN_CONV, D_FF), lambda i: (0, 0)), pl.BlockSpec((1, D_FF), lambda i: (0, 0)),
                  pl.BlockSpec((D_FF, d), lambda i: (0, 0)), pl.BlockSpec((tm, d), lambda i: (i, 0))],
        out_specs=pl.BlockSpec((tm, d), lambda i: (i, 0)),
        out_shape=jax.ShapeDtypeStruct((m, d), F32),
        compiler_params=_cparams("parallel"),
        name="ffn_down",
    )(up.reshape(m, 2 * D_FF), up.reshape(m, 2 * D_FF), halo, conv_w, conv_b[None], w_down.astype(BF16),
      x.reshape(m, d))
    return out.reshape(b, t, d), up[:, t - keep:, :D_FF]


def _layer(x, l, past, prm, attn):
    ssm_h, ssm_buf, rw_s, rw_shift, ffn_buf = past
    b, t, _ = x.shape
    w_cols = _split_cols(prm["w_in"][l], IN_SPLITS)
    w_in = jnp.concatenate([w_cols[i] for i in PROJ_ORDER] + [jnp.zeros((D_MODEL, N_IN_PAD - N_IN), F32)], axis=1)
    proj = norm_matmul(x.reshape(b * t, D_MODEL), prm["norm1_g"][l], w_in).reshape(b, t, N_IN_PAD)
    parts = dict(zip(PROJ_ORDER, _split_cols(proj[..., :N_IN], [IN_SPLITS[i] for i in PROJ_ORDER])))
    (s_z, s_xbc, s_dt, m_q, m_k, m_v, d_q, d_k, d_v, d_qi, d_ki, d_wi, p_rwkv, _) = (parts[i] for i in range(14))
    o_a, ssm_buf_new, ssm_h_new = _mamba(s_z, s_xbc, s_dt, ssm_buf, ssm_h, prm["ssm_conv_w"][l], prm["ssm_conv_b"][l],
                                         prm["ssm_dt_bias"][l], prm["ssm_A_log"][l], prm["ssm_D"][l],
                                         prm["ssm_norm_g"][l])
    o_b = attn("moba", l, m_q, m_k, m_v)
    o_c = attn("dsa", l, d_q, d_k, d_v, d_qi, d_ki, d_wi)
    o_d, rw_shift_new, rw_s_new = _rwkv(p_rwkv, rw_shift, rw_s, prm["rwkv_mu"][l], prm["rwkv_w0"][l], prm["rwkv_w2"][l],
                                        prm["rwkv_a0"][l], prm["rwkv_a2"][l], prm["rwkv_g2"][l], prm["rwkv_k_k"][l],
                                        prm["rwkv_k_a"][l], prm["rwkv_r_k"][l], prm["rwkv_ln_w"][l], prm["rwkv_ln_b"][l])
    flat = lambda z: z.reshape(b * t, z.shape[-1])
    x = merge_branches(flat(x), [flat(o) for o in (o_a, o_b, o_c, o_d)], flat(proj), prm["w_branch"][l],
                       prm["w_out"][l]).reshape(b, t, D_MODEL)
    up = norm_matmul(x.reshape(b * t, D_MODEL), prm["norm2_g"][l], prm["ffn_w_up"][l]).reshape(b, t, 2 * D_FF)
    if t >= FFN_CONV:
        x, ffn_buf_new = ffn_down(x, up, ffn_buf, prm["ffn_conv_w"][l], prm["ffn_conv_b"][l], prm["ffn_w_down"][l])
    else:
        a, bv = jnp.split(up, 2, axis=-1)
        a, ffn_buf_new = _causal_dwconv(a, ffn_buf, prm["ffn_conv_w"][l], prm["ffn_conv_b"][l])
        x = x + _tokens_matmul(jax.nn.silu(a) * bv, prm["ffn_w_down"][l])
    heads = lambda z: z.reshape(b, t, N_HEADS, HEAD_DIM)
    new = (heads(m_k), heads(m_v), heads(d_k), heads(d_v), d_ki, ssm_h_new, ssm_buf_new, rw_s_new, rw_shift_new,
           ffn_buf_new)
    return x, new


def kernel(x_prompt, x_sample, cache_moba_k, cache_moba_v, cache_dsa_k, cache_dsa_v, cache_dsa_kidx, page_table,
           state_ssm, state_ssm_conv, state_rwkv, state_rwkv_shift, state_ffn_conv,
           norm1_g, w_in, ssm_conv_w, ssm_conv_b, ssm_dt_bias, ssm_A_log, ssm_D, ssm_norm_g,
           rwkv_mu, rwkv_w0, rwkv_w2, rwkv_a0, rwkv_a2, rwkv_g2, rwkv_k_k, rwkv_k_a, rwkv_r_k, rwkv_ln_w, rwkv_ln_b,
           w_branch, w_out, norm2_g, ffn_w_up, ffn_conv_w, ffn_conv_b, ffn_w_down, rel_bias, final_g):
    prm = dict(norm1_g=norm1_g, w_in=w_in, ssm_conv_w=ssm_conv_w, ssm_conv_b=ssm_conv_b, ssm_dt_bias=ssm_dt_bias,
               ssm_A_log=ssm_A_log, ssm_D=ssm_D, ssm_norm_g=ssm_norm_g, rwkv_mu=rwkv_mu, rwkv_w0=rwkv_w0,
               rwkv_w2=rwkv_w2, rwkv_a0=rwkv_a0, rwkv_a2=rwkv_a2, rwkv_g2=rwkv_g2, rwkv_k_k=rwkv_k_k,
               rwkv_k_a=rwkv_k_a, rwkv_r_k=rwkv_r_k, rwkv_ln_w=rwkv_ln_w, rwkv_ln_b=rwkv_ln_b, w_branch=w_branch,
               w_out=w_out, norm2_g=norm2_g, ffn_w_up=ffn_w_up, ffn_conv_w=ffn_conv_w, ffn_conv_b=ffn_conv_b,
               ffn_w_down=ffn_w_down)
    depth = w_in.shape[0]
    bp = x_prompt.shape[0]
    assert x_sample.shape[1] == 1
    tab_moba, tab_dsa = rel_bias[:, :N_HEADS], rel_bias[:, N_HEADS:]
    tiles_moba, tiles_dsa = _bias_tiles(tab_moba * LOG2E), _bias_tiles(tab_dsa * LOG2E)
    ck_m, cv_m, ck_d, cv_d, cki_d = (_transposed_pages(c) for c in (cache_moba_k, cache_moba_v, cache_dsa_k,
                                                                    cache_dsa_v, cache_dsa_kidx))

    def attn_prompt(kind, l, q, k, v, qi=None, ki=None, wi=None):
        if kind == "moba":
            return moba_prompt(q, k, v, tiles_moba)
        return dsa_prompt(q, k, v, qi, ki, wi, tiles_dsa)

    def attn_sample(kind, l, q, k, v, qi=None, ki=None, wi=None):
        if kind == "moba":
            return moba_sample(q[:, 0], k[:, 0], v[:, 0], ck_m, cv_m, l, page_table, tab_moba)[:, None]
        return dsa_sample(q[:, 0], k[:, 0], v[:, 0], qi[:, 0], wi[:, 0], ki[:, 0], ck_d, cv_d, cki_d,
                          l, page_table, tab_dsa)[:, None]

    zp = lambda *shape: jnp.zeros((bp,) + shape, F32)
    prompt_past = (zp(N_HEADS, HEAD_DIM, SSM_DSTATE), zp(SSM_CONV - 1, SSM_CONV_DIM),
                   zp(N_HEADS, HEAD_DIM, HEAD_DIM), zp(RWKV_IN), zp(FFN_CONV - 1, D_FF))
    xp, xs = x_prompt, x_sample
    new_p, new_s = [], []
    for l in range(depth):
        sample_past = (state_ssm[l], state_ssm_conv[l], state_rwkv[l], state_rwkv_shift[l], state_ffn_conv[l])
        xp, np_l = _layer(xp, l, prompt_past, prm, attn_prompt)
        xs, ns_l = _layer(xs, l, sample_past, prm, attn_sample)
        new_p.append(np_l)
        new_s.append(ns_l)
    outs = [_rms_norm(xp, final_g), _rms_norm(xs, final_g)]
    for i in range(10):
        outs.append(jnp.stack([o[i] for o in new_p], axis=0))
        outs.append(jnp.stack([o[i] for o in new_s], axis=0))
    return tuple(outs)
```
